```python
import jax, jax.numpy as jnp
from jax import lax
import numpy as np

D_MODEL = 1024
BATCH = 8
SEQ = 4096
DEPTH = 4

N_META = 16
D_MIX = D_MODEL
LRU_WIDTH = D_MIX // 2
LRU_HEADS = 8
LRU_HEAD_DIM = LRU_WIDTH // LRU_HEADS
CONV_WIDTH = 4
LRU_C = 8.0
MLA_HEADS = 8
QK_NOPE = 64
QK_ROPE = 32
V_DIM = (D_MIX - LRU_WIDTH) // MLA_HEADS
Q_LORA = 3 * D_MODEL // 8
KV_LORA = D_MODEL // 4
IN_COLS = 2 * LRU_WIDTH + Q_LORA + KV_LORA + QK_ROPE
D_FF = 11 * D_MODEL // 4
ROPE_THETA = 10000.0
Q_BLOCK = 128
EPS = 1e-6

kernel_name = 'hymba_rglru_mla_macaron_sandwich'


def rms_norm(x, g):
    xf = x.astype(jnp.float32)
    y = xf * lax.rsqrt(jnp.mean(xf * xf, axis=-1, keepdims=True) + EPS)
    return (y * g.astype(jnp.float32)).astype(x.dtype)


def swiglu(u, w_gate, w_up, w_down):
    return (jax.nn.silu(u @ w_gate) * (u @ w_up)) @ w_down


def rope_tables(T):
    pos = jnp.arange(T, dtype=jnp.float32)
    inv_freq = 1.0 / (ROPE_THETA ** (jnp.arange(0, QK_ROPE, 2, dtype=jnp.float32) / QK_ROPE))
    ang = pos[:, None] * inv_freq[None, :]
    return jnp.cos(ang), jnp.sin(ang)


def apply_rope(x, cos, sin):
    x1, x2 = jnp.split(x, 2, axis=-1)
    cos = cos.astype(x.dtype)
    sin = sin.astype(x.dtype)
    return jnp.concatenate([x1 * cos - x2 * sin, x2 * cos + x1 * sin], axis=-1)


def _lru_combine(left, right):
    a_l, b_l = left
    a_r, b_r = right
    return a_l * a_r, a_r * b_l + b_r


def rglru_group(xr, gr, conv_w, conv_b, w_a, b_a, w_x, b_x, lam):
    B, T, W = xr.shape
    xp = jnp.pad(xr, ((0, 0), (CONV_WIDTH - 1, 0), (0, 0)))
    xc = xp[:, CONV_WIDTH - 1:] * conv_w[CONV_WIDTH - 1] + conv_b
    for k in range(CONV_WIDTH - 1):
        xc = xc + xp[:, k:k + T] * conv_w[k]
    xh = xc.reshape(B, T, LRU_HEADS, LRU_HEAD_DIM)
    r = jax.nn.sigmoid(jnp.einsum('bthi,hij->bthj', xh, w_a).reshape(B, T, W) + b_a)
    i = jax.nn.sigmoid(jnp.einsum('bthi,hij->bthj', xh, w_x).reshape(B, T, W) + b_x)
    log_a = -LRU_C * r.astype(jnp.float32) * jax.nn.softplus(-lam.astype(jnp.float32))
    a = jnp.exp(log_a)
    b = jnp.sqrt(-jnp.expm1(2.0 * log_a)) * (i * xc).astype(jnp.float32)
    _, h = lax.associative_scan(_lru_combine, (a, b), axis=1)
    return h.astype(xr.dtype) * jax.nn.gelu(gr)


def causal_block_attention(q, k, v):
    B, T, H, Dqk = q.shape
    n_blk = -(-T // Q_BLOCK)
    pad = n_blk * Q_BLOCK - T
    qb = jnp.pad(q, ((0, 0), (0, pad), (0, 0), (0, 0)))
    qb = qb.reshape(B, n_blk, Q_BLOCK, H, Dqk).transpose(1, 0, 2, 3, 4)
    scale = Dqk ** -0.5
    kpos = jnp.arange(T)

    def one_block(args):
        q_blk, blk = args
        s = jnp.einsum('bqhd,bkhd->bhqk', q_blk, k).astype(jnp.float32) * scale
        qpos = blk * Q_BLOCK + jnp.arange(Q_BLOCK)
        s = jnp.where(kpos[None, :] <= qpos[:, None], s, -jnp.inf)
        p = jax.nn.softmax(s, axis=-1).astype(v.dtype)
        return jnp.einsum('bhqk,bkhd->bqhd', p, v)

    o = lax.map(one_block, (qb, jnp.arange(n_blk)))
    o = o.transpose(1, 0, 2, 3, 4).reshape(B, n_blk * Q_BLOCK, H, v.shape[-1])
    return o[:, :T]


def mla_group(cq, ckv, kr, cos, sin, q_norm_g, w_uq, kv_norm_g, w_ukv):
    B, T, _ = cq.shape
    q = (rms_norm(cq, q_norm_g) @ w_uq).reshape(B, T, MLA_HEADS, QK_NOPE + QK_ROPE)
    q_nope, q_rope = jnp.split(q, [QK_NOPE], axis=-1)
    q_rope = apply_rope(q_rope, cos[None, :, None, :], sin[None, :, None, :])
    kv = (rms_norm(ckv, kv_norm_g) @ w_ukv).reshape(B, T, MLA_HEADS, QK_NOPE + V_DIM)
    k_nope, v = jnp.split(kv, [QK_NOPE], axis=-1)
    k_rope = apply_rope(kr, cos[None], sin[None])
    k_rope = jnp.broadcast_to(k_rope[:, :, None, :], (B, T, MLA_HEADS, QK_ROPE))
    qf = jnp.concatenate([q_nope, q_rope], axis=-1)
    kf = jnp.concatenate([k_nope, k_rope], axis=-1)
    return causal_block_attention(qf, kf, v).reshape(B, T, MLA_HEADS * V_DIM)


def _fwd_setup_inputs(seed: int = 0) -> dict:
    key = jax.random.key(seed)
    ks = jax.random.split(key, 32)

    def nrm(k, shape, scale):
        return jax.random.normal(k, shape, jnp.float32) * scale

    def gain(k, n):
        return 1.0 + 0.02 * jax.random.normal(k, (DEPTH, n), jnp.float32)

    a0 = jax.random.uniform(ks[15], (DEPTH, LRU_WIDTH), jnp.float32, 0.9, 0.999)
    lam = jnp.log(a0) - jnp.log1p(-a0)
    return {
        'x': nrm(ks[0], (BATCH, SEQ, D_MODEL), 1.0),
        'meta_tokens': nrm(ks[1], (N_META, D_MODEL), 1.0),
        'ffn1_pre_g': gain(ks[2], D_MODEL),
        'ffn1_w_gate': nrm(ks[3], (DEPTH, D_MODEL, D_FF), D_MODEL ** -0.5),
        'ffn1_w_up': nrm(ks[4], (DEPTH, D_MODEL, D_FF), D_MODEL ** -0.5),
        'ffn1_w_down': nrm(ks[5], (DEPTH, D_FF, D_MODEL), D_FF ** -0.5),
        'ffn1_post_g': gain(ks[6], D_MODEL),
        'mix_pre_g': gain(ks[7], D_MODEL),
        'w_in': nrm(ks[8], (DEPTH, D_MODEL, IN_COLS), D_MODEL ** -0.5),
        'lru_conv_w': nrm(ks[9], (DEPTH, CONV_WIDTH, LRU_WIDTH), CONV_WIDTH ** -0.5),
        'lru_conv_b': nrm(ks[10], (DEPTH, LRU_WIDTH), 0.01),
        'lru_w_a': nrm(ks[11], (DEPTH, LRU_HEADS, LRU_HEAD_DIM, LRU_HEAD_DIM), LRU_HEAD_DIM ** -0.5),
        'lru_b_a': nrm(ks[12], (DEPTH, LRU_WIDTH), 0.01),
        'lru_w_x': nrm(ks[13], (DEPTH, LRU_HEADS, LRU_HEAD_DIM, LRU_HEAD_DIM), LRU_HEAD_DIM ** -0.5),
        'lru_b_x': nrm(ks[14], (DEPTH, LRU_WIDTH), 0.01),
        'lru_lambda': lam,
        'mla_q_norm_g': gain(ks[16], Q_LORA),
        'mla_w_uq': nrm(ks[17], (DEPTH, Q_LORA, MLA_HEADS * (QK_NOPE + QK_ROPE)), Q_LORA ** -0.5),
        'mla_kv_norm_g': gain(ks[18], KV_LORA),
        'mla_w_ukv': nrm(ks[19], (DEPTH, KV_LORA, MLA_HEADS * (QK_NOPE + V_DIM)), KV_LORA ** -0.5),
        'lru_out_g': gain(ks[20], LRU_WIDTH),
        'mla_out_g': gain(ks[21], MLA_HEADS * V_DIM),
        'w_out': nrm(ks[22], (DEPTH, D_MIX, D_MODEL), D_MIX ** -0.5),
        'mix_post_g': gain(ks[23], D_MODEL),
        'ffn2_pre_g': gain(ks[24], D_MODEL),
        'ffn2_w_gate': nrm(ks[25], (DEPTH, D_MODEL, D_FF), D_MODEL ** -0.5),
        'ffn2_w_up': nrm(ks[26], (DEPTH, D_MODEL, D_FF), D_MODEL ** -0.5),
        'ffn2_w_down': nrm(ks[27], (DEPTH, D_FF, D_MODEL), D_FF ** -0.5),
        'ffn2_post_g': gain(ks[28], D_MODEL),
    }


def _fwd_reference(x, meta_tokens, ffn1_pre_g, ffn1_w_gate, ffn1_w_up, ffn1_w_down, ffn1_post_g,
              mix_pre_g, w_in, lru_conv_w, lru_conv_b, lru_w_a, lru_b_a, lru_w_x, lru_b_x,
              lru_lambda, mla_q_norm_g, mla_w_uq, mla_kv_norm_g, mla_w_ukv, lru_out_g,
              mla_out_g, w_out, mix_post_g, ffn2_pre_g, ffn2_w_gate, ffn2_w_up, ffn2_w_down,
              ffn2_post_g):
    B = x.shape[0]
    meta = jnp.broadcast_to(meta_tokens.astype(x.dtype)[None], (B, N_META, D_MODEL))
    h = jnp.concatenate([meta, x], axis=1)
    T = h.shape[1]
    cos, sin = rope_tables(T)
    splits = [LRU_WIDTH, 2 * LRU_WIDTH, 2 * LRU_WIDTH + Q_LORA, 2 * LRU_WIDTH + Q_LORA + KV_LORA]
    for l in range(DEPTH):
        f = swiglu(rms_norm(h, ffn1_pre_g[l]), ffn1_w_gate[l], ffn1_w_up[l], ffn1_w_down[l])
        h = h + 0.5 * rms_norm(f, ffn1_post_g[l])
        z = rms_norm(h, mix_pre_g[l]) @ w_in[l]
        xr, gr, cq, ckv, kr = jnp.split(z, splits, axis=-1)
        y_lru = rglru_group(xr, gr, lru_conv_w[l], lru_conv_b[l], lru_w_a[l], lru_b_a[l],
                            lru_w_x[l], lru_b_x[l], lru_lambda[l])
        y_mla = mla_group(cq, ckv, kr, cos, sin, mla_q_norm_g[l], mla_w_uq[l],
                          mla_kv_norm_g[l], mla_w_ukv[l])
        y = jnp.concatenate([rms_norm(y_lru, lru_out_g[l]), rms_norm(y_mla, mla_out_g[l])],
                            axis=-1) @ w_out[l]
        h = h + rms_norm(y, mix_post_g[l])
        f = swiglu(rms_norm(h, ffn2_pre_g[l]), ffn2_w_gate[l], ffn2_w_up[l], ffn2_w_down[l])
        h = h + 0.5 * rms_norm(f, ffn2_post_g[l])
    return h[:, N_META:]


import jax as _jax
import jax.numpy as _jnp

TWIN_FORMAT = 'train_step'
FWD_PARAMS = ['x', 'meta_tokens', 'ffn1_pre_g', 'ffn1_w_gate', 'ffn1_w_up', 'ffn1_w_down', 'ffn1_post_g', 'mix_pre_g', 'w_in', 'lru_conv_w', 'lru_conv_b', 'lru_w_a', 'lru_b_a', 'lru_w_x', 'lru_b_x', 'lru_lambda', 'mla_q_norm_g', 'mla_w_uq', 'mla_kv_norm_g', 'mla_w_ukv', 'lru_out_g', 'mla_out_g', 'w_out', 'mix_post_g', 'ffn2_pre_g', 'ffn2_w_gate', 'ffn2_w_up', 'ffn2_w_down', 'ffn2_post_g']
TWIN_WEIGHTS = ['meta_tokens', 'ffn1_pre_g', 'ffn1_w_gate', 'ffn1_w_up', 'ffn1_w_down', 'ffn1_post_g', 'mix_pre_g', 'w_in', 'lru_conv_w', 'lru_conv_b', 'lru_w_a', 'lru_b_a', 'lru_w_x', 'lru_b_x', 'lru_lambda', 'mla_q_norm_g', 'mla_w_uq', 'mla_kv_norm_g', 'mla_w_ukv', 'lru_out_g', 'mla_out_g', 'w_out', 'mix_post_g', 'ffn2_pre_g', 'ffn2_w_gate', 'ffn2_w_up', 'ffn2_w_down', 'ffn2_post_g']
TWIN_DIFF_INPUT = 'x'
TWIN_INPUTS = ['x', 'meta_tokens', 'ffn1_pre_g', 'ffn1_w_gate', 'ffn1_w_up', 'ffn1_w_down', 'ffn1_post_g', 'mix_pre_g', 'w_in', 'lru_conv_w', 'lru_conv_b', 'lru_w_a', 'lru_b_a', 'lru_w_x', 'lru_b_x', 'lru_lambda', 'mla_q_norm_g', 'mla_w_uq', 'mla_kv_norm_g', 'mla_w_ukv', 'lru_out_g', 'mla_out_g', 'w_out', 'mix_post_g', 'ffn2_pre_g', 'ffn2_w_gate', 'ffn2_w_up', 'ffn2_w_down', 'ffn2_post_g', 'loss_target', 'm_meta_tokens', 'm_ffn1_pre_g', 'm_ffn1_w_gate', 'm_ffn1_w_up', 'm_ffn1_w_down', 'm_ffn1_post_g', 'm_mix_pre_g', 'm_w_in', 'm_lru_conv_w', 'm_lru_conv_b', 'm_lru_w_a', 'm_lru_b_a', 'm_lru_w_x', 'm_lru_b_x', 'm_lru_lambda', 'm_mla_q_norm_g', 'm_mla_w_uq', 'm_mla_kv_norm_g', 'm_mla_w_ukv', 'm_lru_out_g', 'm_mla_out_g', 'm_w_out', 'm_mix_post_g', 'm_ffn2_pre_g', 'm_ffn2_w_gate', 'm_ffn2_w_up', 'm_ffn2_w_down', 'm_ffn2_post_g', 'v_meta_tokens', 'v_ffn1_pre_g', 'v_ffn1_w_gate', 'v_ffn1_w_up', 'v_ffn1_w_down', 'v_ffn1_post_g', 'v_mix_pre_g', 'v_w_in', 'v_lru_conv_w', 'v_lru_conv_b', 'v_lru_w_a', 'v_lru_b_a', 'v_lru_w_x', 'v_lru_b_x', 'v_lru_lambda', 'v_mla_q_norm_g', 'v_mla_w_uq', 'v_mla_kv_norm_g', 'v_mla_w_ukv', 'v_lru_out_g', 'v_mla_out_g', 'v_w_out', 'v_mix_post_g', 'v_ffn2_pre_g', 'v_ffn2_w_gate', 'v_ffn2_w_up', 'v_ffn2_w_down', 'v_ffn2_post_g']
TWIN_OUTPUTS = ['loss', 'grad_x', 'grad_meta_tokens', 'grad_ffn1_pre_g', 'grad_ffn1_w_gate', 'grad_ffn1_w_up', 'grad_ffn1_w_down', 'grad_ffn1_post_g', 'grad_mix_pre_g', 'grad_w_in', 'grad_lru_conv_w', 'grad_lru_conv_b', 'grad_lru_w_a', 'grad_lru_b_a', 'grad_lru_w_x', 'grad_lru_b_x', 'grad_lru_lambda', 'grad_mla_q_norm_g', 'grad_mla_w_uq', 'grad_mla_kv_norm_g', 'grad_mla_w_ukv', 'grad_lru_out_g', 'grad_mla_out_g', 'grad_w_out', 'grad_mix_post_g', 'grad_ffn2_pre_g', 'grad_ffn2_w_gate', 'grad_ffn2_w_up', 'grad_ffn2_w_down', 'grad_ffn2_post_g', 'delta_meta_tokens', 'delta_ffn1_pre_g', 'delta_ffn1_w_gate', 'delta_ffn1_w_up', 'delta_ffn1_w_down', 'delta_ffn1_post_g', 'delta_mix_pre_g', 'delta_w_in', 'delta_lru_conv_w', 'delta_lru_conv_b', 'delta_lru_w_a', 'delta_lru_b_a', 'delta_lru_w_x', 'delta_lru_b_x', 'delta_lru_lambda', 'delta_mla_q_norm_g', 'delta_mla_w_uq', 'delta_mla_kv_norm_g', 'delta_mla_w_ukv', 'delta_lru_out_g', 'delta_mla_out_g', 'delta_w_out', 'delta_mix_post_g', 'delta_ffn2_pre_g', 'delta_ffn2_w_gate', 'delta_ffn2_w_up', 'delta_ffn2_w_down', 'delta_ffn2_post_g', 'new_m_meta_tokens', 'new_m_ffn1_pre_g', 'new_m_ffn1_w_gate', 'new_m_ffn1_w_up', 'new_m_ffn1_w_down', 'new_m_ffn1_post_g', 'new_m_mix_pre_g', 'new_m_w_in', 'new_m_lru_conv_w', 'new_m_lru_conv_b', 'new_m_lru_w_a', 'new_m_lru_b_a', 'new_m_lru_w_x', 'new_m_lru_b_x', 'new_m_lru_lambda', 'new_m_mla_q_norm_g', 'new_m_mla_w_uq', 'new_m_mla_kv_norm_g', 'new_m_mla_w_ukv', 'new_m_lru_out_g', 'new_m_mla_out_g', 'new_m_w_out', 'new_m_mix_post_g', 'new_m_ffn2_pre_g', 'new_m_ffn2_w_gate', 'new_m_ffn2_w_up', 'new_m_ffn2_w_down', 'new_m_ffn2_post_g', 'new_v_meta_tokens', 'new_v_ffn1_pre_g', 'new_v_ffn1_w_gate', 'new_v_ffn1_w_up', 'new_v_ffn1_w_down', 'new_v_ffn1_post_g', 'new_v_mix_pre_g', 'new_v_w_in', 'new_v_lru_conv_w', 'new_v_lru_conv_b', 'new_v_lru_w_a', 'new_v_lru_b_a', 'new_v_lru_w_x', 'new_v_lru_b_x', 'new_v_lru_lambda', 'new_v_mla_q_norm_g', 'new_v_mla_w_uq', 'new_v_mla_kv_norm_g', 'new_v_mla_w_ukv', 'new_v_lru_out_g', 'new_v_mla_out_g', 'new_v_w_out', 'new_v_mix_post_g', 'new_v_ffn2_pre_g', 'new_v_ffn2_w_gate', 'new_v_ffn2_w_up', 'new_v_ffn2_w_down', 'new_v_ffn2_post_g']
TWIN_LEAF_KINDS = {'loss': 'loss', 'grad_x': 'grad_x', 'grad_meta_tokens': 'grad_w', 'grad_ffn1_pre_g': 'grad_w', 'grad_ffn1_w_gate': 'grad_w', 'grad_ffn1_w_up': 'grad_w', 'grad_ffn1_w_down': 'grad_w', 'grad_ffn1_post_g': 'grad_w', 'grad_mix_pre_g': 'grad_w', 'grad_w_in': 'grad_w', 'grad_lru_conv_w': 'grad_w', 'grad_lru_conv_b': 'grad_w', 'grad_lru_w_a': 'grad_w', 'grad_lru_b_a': 'grad_w', 'grad_lru_w_x': 'grad_w', 'grad_lru_b_x': 'grad_w', 'grad_lru_lambda': 'grad_w', 'grad_mla_q_norm_g': 'grad_w', 'grad_mla_w_uq': 'grad_w', 'grad_mla_kv_norm_g': 'grad_w', 'grad_mla_w_ukv': 'grad_w', 'grad_lru_out_g': 'grad_w', 'grad_mla_out_g': 'grad_w', 'grad_w_out': 'grad_w', 'grad_mix_post_g': 'grad_w', 'grad_ffn2_pre_g': 'grad_w', 'grad_ffn2_w_gate': 'grad_w', 'grad_ffn2_w_up': 'grad_w', 'grad_ffn2_w_down': 'grad_w', 'grad_ffn2_post_g': 'grad_w', 'delta_meta_tokens': 'delta_w', 'delta_ffn1_pre_g': 'delta_w', 'delta_ffn1_w_gate': 'delta_w', 'delta_ffn1_w_up': 'delta_w', 'delta_ffn1_w_down': 'delta_w', 'delta_ffn1_post_g': 'delta_w', 'delta_mix_pre_g': 'delta_w', 'delta_w_in': 'delta_w', 'delta_lru_conv_w': 'delta_w', 'delta_lru_conv_b': 'delta_w', 'delta_lru_w_a': 'delta_w', 'delta_lru_b_a': 'delta_w', 'delta_lru_w_x': 'delta_w', 'delta_lru_b_x': 'delta_w', 'delta_lru_lambda': 'delta_w', 'delta_mla_q_norm_g': 'delta_w', 'delta_mla_w_uq': 'delta_w', 'delta_mla_kv_norm_g': 'delta_w', 'delta_mla_w_ukv': 'delta_w', 'delta_lru_out_g': 'delta_w', 'delta_mla_out_g': 'delta_w', 'delta_w_out': 'delta_w', 'delta_mix_post_g': 'delta_w', 'delta_ffn2_pre_g': 'delta_w', 'delta_ffn2_w_gate': 'delta_w', 'delta_ffn2_w_up': 'delta_w', 'delta_ffn2_w_down': 'delta_w', 'delta_ffn2_post_g': 'delta_w', 'new_m_meta_tokens': 'new_m', 'new_m_ffn1_pre_g': 'new_m', 'new_m_ffn1_w_gate': 'new_m', 'new_m_ffn1_w_up': 'new_m', 'new_m_ffn1_w_down': 'new_m', 'new_m_ffn1_post_g': 'new_m', 'new_m_mix_pre_g': 'new_m', 'new_m_w_in': 'new_m', 'new_m_lru_conv_w': 'new_m', 'new_m_lru_conv_b': 'new_m', 'new_m_lru_w_a': 'new_m', 'new_m_lru_b_a': 'new_m', 'new_m_lru_w_x': 'new_m', 'new_m_lru_b_x': 'new_m', 'new_m_lru_lambda': 'new_m', 'new_m_mla_q_norm_g': 'new_m', 'new_m_mla_w_uq': 'new_m', 'new_m_mla_kv_norm_g': 'new_m', 'new_m_mla_w_ukv': 'new_m', 'new_m_lru_out_g': 'new_m', 'new_m_mla_out_g': 'new_m', 'new_m_w_out': 'new_m', 'new_m_mix_post_g': 'new_m', 'new_m_ffn2_pre_g': 'new_m', 'new_m_ffn2_w_gate': 'new_m', 'new_m_ffn2_w_up': 'new_m', 'new_m_ffn2_w_down': 'new_m', 'new_m_ffn2_post_g': 'new_m', 'new_v_meta_tokens': 'new_v', 'new_v_ffn1_pre_g': 'new_v', 'new_v_ffn1_w_gate': 'new_v', 'new_v_ffn1_w_up': 'new_v', 'new_v_ffn1_w_down': 'new_v', 'new_v_ffn1_post_g': 'new_v', 'new_v_mix_pre_g': 'new_v', 'new_v_w_in': 'new_v', 'new_v_lru_conv_w': 'new_v', 'new_v_lru_conv_b': 'new_v', 'new_v_lru_w_a': 'new_v', 'new_v_lru_b_a': 'new_v', 'new_v_lru_w_x': 'new_v', 'new_v_lru_b_x': 'new_v', 'new_v_lru_lambda': 'new_v', 'new_v_mla_q_norm_g': 'new_v', 'new_v_mla_w_uq': 'new_v', 'new_v_mla_kv_norm_g': 'new_v', 'new_v_mla_w_ukv': 'new_v', 'new_v_lru_out_g': 'new_v', 'new_v_mla_out_g': 'new_v', 'new_v_w_out': 'new_v', 'new_v_mix_post_g': 'new_v', 'new_v_ffn2_pre_g': 'new_v', 'new_v_ffn2_w_gate': 'new_v', 'new_v_ffn2_w_up': 'new_v', 'new_v_ffn2_w_down': 'new_v', 'new_v_ffn2_post_g': 'new_v'}


def _forward(args):
    return _fwd_reference(*[args[k] for k in FWD_PARAMS])


def _output_shape():
    out = _jax.eval_shape(lambda: _forward(_fwd_setup_inputs(0)))
    return out.shape, out.dtype

N_MICROBATCH = 1
ADAM_LR = 0.001
ADAM_B1 = 0.9
ADAM_B2 = 0.999
ADAM_EPS = 1e-08
ADAM_WD = 0.01
ADAM_STEP = 10
PER_EXAMPLE_BATCH_AXIS = {'x': 0, 'loss_target': 0}
SHARED_INPUTS = []
_WEIGHT_DTYPES = {'meta_tokens': _jnp.float32, 'ffn1_pre_g': _jnp.float32, 'ffn1_w_gate': _jnp.float32, 'ffn1_w_up': _jnp.float32, 'ffn1_w_down': _jnp.float32, 'ffn1_post_g': _jnp.float32, 'mix_pre_g': _jnp.float32, 'w_in': _jnp.float32, 'lru_conv_w': _jnp.float32, 'lru_conv_b': _jnp.float32, 'lru_w_a': _jnp.float32, 'lru_b_a': _jnp.float32, 'lru_w_x': _jnp.float32, 'lru_b_x': _jnp.float32, 'lru_lambda': _jnp.float32, 'mla_q_norm_g': _jnp.float32, 'mla_w_uq': _jnp.float32, 'mla_kv_norm_g': _jnp.float32, 'mla_w_ukv': _jnp.float32, 'lru_out_g': _jnp.float32, 'mla_out_g': _jnp.float32, 'w_out': _jnp.float32, 'mix_post_g': _jnp.float32, 'ffn2_pre_g': _jnp.float32, 'ffn2_w_gate': _jnp.float32, 'ffn2_w_up': _jnp.float32, 'ffn2_w_down': _jnp.float32, 'ffn2_post_g': _jnp.float32}
MOMENT_SCALE = {'meta_tokens': 2.129000e+00, 'ffn1_pre_g': 7.806424e+00, 'ffn1_w_gate': 2.820773e+00, 'ffn1_w_up': 3.004755e+00, 'ffn1_w_down': 4.968463e+00, 'ffn1_post_g': 8.578967e+00, 'mix_pre_g': 1.492031e+01, 'w_in': 1.202251e+01, 'lru_conv_w': 7.869688e+00, 'lru_conv_b': 7.672697e+01, 'lru_w_a': 2.529729e+00, 'lru_b_a': 1.893133e+00, 'lru_w_x': 5.127516e+00, 'lru_b_x': 2.680818e+00, 'lru_lambda': 3.630163e+00, 'mla_q_norm_g': 1.447867e+00, 'mla_w_uq': 1.154129e+00, 'mla_kv_norm_g': 3.607285e+01, 'mla_w_ukv': 1.495104e+01, 'lru_out_g': 7.698827e+00, 'mla_out_g': 1.882601e+01, 'w_out': 1.634265e+01, 'mix_post_g': 3.634680e+01, 'ffn2_pre_g': 3.026475e+00, 'ffn2_w_gate': 1.016281e+00, 'ffn2_w_up': 1.471677e+00, 'ffn2_w_down': 2.393722e+00, 'ffn2_post_g': 7.893639e+00}


def _to_microbatches(a, axis):
    t = _jnp.moveaxis(a, axis, 0)
    t = t.reshape((N_MICROBATCH, t.shape[0] // N_MICROBATCH) + t.shape[1:])
    return _jnp.moveaxis(t, 1, axis + 1)


def setup_inputs(seed: int = 0) -> dict:
    inp = _fwd_setup_inputs(seed)
    key = _jax.random.fold_in(_jax.random.key(seed), 7919)
    shape, _ = _output_shape()
    out = dict(inp)
    out["loss_target"] = _jax.random.normal(_jax.random.fold_in(key, 0), shape, _jnp.float32)
    for i, name in enumerate(TWIN_WEIGHTS):
        w = inp[name].astype(_jnp.float32)
        if MOMENT_SCALE is None:
            s = _jnp.sqrt(_jnp.mean(_jnp.square(w)) + 1e-30)
        else:
            s = MOMENT_SCALE[name]
        km, kv = _jax.random.split(_jax.random.fold_in(key, i + 1))
        out[name] = w
        out["m_" + name] = s * _jax.random.normal(km, w.shape, _jnp.float32)
        out["v_" + name] = (s * s) * _jax.random.uniform(kv, w.shape, _jnp.float32, 0.5, 1.5)
    if N_MICROBATCH > 1:
        for name, axis in PER_EXAMPLE_BATCH_AXIS.items():
            out[name] = _to_microbatches(out[name], axis)
    return {'x': out['x'], 'meta_tokens': out['meta_tokens'], 'ffn1_pre_g': out['ffn1_pre_g'], 'ffn1_w_gate': out['ffn1_w_gate'], 'ffn1_w_up': out['ffn1_w_up'], 'ffn1_w_down': out['ffn1_w_down'], 'ffn1_post_g': out['ffn1_post_g'], 'mix_pre_g': out['mix_pre_g'], 'w_in': out['w_in'], 'lru_conv_w': out['lru_conv_w'], 'lru_conv_b': out['lru_conv_b'], 'lru_w_a': out['lru_w_a'], 'lru_b_a': out['lru_b_a'], 'lru_w_x': out['lru_w_x'], 'lru_b_x': out['lru_b_x'], 'lru_lambda': out['lru_lambda'], 'mla_q_norm_g': out['mla_q_norm_g'], 'mla_w_uq': out['mla_w_uq'], 'mla_kv_norm_g': out['mla_kv_norm_g'], 'mla_w_ukv': out['mla_w_ukv'], 'lru_out_g': out['lru_out_g'], 'mla_out_g': out['mla_out_g'], 'w_out': out['w_out'], 'mix_post_g': out['mix_post_g'], 'ffn2_pre_g': out['ffn2_pre_g'], 'ffn2_w_gate': out['ffn2_w_gate'], 'ffn2_w_up': out['ffn2_w_up'], 'ffn2_w_down': out['ffn2_w_down'], 'ffn2_post_g': out['ffn2_post_g'], 'loss_target': out['loss_target'], 'm_meta_tokens': out['m_meta_tokens'], 'm_ffn1_pre_g': out['m_ffn1_pre_g'], 'm_ffn1_w_gate': out['m_ffn1_w_gate'], 'm_ffn1_w_up': out['m_ffn1_w_up'], 'm_ffn1_w_down': out['m_ffn1_w_down'], 'm_ffn1_post_g': out['m_ffn1_post_g'], 'm_mix_pre_g': out['m_mix_pre_g'], 'm_w_in': out['m_w_in'], 'm_lru_conv_w': out['m_lru_conv_w'], 'm_lru_conv_b': out['m_lru_conv_b'], 'm_lru_w_a': out['m_lru_w_a'], 'm_lru_b_a': out['m_lru_b_a'], 'm_lru_w_x': out['m_lru_w_x'], 'm_lru_b_x': out['m_lru_b_x'], 'm_lru_lambda': out['m_lru_lambda'], 'm_mla_q_norm_g': out['m_mla_q_norm_g'], 'm_mla_w_uq': out['m_mla_w_uq'], 'm_mla_kv_norm_g': out['m_mla_kv_norm_g'], 'm_mla_w_ukv': out['m_mla_w_ukv'], 'm_lru_out_g': out['m_lru_out_g'], 'm_mla_out_g': out['m_mla_out_g'], 'm_w_out': out['m_w_out'], 'm_mix_post_g': out['m_mix_post_g'], 'm_ffn2_pre_g': out['m_ffn2_pre_g'], 'm_ffn2_w_gate': out['m_ffn2_w_gate'], 'm_ffn2_w_up': out['m_ffn2_w_up'], 'm_ffn2_w_down': out['m_ffn2_w_down'], 'm_ffn2_post_g': out['m_ffn2_post_g'], 'v_meta_tokens': out['v_meta_tokens'], 'v_ffn1_pre_g': out['v_ffn1_pre_g'], 'v_ffn1_w_gate': out['v_ffn1_w_gate'], 'v_ffn1_w_up': out['v_ffn1_w_up'], 'v_ffn1_w_down': out['v_ffn1_w_down'], 'v_ffn1_post_g': out['v_ffn1_post_g'], 'v_mix_pre_g': out['v_mix_pre_g'], 'v_w_in': out['v_w_in'], 'v_lru_conv_w': out['v_lru_conv_w'], 'v_lru_conv_b': out['v_lru_conv_b'], 'v_lru_w_a': out['v_lru_w_a'], 'v_lru_b_a': out['v_lru_b_a'], 'v_lru_w_x': out['v_lru_w_x'], 'v_lru_b_x': out['v_lru_b_x'], 'v_lru_lambda': out['v_lru_lambda'], 'v_mla_q_norm_g': out['v_mla_q_norm_g'], 'v_mla_w_uq': out['v_mla_w_uq'], 'v_mla_kv_norm_g': out['v_mla_kv_norm_g'], 'v_mla_w_ukv': out['v_mla_w_ukv'], 'v_lru_out_g': out['v_lru_out_g'], 'v_mla_out_g': out['v_mla_out_g'], 'v_w_out': out['v_w_out'], 'v_mix_post_g': out['v_mix_post_g'], 'v_ffn2_pre_g': out['v_ffn2_pre_g'], 'v_ffn2_w_gate': out['v_ffn2_w_gate'], 'v_ffn2_w_up': out['v_ffn2_w_up'], 'v_ffn2_w_down': out['v_ffn2_w_down'], 'v_ffn2_post_g': out['v_ffn2_post_g']}


def _loss(weights, diff, rest, loss_target):
    with _jax.named_scope("forward"):
        args = {**rest, TWIN_DIFF_INPUT: diff, **{k: w.astype(_WEIGHT_DTYPES[k]) for k, w in weights.items()}}
        y = _forward(args)
    with _jax.named_scope("loss_head"):
        err = _jnp.square(y.astype(_jnp.float32) - loss_target)
        return 0.5 * _jnp.sum(_jnp.mean(err, axis=-1)) if err.ndim else 0.5 * err


def _adamw(w, g, m, v):
    m = ADAM_B1 * m + (1.0 - ADAM_B1) * g
    v = ADAM_B2 * v + (1.0 - ADAM_B2) * _jnp.square(g)
    m_hat = m / (1.0 - ADAM_B1 ** ADAM_STEP)
    v_hat = v / (1.0 - ADAM_B2 ** ADAM_STEP)
    delta = -ADAM_LR * (m_hat / (_jnp.sqrt(v_hat) + ADAM_EPS) + ADAM_WD * w)
    return delta, m, v


def reference(x, meta_tokens, ffn1_pre_g, ffn1_w_gate, ffn1_w_up, ffn1_w_down, ffn1_post_g, mix_pre_g, w_in, lru_conv_w, lru_conv_b, lru_w_a, lru_b_a, lru_w_x, lru_b_x, lru_lambda, mla_q_norm_g, mla_w_uq, mla_kv_norm_g, mla_w_ukv, lru_out_g, mla_out_g, w_out, mix_post_g, ffn2_pre_g, ffn2_w_gate, ffn2_w_up, ffn2_w_down, ffn2_post_g, loss_target, m_meta_tokens, m_ffn1_pre_g, m_ffn1_w_gate, m_ffn1_w_up, m_ffn1_w_down, m_ffn1_post_g, m_mix_pre_g, m_w_in, m_lru_conv_w, m_lru_conv_b, m_lru_w_a, m_lru_b_a, m_lru_w_x, m_lru_b_x, m_lru_lambda, m_mla_q_norm_g, m_mla_w_uq, m_mla_kv_norm_g, m_mla_w_ukv, m_lru_out_g, m_mla_out_g, m_w_out, m_mix_post_g, m_ffn2_pre_g, m_ffn2_w_gate, m_ffn2_w_up, m_ffn2_w_down, m_ffn2_post_g, v_meta_tokens, v_ffn1_pre_g, v_ffn1_w_gate, v_ffn1_w_up, v_ffn1_w_down, v_ffn1_post_g, v_mix_pre_g, v_w_in, v_lru_conv_w, v_lru_conv_b, v_lru_w_a, v_lru_b_a, v_lru_w_x, v_lru_b_x, v_lru_lambda, v_mla_q_norm_g, v_mla_w_uq, v_mla_kv_norm_g, v_mla_w_ukv, v_lru_out_g, v_mla_out_g, v_w_out, v_mix_post_g, v_ffn2_pre_g, v_ffn2_w_gate, v_ffn2_w_up, v_ffn2_w_down, v_ffn2_post_g):
    given = dict(x=x, meta_tokens=meta_tokens, ffn1_pre_g=ffn1_pre_g, ffn1_w_gate=ffn1_w_gate, ffn1_w_up=ffn1_w_up, ffn1_w_down=ffn1_w_down, ffn1_post_g=ffn1_post_g, mix_pre_g=mix_pre_g, w_in=w_in, lru_conv_w=lru_conv_w, lru_conv_b=lru_conv_b, lru_w_a=lru_w_a, lru_b_a=lru_b_a, lru_w_x=lru_w_x, lru_b_x=lru_b_x, lru_lambda=lru_lambda, mla_q_norm_g=mla_q_norm_g, mla_w_uq=mla_w_uq, mla_kv_norm_g=mla_kv_norm_g, mla_w_ukv=mla_w_ukv, lru_out_g=lru_out_g, mla_out_g=mla_out_g, w_out=w_out, mix_post_g=mix_post_g, ffn2_pre_g=ffn2_pre_g, ffn2_w_gate=ffn2_w_gate, ffn2_w_up=ffn2_w_up, ffn2_w_down=ffn2_w_down, ffn2_post_g=ffn2_post_g, loss_target=loss_target, m_meta_tokens=m_meta_tokens, m_ffn1_pre_g=m_ffn1_pre_g, m_ffn1_w_gate=m_ffn1_w_gate, m_ffn1_w_up=m_ffn1_w_up, m_ffn1_w_down=m_ffn1_w_down, m_ffn1_post_g=m_ffn1_post_g, m_mix_pre_g=m_mix_pre_g, m_w_in=m_w_in, m_lru_conv_w=m_lru_conv_w, m_lru_conv_b=m_lru_conv_b, m_lru_w_a=m_lru_w_a, m_lru_b_a=m_lru_b_a, m_lru_w_x=m_lru_w_x, m_lru_b_x=m_lru_b_x, m_lru_lambda=m_lru_lambda, m_mla_q_norm_g=m_mla_q_norm_g, m_mla_w_uq=m_mla_w_uq, m_mla_kv_norm_g=m_mla_kv_norm_g, m_mla_w_ukv=m_mla_w_ukv, m_lru_out_g=m_lru_out_g, m_mla_out_g=m_mla_out_g, m_w_out=m_w_out, m_mix_post_g=m_mix_post_g, m_ffn2_pre_g=m_ffn2_pre_g, m_ffn2_w_gate=m_ffn2_w_gate, m_ffn2_w_up=m_ffn2_w_up, m_ffn2_w_down=m_ffn2_w_down, m_ffn2_post_g=m_ffn2_post_g, v_meta_tokens=v_meta_tokens, v_ffn1_pre_g=v_ffn1_pre_g, v_ffn1_w_gate=v_ffn1_w_gate, v_ffn1_w_up=v_ffn1_w_up, v_ffn1_w_down=v_ffn1_w_down, v_ffn1_post_g=v_ffn1_post_g, v_mix_pre_g=v_mix_pre_g, v_w_in=v_w_in, v_lru_conv_w=v_lru_conv_w, v_lru_conv_b=v_lru_conv_b, v_lru_w_a=v_lru_w_a, v_lru_b_a=v_lru_b_a, v_lru_w_x=v_lru_w_x, v_lru_b_x=v_lru_b_x, v_lru_lambda=v_lru_lambda, v_mla_q_norm_g=v_mla_q_norm_g, v_mla_w_uq=v_mla_w_uq, v_mla_kv_norm_g=v_mla_kv_norm_g, v_mla_w_ukv=v_mla_w_ukv, v_lru_out_g=v_lru_out_g, v_mla_out_g=v_mla_out_g, v_w_out=v_w_out, v_mix_post_g=v_mix_post_g, v_ffn2_pre_g=v_ffn2_pre_g, v_ffn2_w_gate=v_ffn2_w_gate, v_ffn2_w_up=v_ffn2_w_up, v_ffn2_w_down=v_ffn2_w_down, v_ffn2_post_g=v_ffn2_post_g)
    weights = {n: given[n] for n in TWIN_WEIGHTS}
    shared = {n: given[n] for n in SHARED_INPUTS}
    per_example = {n: given[n] for n in ['x']}
    grad_fn = _jax.value_and_grad(_loss, argnums=(0, 1))

    def one_microbatch(ex, loss_target):
        ex = dict(ex)
        diff = ex.pop(TWIN_DIFF_INPUT)
        return grad_fn(weights, diff, {**shared, **ex}, loss_target)

    if N_MICROBATCH == 1:
        loss, (grad_w, grad_x) = one_microbatch(per_example, given["loss_target"])
    else:
        def body(carry, xs):
            loss_sum, grad_sum = carry
            l_k, (gw_k, gx_k) = one_microbatch(xs[0], xs[1])
            with _jax.named_scope("update"):
                return (loss_sum + l_k, _jax.tree.map(_jnp.add, grad_sum, gw_k)), gx_k

        init = (_jnp.zeros((), _jnp.float32), _jax.tree.map(_jnp.zeros_like, weights))
        (loss, grad_w), grad_x = _jax.lax.scan(body, init, (per_example, given["loss_target"]))
    with _jax.named_scope("update"):
        delta_w, new_m, new_v = {}, {}, {}
        for n in TWIN_WEIGHTS:
            delta_w[n], new_m[n], new_v[n] = _adamw(weights[n], grad_w[n], given["m_" + n], given["v_" + n])
    return (loss, grad_x, *[grad_w[n] for n in TWIN_WEIGHTS], *[delta_w[n] for n in TWIN_WEIGHTS],
            *[new_m[n] for n in TWIN_WEIGHTS], *[new_v[n] for n in TWIN_WEIGHTS])
```

```python
import functools
import math

import jax
import jax.numpy as jnp
from jax import lax
from jax.experimental import pallas as pl
from jax.experimental.pallas import tpu as pltpu

F32 = jnp.float32
BF16 = jnp.bfloat16

EPS = 1e-6
N_DEV = 8
LANES = 128
ROW_TILE = 384
SCAN_CHUNKS = 8
VMEM_LIMIT = 56 * 1024 * 1024
MM_ACC_BYTES = 8 * 1024 * 1024

LRU_C = 8.0
CONV_WIDTH = 4
MLA_HEADS = 8
QK_NOPE = 64
QK_ROPE = 32
ROPE_THETA = 10000.0

ADAM_LR = 0.001
ADAM_B1 = 0.9
ADAM_B2 = 0.999
ADAM_EPS = 1e-08
ADAM_WD = 0.01
ADAM_STEP = 10

MESH_AXES = ("x", "y", "c")
MESH = pl.DeviceIdType.MESH

WEIGHTS = ['meta_tokens', 'ffn1_pre_g', 'ffn1_w_gate', 'ffn1_w_up', 'ffn1_w_down', 'ffn1_post_g', 'mix_pre_g', 'w_in',
           'lru_conv_w', 'lru_conv_b', 'lru_w_a', 'lru_b_a', 'lru_w_x', 'lru_b_x', 'lru_lambda', 'mla_q_norm_g',
           'mla_w_uq', 'mla_kv_norm_g', 'mla_w_ukv', 'lru_out_g', 'mla_out_g', 'w_out', 'mix_post_g', 'ffn2_pre_g',
           'ffn2_w_gate', 'ffn2_w_up', 'ffn2_w_down', 'ffn2_post_g']
BIG = ['ffn1_w_gate', 'ffn1_w_up', 'ffn1_w_down', 'w_in', 'mla_w_uq', 'mla_w_ukv', 'w_out',
       'ffn2_w_gate', 'ffn2_w_up', 'ffn2_w_down']
SMALL_SHARDED = ['lru_conv_w', 'meta_tokens']
REPLICATED = [n for n in WEIGHTS if n not in BIG and n not in SMALL_SHARDED]


def _cparams(**kw):
    return pltpu.CompilerParams(vmem_limit_bytes=VMEM_LIMIT, **kw)


def _resident(shape):
    nd = len(shape)
    return pl.BlockSpec(shape, lambda *_: (0,) * nd, pipeline_mode=pl.Buffered(1))


def _rows(cols, tm=ROW_TILE):
    return pl.BlockSpec((tm, cols), lambda i: (i, 0))


def _const(shape):
    nd = len(shape)
    return pl.BlockSpec(shape, lambda *_: (0,) * nd)


def _dot(a, b):
    return jnp.dot(a, b, preferred_element_type=F32)


def _dot_nt(a, b):
    return lax.dot_general(a, b, (((1,), (1,)), ((), ())), preferred_element_type=F32)


def _dot_tn(a, b):
    return lax.dot_general(a, b, (((0,), (0,)), ((), ())), preferred_element_type=F32)


def _rms_fwd(x, g):
    rinv = lax.rsqrt(jnp.mean(x * x, axis=-1, keepdims=True) + EPS)
    xn = x * rinv
    return xn * g, xn, rinv


def _rms_bwd(xn, rinv, g, dy):
    dxn = dy * g
    dx = rinv * (dxn - xn * jnp.mean(dxn * xn, axis=-1, keepdims=True))
    return dx, jnp.sum(dy * xn, axis=0, keepdims=True)


def _accumulate(ref, val, first):
    @pl.when(first)
    def _():
        ref[...] = val

    @pl.when(jnp.logical_not(first))
    def _():
        ref[...] += val


def _all_gather(shards):
    n = len(shards)

    def body(*refs):
        x_refs, out_refs = refs[:n], refs[n:2 * n]
        send_sems, recv_sems, local_sems = refs[2 * n:]
        x, y, c = lax.axis_index("x"), lax.axis_index("y"), lax.axis_index("c")
        me, sibling = (x, y, c), (x, y, 1 - c)
        chips = [(1 - x, y), (x, 1 - y), (1 - x, 1 - y)]

        def copy(a, k, block, to, src=None):
            dst = out_refs[a].at[4 * block[0] + 2 * block[1] + block[2]]
            return pltpu.make_async_remote_copy(
                src_ref=dst if src is None else src, dst_ref=dst,
                send_sem=send_sems.at[7 * a + k], recv_sem=recv_sems.at[7 * a + k], device_id=to, device_id_type=MESH)

        mine = [pltpu.make_async_copy(x_refs[a], out_refs[a].at[4 * x + 2 * y + c], local_sems.at[a]) for a in range(n)]
        for cp in mine:
            cp.start()
        first = []
        for a in range(n):
            first.append(copy(a, 0, me, sibling, src=x_refs[a]))
            first += [copy(a, 1 + j, me, (*chip, c), src=x_refs[a]) for j, chip in enumerate(chips)]
        for cp in first:
            cp.start()
        passed = []
        for j, chip in enumerate(chips):
            for a in range(n):
                copy(a, 1 + j, (*chip, c), me).wait_recv()
                cp = copy(a, 4 + j, (*chip, c), sibling)
                cp.start()
                passed.append(cp)
        for a in range(n):
            copy(a, 0, sibling, me).wait_recv()
        for j, chip in enumerate(chips):
            for a in range(n):
                copy(a, 4 + j, (*chip, 1 - c), me).wait_recv()
        for cp in first + passed:
            cp.wait_send()
        for cp in mine:
            cp.wait()

    hbm = pl.BlockSpec(memory_space=pl.ANY)
    return pl.pallas_call(
        body, name="all_gather",
        out_shape=tuple(jax.ShapeDtypeStruct((N_DEV,) + s.shape, s.dtype) for s in shards),
        in_specs=[hbm] * n, out_specs=[hbm] * n,
        scratch_shapes=[pltpu.SemaphoreType.DMA((7 * n,)), pltpu.SemaphoreType.DMA((7 * n,)),
                        pltpu.SemaphoreType.DMA((n,))],
    )(*shards)


def _swap_with_sibling(parts):
    n = len(parts)

    def body(*refs):
        p_refs, out_refs = refs[:n], refs[n:2 * n]
        send_sems, recv_sems = refs[2 * n:]
        x, y, c = lax.axis_index("x"), lax.axis_index("y"), lax.axis_index("c")
        copies = []
        for a in range(n):
            for k in range(4):
                cp = pltpu.make_async_remote_copy(
                    src_ref=p_refs[a].at[2 * k + (1 - c)], dst_ref=out_refs[a].at[k],
                    send_sem=send_sems.at[4 * a + k], recv_sem=recv_sems.at[4 * a + k],
                    device_id=(x, y, 1 - c), device_id_type=MESH)
                cp.start()
                copies.append(cp)
        for cp in copies:
            cp.wait()

    hbm = pl.BlockSpec(memory_space=pl.ANY)
    return pl.pallas_call(
        body, name="rs_sibling",
        out_shape=tuple(jax.ShapeDtypeStruct((4,) + p.shape[1:], p.dtype) for p in parts),
        in_specs=[hbm] * n, out_specs=[hbm] * n,
        scratch_shapes=[pltpu.SemaphoreType.DMA((4 * n,)), pltpu.SemaphoreType.DMA((4 * n,))],
    )(*parts)


def _exchange_chips(parts):
    n = len(parts)

    def body(*refs):
        p_refs, out_refs = refs[:n], refs[n:2 * n]
        send_sems, recv_sems = refs[2 * n:]
        x, y, c = lax.axis_index("x"), lax.axis_index("y"), lax.axis_index("c")
        copies = []
        for a in range(n):
            for k, (tx, ty) in enumerate([(1 - x, y), (x, 1 - y), (1 - x, 1 - y)]):
                cp = pltpu.make_async_remote_copy(
                    src_ref=p_refs[a].at[2 * tx + ty], dst_ref=out_refs[a].at[k],
                    send_sem=send_sems.at[3 * a + k], recv_sem=recv_sems.at[3 * a + k],
                    device_id=(tx, ty, c), device_id_type=MESH)
                cp.start()
                copies.append(cp)
        for cp in copies:
            cp.wait()

    hbm = pl.BlockSpec(memory_space=pl.ANY)
    return pl.pallas_call(
        body, name="rs_chips",
        out_shape=tuple(jax.ShapeDtypeStruct((3,) + p.shape[1:], p.dtype) for p in parts),
        in_specs=[hbm] * n, out_specs=[hbm] * n,
        scratch_shapes=[pltpu.SemaphoreType.DMA((3 * n,)), pltpu.SemaphoreType.DMA((3 * n,))],
    )(*parts)


def _row_block(rows, cap=2048):
    best = 8
    for t in range(8, min(rows, cap) + 1, 8):
        if rows % t == 0:
            best = t
    return best


def _elementwise_rows(rows, cols):
    return _row_block(rows, cap=max(8, (1 << 17) // cols // 8 * 8))


def _chip_partial(parts, got, core):
    _, rows, cols = got.shape
    tr = _elementwise_rows(rows, cols)
    p4 = parts.reshape(4, 2, rows, cols)

    def body(c_ref, a_ref, b_ref, o_ref):
        o_ref[...] = (a_ref[...].astype(F32) + b_ref[...].astype(F32)).astype(o_ref.dtype)

    return pl.pallas_call(
        body, name="rs_chip_partial",
        out_shape=jax.ShapeDtypeStruct((4, rows, cols), parts.dtype),
        grid_spec=pltpu.PrefetchScalarGridSpec(
            num_scalar_prefetch=1, grid=(4, rows // tr),
            in_specs=[pl.BlockSpec((None, None, tr, cols), lambda k, r, c: (k, c[0], r, 0)),
                      pl.BlockSpec((None, tr, cols), lambda k, r, c: (k, r, 0))],
            out_specs=pl.BlockSpec((None, tr, cols), lambda k, r, c: (k, r, 0))),
        compiler_params=_cparams(),
    )(core, p4, got)


def _adamw_math(w, m, v, g):
    mm = ADAM_B1 * m + (1.0 - ADAM_B1) * g
    vv = ADAM_B2 * v + (1.0 - ADAM_B2) * jnp.square(g)
    m_hat = mm / (1.0 - ADAM_B1 ** ADAM_STEP)
    v_hat = vv / (1.0 - ADAM_B2 ** ADAM_STEP)
    return -ADAM_LR * (m_hat / (jnp.sqrt(v_hat) + ADAM_EPS) + ADAM_WD * w), mm, vv


def _final_sum_adamw(partial, got, chip, w, m, v):
    _, rows, cols = got.shape
    tr = _elementwise_rows(rows, cols)

    def body(c_ref, a_ref, b_ref, w_ref, m_ref, v_ref, g_ref, d_ref, nm_ref, nv_ref):
        g = ((a_ref[...].astype(F32) + b_ref[0].astype(F32)) + b_ref[1].astype(F32)) + b_ref[2].astype(F32)
        g_ref[...] = g
        d_ref[...], nm_ref[...], nv_ref[...] = _adamw_math(w_ref[...], m_ref[...], v_ref[...], g)

    flat = pl.BlockSpec((tr, cols), lambda r, c: (r, 0))
    return pl.pallas_call(
        body, name="rs_final_adamw",
        out_shape=(jax.ShapeDtypeStruct((rows, cols), F32),) * 4,
        grid_spec=pltpu.PrefetchScalarGridSpec(
            num_scalar_prefetch=1, grid=(rows // tr,),
            in_specs=[pl.BlockSpec((None, tr, cols), lambda r, c: (c[0], r, 0)),
                      pl.BlockSpec((3, tr, cols), lambda r, c: (0, r, 0)), flat, flat, flat],
            out_specs=[flat] * 4),
        compiler_params=_cparams(),
    )(chip, partial, got, w, m, v)


def _sum_devices(gathered):
    _, rows, lanes = gathered.shape
    tr = _row_block(rows)

    def body(g_ref, o_ref):
        acc = g_ref[0]
        for j in range(1, N_DEV):
            acc = acc + g_ref[j]
        o_ref[...] = acc

    return pl.pallas_call(
        body, name="small_grad_sum",
        out_shape=jax.ShapeDtypeStruct((rows, lanes), F32),
        grid=(rows // tr,),
        in_specs=[pl.BlockSpec((N_DEV, tr, lanes), lambda r: (0, r, 0))],
        out_specs=pl.BlockSpec((tr, lanes), lambda r: (r, 0)),
        compiler_params=_cparams(),
    )(gathered)


def _assemble(g, kind, n_in=0):
    _, L, A, B = g.shape
    if kind == 'cols':
        oshape = (A, N_DEV * B)
    elif kind == 'rows':
        oshape = (N_DEV * A, B)
    elif kind == 'w_in':
        oshape = (A, n_in + LANES)
    else:
        oshape = (A, N_DEV * LANES)

    def body(x_ref, o_ref):
        if kind == 'cols':
            for j in range(N_DEV):
                o_ref[:, j * B:(j + 1) * B] = x_ref[j]
        elif kind == 'rows':
            for j in range(N_DEV):
                o_ref[j * A:(j + 1) * A, :] = x_ref[j]
        elif kind == 'w_in':
            nat = jnp.concatenate([x_ref[j] for j in range(N_DEV)], axis=1)
            o_ref[:, 0:n_in] = nat[:, 0:n_in]
            o_ref[:, n_in:] = jnp.concatenate(
                [jnp.zeros((A, QK_NOPE), g.dtype), nat[:, n_in:],
                 jnp.zeros((A, LANES - QK_NOPE - QK_ROPE), g.dtype)], axis=1)
        else:
            o_ref[...] = jnp.zeros(oshape, g.dtype)
            for j in range(N_DEV):
                o_ref[:, j * LANES:j * LANES + B] = x_ref[j]

    return pl.pallas_call(
        body, name="assemble_" + kind,
        out_shape=jax.ShapeDtypeStruct((L,) + oshape, g.dtype),
        grid=(L,),
        in_specs=[pl.BlockSpec((N_DEV, None, A, B), lambda l: (0, l, 0, 0))],
        out_specs=pl.BlockSpec((None,) + oshape, lambda l: (l, 0, 0)),
        compiler_params=_cparams(),
    )(g)


def _wgrad(a, b, buf, layer, kind, n_in=0):
    K, M = a.shape
    N = b.shape[1]
    shard = buf.shape[2:]
    tk = ROW_TILE
    nk = K // tk
    if kind == 'cols':
        S = shard[1]
        tn = S
        for t in range(S, N + 1, S):
            if N % t == 0 and t % LANES == 0 and M * t * 4 <= MM_ACC_BYTES:
                tn = t
        oblock = pl.BlockSpec((tn // S, None) + shard, lambda n, k, l=layer: (n, l, 0, 0))
    elif kind == 'rows':
        tn = LANES
        for t in range(LANES, N + 1, LANES):
            if N % t == 0 and M * t * 4 <= MM_ACC_BYTES:
                tn = t
        oblock = pl.BlockSpec((N_DEV, None, shard[0], tn), lambda n, k, l=layer: (0, l, 0, n))
    else:
        tn = N
        oblock = pl.BlockSpec((N_DEV, None) + shard, lambda n, k, l=layer: (0, l, 0, 0))

    def body(a_ref, b_ref, buf_ref, o_ref, acc_ref):
        k = pl.program_id(1)

        @pl.when(k == 0)
        def _():
            acc_ref[...] = jnp.zeros_like(acc_ref)

        acc_ref[...] += _dot_tn(a_ref[...], b_ref[...])

        @pl.when(k == nk - 1)
        def _():
            acc = acc_ref[...]
            if kind == 'cols':
                S = shard[1]
                for j in range(tn // S):
                    o_ref[j] = acc[:, j * S:(j + 1) * S].astype(o_ref.dtype)
            elif kind == 'rows':
                S = shard[0]
                for j in range(N_DEV):
                    o_ref[j] = acc[j * S:(j + 1) * S, :].astype(o_ref.dtype)
            elif kind == 'w_in':
                S = shard[1]
                nat = jnp.concatenate([acc[:, 0:n_in], acc[:, n_in + QK_NOPE:n_in + QK_NOPE + QK_ROPE]], axis=1)
                for j in range(N_DEV):
                    o_ref[j] = nat[:, j * S:(j + 1) * S].astype(o_ref.dtype)
            else:
                S = shard[1]
                for j in range(N_DEV):
                    o_ref[j] = acc[:, j * LANES:j * LANES + S].astype(o_ref.dtype)

    return pl.pallas_call(
        body, name="wgrad_" + kind,
        out_shape=jax.ShapeDtypeStruct(buf.shape, buf.dtype),
        grid=(N // tn, nk),
        in_specs=[pl.BlockSpec((tk, M), lambda n, k: (k, 0)), pl.BlockSpec((tk, tn), lambda n, k: (k, n)),
                  pl.BlockSpec(memory_space=pl.ANY)],
        out_specs=oblock,
        scratch_shapes=[pltpu.VMEM((M, tn), F32)],
        input_output_aliases={2: 0},
        compiler_params=_cparams(),
    )(a, b, buf)


def _adamw(w, m, v, g):
    rows, cols = w.shape
    tr = _elementwise_rows(rows, cols) if rows % 8 == 0 else rows

    def body(w_ref, m_ref, v_ref, g_ref, d_ref, nm_ref, nv_ref):
        d_ref[...], nm_ref[...], nv_ref[...] = _adamw_math(w_ref[...], m_ref[...], v_ref[...], g_ref[...])

    spec = pl.BlockSpec((tr, cols), lambda r: (r, 0))
    return pl.pallas_call(
        body, name="adamw",
        out_shape=(jax.ShapeDtypeStruct((rows, cols), F32),) * 3,
        grid=(rows // tr,),
        in_specs=[spec] * 4, out_specs=[spec] * 3,
        compiler_params=_cparams(),
    )(w, m, v, g)


def _hidden_chunk(F):
    return F // 2 if (F // 2) % LANES == 0 else F


def _ffn_fwd(h, pre_g, post_g, wg, wu, wd):
    Tp, D = h.shape
    F = wg.shape[1]
    fc = _hidden_chunk(F)

    def body(h_ref, pg_ref, qg_ref, wg_ref, wu_ref, wd_ref, ho_ref, a_ref, b_ref, f_ref):
        hx = h_ref[...]
        u, _, _ = _rms_fwd(hx, pg_ref[...])
        ub = u.astype(BF16)
        f = jnp.zeros((ROW_TILE, D), F32)
        for c in range(F // fc):
            sl = slice(c * fc, (c + 1) * fc)
            a = _dot(ub, wg_ref[:, sl])
            b = _dot(ub, wu_ref[:, sl])
            a_ref[:, sl] = a.astype(BF16)
            b_ref[:, sl] = b.astype(BF16)
            act = (a * jax.nn.sigmoid(a) * b).astype(BF16)
            f = f + _dot(act, wd_ref[sl, :])
        f_ref[...] = f
        y, _, _ = _rms_fwd(f, qg_ref[...])
        ho_ref[...] = hx + 0.5 * y

    return pl.pallas_call(
        body, name="ffn_fwd",
        out_shape=(jax.ShapeDtypeStruct((Tp, D), F32), jax.ShapeDtypeStruct((Tp, F), BF16),
                   jax.ShapeDtypeStruct((Tp, F), BF16), jax.ShapeDtypeStruct((Tp, D), F32)),
        grid=(Tp // ROW_TILE,),
        in_specs=[_rows(D), _const((1, D)), _const((1, D)), _resident((D, F)), _resident((D, F)), _resident((F, D))],
        out_specs=[_rows(D), _rows(F), _rows(F), _rows(D)],
        compiler_params=_cparams(),
    )(h, pre_g, post_g, wg, wu, wd)


def _ffn_bwd_down(dh, a, b, f, post_g, wd):
    Tp, D = dh.shape
    F = a.shape[1]
    fc = _hidden_chunk(F)

    def body(dh_ref, a_ref, b_ref, f_ref, qg_ref, wd_ref, da_ref, db_ref, act_ref, df_ref, dqg_ref):
        qg = qg_ref[...]
        _, fn, frinv = _rms_fwd(f_ref[...], qg)
        df, dqg = _rms_bwd(fn, frinv, qg, 0.5 * dh_ref[...])
        dfb = df.astype(BF16)
        df_ref[...] = dfb
        for c in range(F // fc):
            sl = slice(c * fc, (c + 1) * fc)
            dact = _dot_nt(dfb, wd_ref[sl, :])
            av = a_ref[:, sl].astype(F32)
            bv = b_ref[:, sl].astype(F32)
            sig = jax.nn.sigmoid(av)
            silu = av * sig
            act_ref[:, sl] = (silu * bv).astype(BF16)
            da_ref[:, sl] = (dact * bv * (sig * (1.0 + av * (1.0 - sig)))).astype(BF16)
            db_ref[:, sl] = (dact * silu).astype(BF16)
        _accumulate(dqg_ref, dqg, pl.program_id(0) == 0)

    return pl.pallas_call(
        body, name="ffn_bwd_down",
        out_shape=(jax.ShapeDtypeStruct((Tp, F), BF16), jax.ShapeDtypeStruct((Tp, F), BF16),
                   jax.ShapeDtypeStruct((Tp, F), BF16), jax.ShapeDtypeStruct((Tp, D), BF16),
                   jax.ShapeDtypeStruct((1, D), F32)),
        grid=(Tp // ROW_TILE,),
        in_specs=[_rows(D), _rows(F), _rows(F), _rows(D), _const((1, D)), _resident((F, D))],
        out_specs=[_rows(F), _rows(F), _rows(F), _rows(D), _const((1, D))],
        compiler_params=_cparams(),
    )(dh, a, b, f, post_g, wd)


def _ffn_bwd_up(dh, h, da, db, pre_g, wg, wu):
    Tp, D = dh.shape
    F = da.shape[1]

    def body(dh_ref, h_ref, da_ref, db_ref, pg_ref, wg_ref, wu_ref, dhi_ref, u_ref, dpg_ref):
        pg = pg_ref[...]
        u, hn, hrinv = _rms_fwd(h_ref[...], pg)
        u_ref[...] = u.astype(BF16)
        du = _dot_nt(da_ref[...], wg_ref[...]) + _dot_nt(db_ref[...], wu_ref[...])
        dx, dpg = _rms_bwd(hn, hrinv, pg, du)
        dhi_ref[...] = dh_ref[...] + dx
        _accumulate(dpg_ref, dpg, pl.program_id(0) == 0)

    return pl.pallas_call(
        body, name="ffn_bwd_up",
        out_shape=(jax.ShapeDtypeStruct((Tp, D), F32), jax.ShapeDtypeStruct((Tp, D), BF16),
                   jax.ShapeDtypeStruct((1, D), F32)),
        grid=(Tp // ROW_TILE,),
        in_specs=[_rows(D), _rows(D), _rows(F), _rows(F), _const((1, D)), _resident((D, F)), _resident((D, F))],
        out_specs=[_rows(D), _rows(D), _const((1, D))],
        compiler_params=_cparams(),
    )(dh, h, da, db, pre_g, wg, wu)


def _mix_in_fwd(h, g, win, widths):
    Tp, D = h.shape
    ncol = win.shape[1]
    offs = [0]
    for wd_ in widths:
        offs.append(offs[-1] + wd_)

    def body(h_ref, g_ref, w_ref, *out_refs):
        y, _, _ = _rms_fwd(h_ref[...], g_ref[...])
        z = _dot(y.astype(BF16), w_ref[...])
        for o_ref, lo, wd_ in zip(out_refs, offs, widths):
            o_ref[...] = z[:, lo:lo + wd_]

    return pl.pallas_call(
        body, name="mix_in_fwd",
        out_shape=tuple(jax.ShapeDtypeStruct((Tp, wd_), F32) for wd_ in widths),
        grid=(Tp // ROW_TILE,),
        in_specs=[_rows(D), _const((1, D)), _resident((D, ncol))],
        out_specs=[_rows(wd_) for wd_ in widths],
        compiler_params=_cparams(),
    )(h, g, win)


def _mix_in_bwd(dres, h, dparts, g, win):
    Tp, D = h.shape
    ncol = win.shape[1]
    widths = [p.shape[1] for p in dparts]
    n = len(dparts)

    def body(*refs):
        dres_ref, h_ref = refs[0], refs[1]
        part_refs = refs[2:2 + n]
        g_ref, w_ref = refs[2 + n], refs[3 + n]
        dh_ref, dz_ref, u_ref, dg_ref = refs[4 + n:]
        dz = jnp.concatenate([r[...] for r in part_refs], axis=1).astype(BF16)
        dz_ref[...] = dz
        gg = g_ref[...]
        u, hn, rinv = _rms_fwd(h_ref[...], gg)
        u_ref[...] = u.astype(BF16)
        dx, dg = _rms_bwd(hn, rinv, gg, _dot_nt(dz, w_ref[...]))
        dh_ref[...] = dres_ref[...] + dx
        _accumulate(dg_ref, dg, pl.program_id(0) == 0)

    return pl.pallas_call(
        body, name="mix_in_bwd",
        out_shape=(jax.ShapeDtypeStruct((Tp, D), F32), jax.ShapeDtypeStruct((Tp, ncol), BF16),
                   jax.ShapeDtypeStruct((Tp, D), BF16), jax.ShapeDtypeStruct((1, D), F32)),
        grid=(Tp // ROW_TILE,),
        in_specs=[_rows(D), _rows(D)] + [_rows(wd_) for wd_ in widths] + [_const((1, D)), _resident((D, ncol))],
        out_specs=[_rows(D), _rows(ncol), _rows(D), _const((1, D))],
        compiler_params=_cparams(),
    )(dres, h, *dparts, g, win)


def _mix_out_fwd(h, y_lru, y_mla, lru_g, mla_g, post_g, wout):
    Tp, D = h.shape
    W = y_lru.shape[1]

    def body(h_ref, yl_ref, ym_ref, lg_ref, mg_ref, pg_ref, w_ref, ho_ref, y_ref):
        yl, _, _ = _rms_fwd(yl_ref[...], lg_ref[...])
        ym, _, _ = _rms_fwd(ym_ref[...], mg_ref[...])
        y = _dot(yl.astype(BF16), w_ref[0:W, :]) + _dot(ym.astype(BF16), w_ref[W:, :])
        y_ref[...] = y
        yn, _, _ = _rms_fwd(y, pg_ref[...])
        ho_ref[...] = h_ref[...] + yn

    return pl.pallas_call(
        body, name="mix_out_fwd",
        out_shape=(jax.ShapeDtypeStruct((Tp, D), F32), jax.ShapeDtypeStruct((Tp, D), F32)),
        grid=(Tp // ROW_TILE,),
        in_specs=[_rows(D), _rows(W), _rows(W), _const((1, W)), _const((1, W)), _const((1, D)), _resident(wout.shape)],
        out_specs=[_rows(D), _rows(D)],
        compiler_params=_cparams(),
    )(h, y_lru, y_mla, lru_g, mla_g, post_g, wout)


def _mix_out_bwd(dh, y, y_lru, y_mla, lru_g, mla_g, post_g, wout):
    Tp, D = dh.shape
    W = y_lru.shape[1]

    def body(dh_ref, y_ref, yl_ref, ym_ref, lg_ref, mg_ref, pg_ref, w_ref,
             dyl_ref, dym_ref, dy_ref, cat_ref, dlg_ref, dmg_ref, dpg_ref):
        first = pl.program_id(0) == 0
        pg, lg, mg = pg_ref[...], lg_ref[...], mg_ref[...]
        _, yn, yrinv = _rms_fwd(y_ref[...], pg)
        dy, dpg = _rms_bwd(yn, yrinv, pg, dh_ref[...])
        dyb = dy.astype(BF16)
        dy_ref[...] = dyb
        dcat = _dot_nt(dyb, w_ref[...])
        yl, yln, ylr = _rms_fwd(yl_ref[...], lg)
        ym, ymn, ymr = _rms_fwd(ym_ref[...], mg)
        cat_ref[:, 0:W] = yl.astype(BF16)
        cat_ref[:, W:] = ym.astype(BF16)
        dyl, dlg = _rms_bwd(yln, ylr, lg, dcat[:, 0:W])
        dym, dmg = _rms_bwd(ymn, ymr, mg, dcat[:, W:])
        dyl_ref[...] = dyl
        dym_ref[...] = dym
        _accumulate(dlg_ref, dlg, first)
        _accumulate(dmg_ref, dmg, first)
        _accumulate(dpg_ref, dpg, first)

    return pl.pallas_call(
        body, name="mix_out_bwd",
        out_shape=(jax.ShapeDtypeStruct((Tp, W), F32), jax.ShapeDtypeStruct((Tp, W), F32),
                   jax.ShapeDtypeStruct((Tp, D), BF16), jax.ShapeDtypeStruct((Tp, 2 * W), BF16),
                   jax.ShapeDtypeStruct((1, W), F32), jax.ShapeDtypeStruct((1, W), F32),
                   jax.ShapeDtypeStruct((1, D), F32)),
        grid=(Tp // ROW_TILE,),
        in_specs=[_rows(D), _rows(D), _rows(W), _rows(W), _const((1, W)), _const((1, W)), _const((1, D)),
                  _resident(wout.shape)],
        out_specs=[_rows(W), _rows(W), _rows(D), _rows(2 * W), _const((1, W)), _const((1, W)), _const((1, D))],
        compiler_params=_cparams(),
    )(dh, y, y_lru, y_mla, lru_g, mla_g, post_g, wout)


def _softplus_neg(lam):
    return jnp.maximum(-lam, 0.0) + jnp.log1p(jnp.exp(-jnp.abs(lam)))


def _neg_expm1(y):
    series = -y * (1.0 + 0.5 * y * (1.0 + (1.0 / 3.0) * y * (1.0 + 0.25 * y)))
    return jnp.where(y > -0.01, series, 1.0 - jnp.exp(y))


_GELU_K = math.sqrt(2.0 / math.pi)


def _gelu(x):
    return 0.5 * x * (1.0 + jnp.tanh(_GELU_K * (x + 0.044715 * (x * x * x))))


def _gelu_grad(x):
    t = jnp.tanh(_GELU_K * (x + 0.044715 * (x * x * x)))
    return 0.5 * (1.0 + t) + 0.5 * x * (1.0 - t * t) * (_GELU_K * (1.0 + 3.0 * 0.044715 * (x * x)))


def _lru_conv(xpad_ref, lo, n, cw, cb):
    xc = xpad_ref[pl.ds(8 + lo, n), :] * cw[3:4, :] + cb
    for k in range(CONV_WIDTH - 1):
        xc = xc + xpad_ref[pl.ds(8 + lo - (CONV_WIDTH - 1 - k), n), :] * cw[k:k + 1, :]
    return xc


def _lru_gates(xc, wa, wx, ba, bx, sp):
    xb = xc.astype(BF16)
    r = jax.nn.sigmoid(_dot(xb, wa) + ba)
    i = jax.nn.sigmoid(_dot(xb, wx) + bx)
    la = (-LRU_C * r) * sp
    a = jnp.exp(la)
    mult = jnp.sqrt(_neg_expm1(2.0 * la))
    return r, i, a, mult


def _scan_carries(last_h, last_p, reverse):
    row = lax.broadcasted_iota(jnp.int32, last_h.shape, 0)
    carry = jnp.zeros_like(last_h)
    for _ in range(SCAN_CHUNKS - 1):
        nxt = last_h + last_p * carry
        if reverse:
            carry = jnp.where(row < SCAN_CHUNKS - 1, pltpu.roll(nxt, SCAN_CHUNKS - 1, axis=0), 0.0)
        else:
            carry = jnp.where(row > 0, pltpu.roll(nxt, 1, axis=0), 0.0)
    return carry


def _lru_forward_scan(a_ref, h_ref, p_ref, a_off, h_off, rc):
    zero = jnp.zeros((SCAN_CHUNKS, LANES), F32)

    def step(i, carry):
        hh, pp = carry
        av = a_ref[pl.ds(a_off + i, SCAN_CHUNKS, stride=rc), :]
        hh = av * hh + h_ref[pl.ds(h_off + i, SCAN_CHUNKS, stride=rc), :]
        pp = av * pp
        h_ref[pl.ds(h_off + i, SCAN_CHUNKS, stride=rc), :] = hh
        p_ref[pl.ds(i, SCAN_CHUNKS, stride=rc), :] = pp
        return hh, pp

    last_h, last_p = lax.fori_loop(0, rc, step, (zero, zero + 1.0))
    carry = _scan_carries(last_h, last_p, reverse=False)
    for c in range(SCAN_CHUNKS):
        rows = pl.ds(h_off + c * rc, rc)
        h_ref[rows, :] = h_ref[rows, :] + p_ref[pl.ds(c * rc, rc), :] * carry[c:c + 1, :]


def _rglru_fwd(xr, gr, cw, cb, wa, wx, ba, bx, lam):
    Tp, W = xr.shape
    nb = W // LANES
    rc = Tp // SCAN_CHUNKS

    def body(xr_ref, gr_ref, cw_ref, cb_ref, wa_ref, wx_ref, ba_ref, bx_ref, lam_ref, y_ref,
             xpad, a_s, h_s, p_s):
        xpad[0:8, :] = jnp.zeros((8, LANES), F32)
        xpad[pl.ds(8, Tp), :] = xr_ref[...]
        cw_, cb_ = cw_ref[...], cb_ref[...]
        sp = _softplus_neg(lam_ref[...])
        for c in range(SCAN_CHUNKS):
            xc = _lru_conv(xpad, c * rc, rc, cw_, cb_)
            _, i, a, mult = _lru_gates(xc, wa_ref[...], wx_ref[...], ba_ref[...], bx_ref[...], sp)
            a_s[pl.ds(c * rc, rc), :] = a
            h_s[pl.ds(c * rc, rc), :] = mult * (i * xc)
        _lru_forward_scan(a_s, h_s, p_s, 0, 0, rc)
        for c in range(SCAN_CHUNKS):
            rows = pl.ds(c * rc, rc)
            y_ref[rows, :] = h_s[rows, :] * _gelu(gr_ref[rows, :])

    col = pl.BlockSpec((Tp, LANES), lambda j: (0, j))
    vec = pl.BlockSpec((1, LANES), lambda j: (0, j))
    mat = pl.BlockSpec((None, LANES, LANES), lambda j: (j, 0, 0))
    return pl.pallas_call(
        body, name="rglru_fwd",
        out_shape=jax.ShapeDtypeStruct((Tp, W), F32),
        grid=(nb,),
        in_specs=[col, col, pl.BlockSpec((CONV_WIDTH, LANES), lambda j: (0, j)), vec, mat, mat, vec, vec, vec],
        out_specs=col,
        scratch_shapes=[pltpu.VMEM((Tp + 8, LANES), F32), pltpu.VMEM((Tp, LANES), F32),
                        pltpu.VMEM((Tp, LANES), F32), pltpu.VMEM((Tp, LANES), F32)],
        compiler_params=_cparams(),
    )(xr, gr, cw, cb, wa, wx, ba, bx, lam)


def _rglru_bwd(dy, xr, gr, cw, cb, wa, wx, ba, bx, lam):
    Tp, W = xr.shape
    nb = W // LANES
    rc = Tp // SCAN_CHUNKS

    def body(dy_ref, xr_ref, gr_ref, cw_ref, cb_ref, wa_ref, wx_ref, ba_ref, bx_ref, lam_ref,
             dxr_ref, dgr_ref, dcw_ref, dcb_ref, dwa_ref, dwx_ref, dba_ref, dbx_ref, dlam_ref,
             xpad, a_s, h_s, p_s, xc_s, g_s, dxc_s):
        zeros8 = jnp.zeros((8, LANES), F32)
        xpad[0:8, :] = zeros8
        xpad[pl.ds(8, Tp), :] = xr_ref[...]
        a_s[pl.ds(Tp, 8), :] = zeros8
        h_s[0:8, :] = zeros8
        dxc_s[pl.ds(Tp, 8), :] = zeros8
        cw_, cb_ = cw_ref[...], cb_ref[...]
        lam_ = lam_ref[...]
        sp = _softplus_neg(lam_)
        gate_args = (wa_ref[...], wx_ref[...], ba_ref[...], bx_ref[...], sp)

        for c in range(SCAN_CHUNKS):
            rows = pl.ds(c * rc, rc)
            xc = _lru_conv(xpad, c * rc, rc, cw_, cb_)
            xc_s[rows, :] = xc
            _, i, a, mult = _lru_gates(xc, *gate_args)
            a_s[rows, :] = a
            h_s[pl.ds(8 + c * rc, rc), :] = mult * (i * xc)
        _lru_forward_scan(a_s, h_s, p_s, 0, 8, rc)

        for c in range(SCAN_CHUNKS):
            rows = pl.ds(c * rc, rc)
            dyv, grv = dy_ref[rows, :], gr_ref[rows, :]
            g_s[rows, :] = dyv * _gelu(grv)
            dgr_ref[rows, :] = dyv * h_s[pl.ds(8 + c * rc, rc), :] * _gelu_grad(grv)

        zero = jnp.zeros((SCAN_CHUNKS, LANES), F32)

        def rstep(k, carry):
            gg, qq = carry
            i = rc - 1 - k
            av = a_s[pl.ds(i + 1, SCAN_CHUNKS, stride=rc), :]
            gg = g_s[pl.ds(i, SCAN_CHUNKS, stride=rc), :] + av * gg
            qq = av * qq
            g_s[pl.ds(i, SCAN_CHUNKS, stride=rc), :] = gg
            p_s[pl.ds(i, SCAN_CHUNKS, stride=rc), :] = qq
            return gg, qq

        first_g, first_q = lax.fori_loop(0, rc, rstep, (zero, zero + 1.0))
        carry = _scan_carries(first_g, first_q, reverse=True)
        for c in range(SCAN_CHUNKS):
            rows = pl.ds(c * rc, rc)
            g_s[rows, :] = g_s[rows, :] + p_s[rows, :] * carry[c:c + 1, :]

        dwa = jnp.zeros((LANES, LANES), F32)
        dwx = jnp.zeros((LANES, LANES), F32)
        dba = jnp.zeros((1, LANES), F32)
        dbx = jnp.zeros((1, LANES), F32)
        dsp = jnp.zeros((1, LANES), F32)
        for c in range(SCAN_CHUNKS):
            rows = pl.ds(c * rc, rc)
            xc = xc_s[rows, :]
            r, i, a, mult = _lru_gates(xc, *gate_args)
            gg = g_s[rows, :]
            da = gg * h_s[pl.ds(7 + c * rc, rc), :]
            d_s = gg * mult
            dmult = gg * (i * xc)
            dla = da * a - dmult * (a * a) / mult
            dr = dla * (-LRU_C * sp)
            dsp = dsp + jnp.sum(dla * (-LRU_C * r), axis=0, keepdims=True)
            dpr = dr * r * (1.0 - r)
            dpi = (d_s * xc) * i * (1.0 - i)
            dprb, dpib, xcb = dpr.astype(BF16), dpi.astype(BF16), xc.astype(BF16)
            dxc_s[rows, :] = d_s * i + _dot_nt(dprb, wa_ref[...]) + _dot_nt(dpib, wx_ref[...])
            dwa = dwa + _dot_tn(xcb, dprb)
            dwx = dwx + _dot_tn(xcb, dpib)
            dba = dba + jnp.sum(dpr, axis=0, keepdims=True)
            dbx = dbx + jnp.sum(dpi, axis=0, keepdims=True)
        dwa_ref[...] = dwa
        dwx_ref[...] = dwx
        dba_ref[...] = dba
        dbx_ref[...] = dbx
        dlam_ref[...] = dsp * (-jax.nn.sigmoid(-lam_))

        dcw = [jnp.zeros((1, LANES), F32) for _ in range(CONV_WIDTH)]
        dcb = jnp.zeros((1, LANES), F32)
        for c in range(SCAN_CHUNKS):
            rows = pl.ds(c * rc, rc)
            dxc = dxc_s[rows, :]
            dcb = dcb + jnp.sum(dxc, axis=0, keepdims=True)
            dxr = dxc * cw_[3:4, :]
            for k in range(CONV_WIDTH):
                back = CONV_WIDTH - 1 - k
                dcw[k] = dcw[k] + jnp.sum(dxc * xpad[pl.ds(8 + c * rc - back, rc), :], axis=0, keepdims=True)
                if back:
                    dxr = dxr + dxc_s[pl.ds(c * rc + back, rc), :] * cw_[k:k + 1, :]
            dxr_ref[rows, :] = dxr
        dcw_ref[...] = jnp.concatenate(dcw, axis=0)
        dcb_ref[...] = dcb

    col = pl.BlockSpec((Tp, LANES), lambda j: (0, j))
    vec = pl.BlockSpec((1, LANES), lambda j: (0, j))
    mat = pl.BlockSpec((None, LANES, LANES), lambda j: (j, 0, 0))
    cwspec = pl.BlockSpec((CONV_WIDTH, LANES), lambda j: (0, j))
    return pl.pallas_call(
        body, name="rglru_bwd",
        out_shape=(jax.ShapeDtypeStruct((Tp, W), F32), jax.ShapeDtypeStruct((Tp, W), F32),
                   jax.ShapeDtypeStruct((CONV_WIDTH, W), F32), jax.ShapeDtypeStruct((1, W), F32),
                   jax.ShapeDtypeStruct((nb, LANES, LANES), F32), jax.ShapeDtypeStruct((nb, LANES, LANES), F32),
                   jax.ShapeDtypeStruct((1, W), F32), jax.ShapeDtypeStruct((1, W), F32),
                   jax.ShapeDtypeStruct((1, W), F32)),
        grid=(nb,),
        in_specs=[col, col, col, cwspec, vec, mat, mat, vec, vec, vec],
        out_specs=[col, col, cwspec, vec, mat, mat, vec, vec, vec],
        scratch_shapes=[pltpu.VMEM((Tp + 8, LANES), F32), pltpu.VMEM((Tp + 8, LANES), F32),
                        pltpu.VMEM((Tp + 8, LANES), F32), pltpu.VMEM((Tp, LANES), F32),
                        pltpu.VMEM((Tp, LANES), F32), pltpu.VMEM((Tp, LANES), F32),
                        pltpu.VMEM((Tp + 8, LANES), F32)],
        compiler_params=_cparams(),
    )(dy, xr, gr, cw, cb, wa, wx, ba, bx, lam)


def _rope(x, cosm, sin_hi, sin_lo):
    return x * cosm + pltpu.roll(x, QK_ROPE // 2, axis=1) * sin_hi + pltpu.roll(x, LANES - QK_ROPE // 2, axis=1) * sin_lo


def _rope_inverse(x, cosm, sin_hi, sin_lo):
    return x * cosm - pltpu.roll(x, QK_ROPE // 2, axis=1) * sin_hi - pltpu.roll(x, LANES - QK_ROPE // 2, axis=1) * sin_lo


def _mla_proj_fwd(cq, ckv, kr, qg, kg, wuq, wkv, cosm, sin_hi, sin_lo):
    Tp, QL = cq.shape
    KL = ckv.shape[1]
    H = MLA_HEADS
    half = LANES // 2

    def body(cq_ref, ckv_ref, kr_ref, qg_ref, kg_ref, wuq_ref, wkv_ref, c_ref, s1_ref, s2_ref,
             q_ref, k_ref, v_ref):
        tabs = (c_ref[...], s1_ref[...], s2_ref[...])
        lane = lax.broadcasted_iota(jnp.int32, (ROW_TILE, LANES), 1)
        low = lane < half
        cqn, _, _ = _rms_fwd(cq_ref[...], qg_ref[...])
        q = _dot(cqn.astype(BF16), wuq_ref[...])
        ckvn, _, _ = _rms_fwd(ckv_ref[...], kg_ref[...])
        kv = _dot(ckvn.astype(BF16), wkv_ref[...])
        krr = _rope(kr_ref[...], *tabs)
        for hd in range(H):
            sl = slice(hd * LANES, (hd + 1) * LANES)
            q_ref[hd] = _rope(q[:, sl], *tabs).astype(BF16)
            k_ref[hd] = jnp.where(low, kv[:, sl], krr).astype(BF16)
        for p in range(H // 2):
            even = kv[:, 2 * p * LANES:(2 * p + 1) * LANES]
            odd = kv[:, (2 * p + 1) * LANES:(2 * p + 2) * LANES]
            v_ref[:, p * LANES:(p + 1) * LANES] = jnp.where(low, pltpu.roll(even, half, axis=1), odd).astype(BF16)

    heads = pl.BlockSpec((H, ROW_TILE, LANES), lambda i: (0, i, 0))
    return pl.pallas_call(
        body, name="mla_proj_fwd",
        out_shape=(jax.ShapeDtypeStruct((H, Tp, LANES), BF16), jax.ShapeDtypeStruct((H, Tp, LANES), BF16),
                   jax.ShapeDtypeStruct((Tp, H * half), BF16)),
        grid=(Tp // ROW_TILE,),
        in_specs=[_rows(QL), _rows(KL), _rows(LANES), _const((1, QL)), _const((1, KL)),
                  _resident(wuq.shape), _resident(wkv.shape), _rows(LANES), _rows(LANES), _rows(LANES)],
        out_specs=[heads, heads, _rows(H * half)],
        compiler_params=_cparams(),
    )(cq, ckv, kr, qg, kg, wuq, wkv, cosm, sin_hi, sin_lo)


def _mla_proj_bwd(dq, dk, dv, cq, ckv, qg, kg, wuq, wkv, cosm, sin_hi, sin_lo):
    H, Tp, _ = dq.shape
    QL, KL = cq.shape[1], ckv.shape[1]
    half = LANES // 2

    def body(dq_ref, dk_ref, dv_ref, cq_ref, ckv_ref, qg_ref, kg_ref, wuq_ref, wkv_ref,
             c_ref, s1_ref, s2_ref,
             dcq_ref, dckv_ref, dkr_ref, dqa_ref, dkva_ref, cqn_ref, ckvn_ref, dqg_ref, dkg_ref):
        first = pl.program_id(0) == 0
        tabs = (c_ref[...], s1_ref[...], s2_ref[...])
        lane = lax.broadcasted_iota(jnp.int32, (ROW_TILE, LANES), 1)
        low = lane < half
        rope_lanes = jnp.logical_and(lane >= QK_NOPE, lane < QK_NOPE + QK_ROPE)
        dkr = jnp.zeros((ROW_TILE, LANES), F32)
        for hd in range(H):
            sl = slice(hd * LANES, (hd + 1) * LANES)
            dqa_ref[:, sl] = _rope_inverse(dq_ref[hd], *tabs).astype(BF16)
            dkh = dk_ref[hd]
            dvp = dv_ref[:, (hd // 2) * LANES:(hd // 2 + 1) * LANES]
            dvh = pltpu.roll(dvp, half, axis=1) if hd % 2 == 0 else dvp
            dkva_ref[:, sl] = jnp.where(low, dkh, dvh).astype(BF16)
            dkr = dkr + jnp.where(rope_lanes, dkh, 0.0)
        dkr_ref[...] = _rope_inverse(dkr, *tabs)
        qg, kg = qg_ref[...], kg_ref[...]
        cqs, cqn, cqr = _rms_fwd(cq_ref[...], qg)
        cqn_ref[...] = cqs.astype(BF16)
        dcq, dqg = _rms_bwd(cqn, cqr, qg, _dot_nt(dqa_ref[...], wuq_ref[...]))
        dcq_ref[...] = dcq
        cks, ckn, ckr = _rms_fwd(ckv_ref[...], kg)
        ckvn_ref[...] = cks.astype(BF16)
        dckv, dkg = _rms_bwd(ckn, ckr, kg, _dot_nt(dkva_ref[...], wkv_ref[...]))
        dckv_ref[...] = dckv
        _accumulate(dqg_ref, dqg, first)
        _accumulate(dkg_ref, dkg, first)

    heads = pl.BlockSpec((H, ROW_TILE, LANES), lambda i: (0, i, 0))
    return pl.pallas_call(
        body, name="mla_proj_bwd",
        out_shape=(jax.ShapeDtypeStruct((Tp, QL), F32), jax.ShapeDtypeStruct((Tp, KL), F32),
                   jax.ShapeDtypeStruct((Tp, LANES), F32), jax.ShapeDtypeStruct((Tp, H * LANES), BF16),
                   jax.ShapeDtypeStruct((Tp, H * LANES), BF16),
                   jax.ShapeDtypeStruct((Tp, QL), BF16), jax.ShapeDtypeStruct((Tp, KL), BF16),
                   jax.ShapeDtypeStruct((1, QL), F32), jax.ShapeDtypeStruct((1, KL), F32)),
        grid=(Tp // ROW_TILE,),
        in_specs=[heads, heads, _rows(H * half), _rows(QL), _rows(KL), _const((1, QL)), _const((1, KL)),
                  _resident(wuq.shape), _resident(wkv.shape), _rows(LANES), _rows(LANES), _rows(LANES)],
        out_specs=[_rows(QL), _rows(KL), _rows(LANES), _rows(H * LANES), _rows(H * LANES),
                   _rows(QL), _rows(KL), _const((1, QL)), _const((1, KL))],
        compiler_params=_cparams(),
    )(dq, dk, dv, cq, ckv, qg, kg, wuq, wkv, cosm, sin_hi, sin_lo)


_NEG = -1e30
_ATT_SCALE = (QK_NOPE + QK_ROPE) ** -0.5


def _causal(s, q0, k0, transposed):
    r = lax.broadcasted_iota(jnp.int32, s.shape, 0)
    c = lax.broadcasted_iota(jnp.int32, s.shape, 1)
    keep = (k0 + r <= q0 + c) if transposed else (k0 + c <= q0 + r)
    return jnp.where(keep, s, _NEG)


def _attn_fwd(q, k, v):
    H, Tp, _ = q.shape
    t = ROW_TILE
    half = LANES // 2

    def body(q_ref, k_ref, v_ref, y_ref, lse_ref):
        i = pl.program_id(1)
        outs = []
        for hh in range(2):
            qv = q_ref[hh]

            def step(j, carry, masked):
                m, l, acc = carry
                kv = k_ref[hh, pl.ds(j * t, t), :]
                vv = v_ref[pl.ds(j * t, t), :]
                s = _dot_nt(qv, kv) * _ATT_SCALE
                if masked:
                    s = _causal(s, i * t, j * t, False)
                m_new = jnp.maximum(m, jnp.max(s, axis=-1, keepdims=True))
                alpha = jnp.exp(m - m_new)
                p = jnp.exp(s - m_new)
                l = alpha * l + jnp.sum(p, axis=-1, keepdims=True)
                acc = alpha * acc + _dot(p.astype(BF16), vv)
                return m_new, l, acc

            init = (jnp.full((t, 1), _NEG, F32), jnp.zeros((t, 1), F32), jnp.zeros((t, LANES), F32))
            carry = lax.fori_loop(0, i, functools.partial(step, masked=False), init)
            m, l, acc = step(i, carry, True)
            outs.append(acc / l)
            lse_ref[hh] = m + jnp.log(l)
        lane = lax.broadcasted_iota(jnp.int32, (t, LANES), 1)
        y_ref[...] = jnp.where(lane < half, outs[0], outs[1])

    return pl.pallas_call(
        body, name="attn_fwd",
        out_shape=(jax.ShapeDtypeStruct((Tp, H * half), F32), jax.ShapeDtypeStruct((H, Tp, 1), F32)),
        grid=(H // 2, Tp // t),
        in_specs=[pl.BlockSpec((2, t, LANES), lambda p, i: (p, i, 0)),
                  pl.BlockSpec((2, Tp, LANES), lambda p, i: (p, 0, 0)),
                  pl.BlockSpec((Tp, LANES), lambda p, i: (0, p))],
        out_specs=[pl.BlockSpec((t, LANES), lambda p, i: (i, p)),
                   pl.BlockSpec((2, t, 1), lambda p, i: (p, i, 0))],
        compiler_params=_cparams(),
    )(q, k, v)


def _attn_bwd_q(q, k, v, dy, y, lse):
    H, Tp, _ = q.shape
    t = ROW_TILE
    half = LANES // 2

    def body(q_ref, k_ref, v_ref, dy_ref, y_ref, lse_ref, dq_ref, delta_ref):
        i = pl.program_id(1)
        lane = lax.broadcasted_iota(jnp.int32, (t, LANES), 1)
        for hh in range(2):
            mine = (lane < half) if hh == 0 else (lane >= half)
            do = jnp.where(mine, dy_ref[...], 0.0)
            delta = jnp.sum(do * y_ref[...], axis=-1, keepdims=True)
            delta_ref[hh] = delta
            dob = do.astype(BF16)
            qv = q_ref[hh]
            lse_ = lse_ref[hh]

            def step(j, dq, masked):
                kv = k_ref[hh, pl.ds(j * t, t), :]
                vv = v_ref[pl.ds(j * t, t), :]
                s = _dot_nt(qv, kv) * _ATT_SCALE
                if masked:
                    s = _causal(s, i * t, j * t, False)
                p = jnp.exp(s - lse_)
                dp = _dot_nt(dob, vv)
                ds = p * (dp - delta) * _ATT_SCALE
                return dq + _dot(ds.astype(BF16), kv)

            dq = lax.fori_loop(0, i, functools.partial(step, masked=False), jnp.zeros((t, LANES), F32))
            dq_ref[hh] = step(i, dq, True)

    return pl.pallas_call(
        body, name="attn_bwd_q",
        out_shape=(jax.ShapeDtypeStruct((H, Tp, LANES), F32), jax.ShapeDtypeStruct((H, Tp, 1), F32)),
        grid=(H // 2, Tp // t),
        in_specs=[pl.BlockSpec((2, t, LANES), lambda p, i: (p, i, 0)),
                  pl.BlockSpec((2, Tp, LANES), lambda p, i: (p, 0, 0)),
                  pl.BlockSpec((Tp, LANES), lambda p, i: (0, p)),
                  pl.BlockSpec((t, LANES), lambda p, i: (i, p)),
                  pl.BlockSpec((t, LANES), lambda p, i: (i, p)),
                  pl.BlockSpec((2, t, 1), lambda p, i: (p, i, 0))],
        out_specs=[pl.BlockSpec((2, t, LANES), lambda p, i: (p, i, 0)),
                   pl.BlockSpec((2, t, 1), lambda p, i: (p, i, 0))],
        compiler_params=_cparams(),
    )(q, k, v, dy, y, lse)


def _attn_bwd_kv(q, k, v, dy, lse_row, delta_row):
    H, Tp, _ = q.shape
    t = ROW_TILE
    half = LANES // 2
    nq = Tp // t

    def body(q_ref, k_ref, v_ref, dy_ref, lse_ref, delta_ref, dk_ref, dv_ref):
        j = pl.program_id(1)
        lane = lax.broadcasted_iota(jnp.int32, (t, LANES), 1)
        vv = v_ref[...]
        dv_total = jnp.zeros((t, LANES), F32)
        for hh in range(2):
            mine = (lane < half) if hh == 0 else (lane >= half)
            kv = k_ref[hh]

            def step(i, carry, masked):
                dk, dv = carry
                cols = pl.ds(pl.multiple_of(i * t, LANES), t)
                qv = q_ref[hh, pl.ds(i * t, t), :]
                dob = jnp.where(mine, dy_ref[pl.ds(i * t, t), :], 0.0).astype(BF16)
                st = _dot_nt(kv, qv) * _ATT_SCALE
                if masked:
                    st = _causal(st, i * t, j * t, True)
                pt = jnp.exp(st - lse_ref[hh, :, cols])
                dv = dv + _dot(pt.astype(BF16), dob)
                dpt = _dot_nt(vv, dob)
                dst = pt * (dpt - delta_ref[hh, :, cols]) * _ATT_SCALE
                dk = dk + _dot(dst.astype(BF16), qv)
                return dk, dv

            zero = jnp.zeros((t, LANES), F32)
            carry = step(j, (zero, zero), True)
            dk, dv = lax.fori_loop(j + 1, nq, functools.partial(step, masked=False), carry)
            dk_ref[hh] = dk
            dv_total = dv_total + dv
        dv_ref[...] = dv_total

    return pl.pallas_call(
        body, name="attn_bwd_kv",
        out_shape=(jax.ShapeDtypeStruct((H, Tp, LANES), F32), jax.ShapeDtypeStruct((Tp, H * half), F32)),
        grid=(H // 2, nq),
        in_specs=[pl.BlockSpec((2, Tp, LANES), lambda p, j: (p, 0, 0)),
                  pl.BlockSpec((2, t, LANES), lambda p, j: (p, j, 0)),
                  pl.BlockSpec((t, LANES), lambda p, j: (j, p)),
                  pl.BlockSpec((Tp, LANES), lambda p, j: (0, p)),
                  pl.BlockSpec((2, 1, Tp), lambda p, j: (p, 0, 0)),
                  pl.BlockSpec((2, 1, Tp), lambda p, j: (p, 0, 0))],
        out_specs=[pl.BlockSpec((2, t, LANES), lambda p, j: (p, j, 0)),
                   pl.BlockSpec((t, LANES), lambda p, j: (j, p))],
        compiler_params=_cparams(),
    )(q, k, v, dy, lse_row, delta_row)


def _loss_head(h, target, lo, hi):
    Tp, D = h.shape

    def body(h_ref, t_ref, dh_ref, loss_ref):
        i = pl.program_id(0)
        row = i * ROW_TILE + lax.broadcasted_iota(jnp.int32, (ROW_TILE, 1), 0)
        live = jnp.logical_and(row >= lo, row < hi)
        err = jnp.where(live, h_ref[...] - t_ref[...], 0.0)
        dh_ref[...] = err / D
        part = 0.5 * jnp.sum(jnp.mean(err * err, axis=-1, keepdims=True), axis=0, keepdims=True)
        _accumulate(loss_ref, jnp.broadcast_to(part, (8, LANES)), i == 0)

    return pl.pallas_call(
        body, name="loss_head",
        out_shape=(jax.ShapeDtypeStruct((Tp, D), F32), jax.ShapeDtypeStruct((8, LANES), F32)),
        grid=(Tp // ROW_TILE,),
        in_specs=[_rows(D), _rows(D)],
        out_specs=[_rows(D), _const((8, LANES))],
        compiler_params=_cparams(),
    )(h, target)


def _pack(arrays, row_multiple=8):
    flat = jnp.concatenate([a.reshape(-1).astype(F32) for a in arrays])
    quantum = LANES * row_multiple
    total = -(-flat.shape[0] // quantum) * quantum
    flat = jnp.pad(flat, (0, total - flat.shape[0]))
    return flat.reshape(-1, LANES)


def _unpack(buf, shapes):
    flat = buf.reshape(-1)
    out, off = [], 0
    for shp in shapes:
        n = math.prod(shp)
        out.append(flat[off:off + n].reshape(tuple(shp)))
        off += n
    return out


def _block_diag_blocks(w):
    nh, d, _ = w.shape
    per = LANES // d
    eye = jnp.eye(per, dtype=w.dtype)
    g = w.reshape(nh // per, per, d, d)
    return jnp.einsum('bpij,pq->bpiqj', g, eye).reshape(nh // per, LANES, LANES)


def _block_diag_extract(blocks, d):
    nb = blocks.shape[0]
    per = LANES // d
    return jnp.stack([blocks[b, p * d:(p + 1) * d, p * d:(p + 1) * d] for b in range(nb) for p in range(per)])


def _step(a):
    x = a['x'][0]
    target = a['loss_target'][0]
    seq, D = x.shape
    n_meta = a['meta_tokens'].shape[0]
    T = seq + n_meta
    Tp = -(-T // ROW_TILE) * ROW_TILE
    L = a['ffn1_pre_g'].shape[0]
    W = a['lru_conv_b'].shape[1]
    QL = a['mla_q_norm_g'].shape[1]
    KL = a['mla_kv_norm_g'].shape[1]
    H = MLA_HEADS
    LD = a['lru_w_a'].shape[2]
    n_in = 2 * W + QL + KL
    assert W % LANES == 0 and QL % LANES == 0 and KL % LANES == 0 and LANES % LD == 0
    assert a['mla_w_ukv'].shape[2] == LANES and a['mla_w_uq'].shape[2] == QK_NOPE + QK_ROPE

    ax, ay, ac = (lax.axis_index(n) for n in MESH_AXES)
    me = 4 * ax + 2 * ay + ac
    core = ac.reshape(1).astype(jnp.int32)
    chip = (2 * ax + ay).reshape(1).astype(jnp.int32)

    gathered = dict(zip(BIG + SMALL_SHARDED,
                        _all_gather([a[n].astype(BF16) for n in BIG] + [a[n] for n in SMALL_SHARDED])))
    kinds = {'ffn1_w_gate': 'cols', 'ffn1_w_up': 'cols', 'ffn1_w_down': 'rows', 'w_in': 'w_in', 'mla_w_uq': 'heads',
             'mla_w_ukv': 'cols', 'w_out': 'rows', 'ffn2_w_gate': 'cols', 'ffn2_w_up': 'cols', 'ffn2_w_down': 'rows'}
    big = {n: _assemble(gathered[n], kinds[n], n_in) for n in BIG}
    gcw, gmeta = gathered['lru_conv_w'], gathered['meta_tokens']
    conv_w = jnp.moveaxis(gcw, 0, 2).reshape(L, CONV_WIDTH, W)
    meta = jnp.moveaxis(gmeta, 0, 1).reshape(n_meta, D)
    wa_blk = jax.vmap(_block_diag_blocks)(a['lru_w_a']).astype(BF16)
    wx_blk = jax.vmap(_block_diag_blocks)(a['lru_w_x']).astype(BF16)
    widths = [W, W, QL, KL, LANES]

    QH = QK_NOPE + QK_ROPE
    pos = jnp.arange(Tp, dtype=F32)
    inv_freq = 1.0 / (ROPE_THETA ** (jnp.arange(0, QK_ROPE, 2, dtype=F32) / QK_ROPE))
    ang = pos[:, None] * inv_freq[None, :]
    cos, sin = jnp.cos(ang), jnp.sin(ang)
    hr = QK_ROPE // 2
    z = lambda n: jnp.zeros((Tp, n), F32)
    cosm = jnp.concatenate([jnp.ones((Tp, QK_NOPE), F32), cos, cos, z(LANES - QH)], axis=1)
    sin_hi = jnp.concatenate([z(QK_NOPE + hr), sin, z(LANES - QH)], axis=1)
    sin_lo = jnp.concatenate([z(QK_NOPE), -sin, z(LANES - QK_NOPE - hr)], axis=1)
    tabs = (cosm, sin_hi, sin_lo)

    vec = lambda name, l: a[name][l][None, :]

    h = jnp.concatenate([meta, x, jnp.zeros((Tp - T, D), F32)], axis=0)
    tpad = jnp.concatenate([jnp.zeros((n_meta, D), F32), target, jnp.zeros((Tp - T, D), F32)], axis=0)
    saved = []
    for l in range(L):
        s = {'h0': h}
        h, s['a1'], s['b1'], s['f1'] = _ffn_fwd(h, vec('ffn1_pre_g', l), vec('ffn1_post_g', l),
                                                big['ffn1_w_gate'][l], big['ffn1_w_up'][l], big['ffn1_w_down'][l])
        s['h1'] = h
        xr, gr, cq, ckv, kr = _mix_in_fwd(h, vec('mix_pre_g', l), big['w_in'][l], widths)
        s.update(xr=xr, gr=gr, cq=cq, ckv=ckv)
        y_lru = _rglru_fwd(xr, gr, conv_w[l], vec('lru_conv_b', l), wa_blk[l], wx_blk[l], vec('lru_b_a', l),
                           vec('lru_b_x', l), vec('lru_lambda', l))
        q, k, v = _mla_proj_fwd(cq, ckv, kr, vec('mla_q_norm_g', l), vec('mla_kv_norm_g', l),
                                big['mla_w_uq'][l], big['mla_w_ukv'][l], *tabs)
        y_mla, lse = _attn_fwd(q, k, v)
        s.update(q=q, k=k, v=v, lse=lse, y_lru=y_lru, y_mla=y_mla)
        h, s['y'] = _mix_out_fwd(h, y_lru, y_mla, vec('lru_out_g', l), vec('mla_out_g', l), vec('mix_post_g', l),
                                 big['w_out'][l])
        s['h2'] = h
        h, s['a2'], s['b2'], s['f2'] = _ffn_fwd(h, vec('ffn2_pre_g', l), vec('ffn2_post_g', l),
                                                big['ffn2_w_gate'][l], big['ffn2_w_up'][l], big['ffn2_w_down'][l])
        saved.append(s)

    dh, loss_tile = _loss_head(h, tpad, n_meta, T)
    loss = lax.psum(loss_tile[0, 0], MESH_AXES)

    gw = {n: [None] * L for n in REPLICATED + ['lru_conv_w']}
    gbuf = {n: jnp.zeros((N_DEV,) + a[n].shape, BF16) for n in BIG}

    def ffn_bwd(f, l, dh, h_in, a_, b_, f_):
        da, db, act, dfb, dpost = _ffn_bwd_down(dh, a_, b_, f_, vec(f + '_post_g', l), big[f + '_w_down'][l])
        dh_in, u, dpre = _ffn_bwd_up(dh, h_in, da, db, vec(f + '_pre_g', l), big[f + '_w_gate'][l], big[f + '_w_up'][l])
        gbuf[f + '_w_gate'] = _wgrad(u, da, gbuf[f + '_w_gate'], l, 'cols')
        gbuf[f + '_w_up'] = _wgrad(u, db, gbuf[f + '_w_up'], l, 'cols')
        gbuf[f + '_w_down'] = _wgrad(act, dfb, gbuf[f + '_w_down'], l, 'rows')
        gw[f + '_pre_g'][l] = dpre[0]
        gw[f + '_post_g'][l] = dpost[0]
        return dh_in

    for l in reversed(range(L)):
        s = saved[l]
        dh = ffn_bwd('ffn2', l, dh, s['h2'], s['a2'], s['b2'], s['f2'])
        dyl, dym, dyb, cat, dlg, dmg, dpg = _mix_out_bwd(
            dh, s['y'], s['y_lru'], s['y_mla'], vec('lru_out_g', l), vec('mla_out_g', l), vec('mix_post_g', l),
            big['w_out'][l])
        gbuf['w_out'] = _wgrad(cat, dyb, gbuf['w_out'], l, 'rows')
        gw['lru_out_g'][l], gw['mla_out_g'][l], gw['mix_post_g'][l] = dlg[0], dmg[0], dpg[0]
        dxr, dgr, dcw, dcb, dwa, dwx, dba, dbx, dlam = _rglru_bwd(
            dyl, s['xr'], s['gr'], conv_w[l], vec('lru_conv_b', l), wa_blk[l], wx_blk[l],
            vec('lru_b_a', l), vec('lru_b_x', l), vec('lru_lambda', l))
        gw['lru_conv_w'][l], gw['lru_conv_b'][l] = dcw, dcb[0]
        gw['lru_w_a'][l], gw['lru_w_x'][l] = _block_diag_extract(dwa, LD), _block_diag_extract(dwx, LD)
        gw['lru_b_a'][l], gw['lru_b_x'][l], gw['lru_lambda'][l] = dba[0], dbx[0], dlam[0]
        dq, delta = _attn_bwd_q(s['q'], s['k'], s['v'], dym, s['y_mla'], s['lse'])
        dk, dv = _attn_bwd_kv(s['q'], s['k'], s['v'], dym, s['lse'].reshape(H, 1, Tp), delta.reshape(H, 1, Tp))
        dcq, dckv, dkr, dqa, dkva, cqn, ckvn, dqg, dkg = _mla_proj_bwd(
            dq, dk, dv, s['cq'], s['ckv'], vec('mla_q_norm_g', l), vec('mla_kv_norm_g', l),
            big['mla_w_uq'][l], big['mla_w_ukv'][l], *tabs)
        gw['mla_q_norm_g'][l], gw['mla_kv_norm_g'][l] = dqg[0], dkg[0]
        gbuf['mla_w_uq'] = _wgrad(cqn, dqa, gbuf['mla_w_uq'], l, 'heads')
        gbuf['mla_w_ukv'] = _wgrad(ckvn, dkva, gbuf['mla_w_ukv'], l, 'cols')
        dh, dz, hn, dmg_ = _mix_in_bwd(dh, s['h1'], [dxr, dgr, dcq, dckv, dkr], vec('mix_pre_g', l), big['w_in'][l])
        gw['mix_pre_g'][l] = dmg_[0]
        gbuf['w_in'] = _wgrad(hn, dz, gbuf['w_in'], l, 'w_in', n_in)
        dh = ffn_bwd('ffn1', l, dh, s['h0'], s['a1'], s['b1'], s['f1'])

    grad_x = dh[n_meta:T][None]
    gsmall = {n: jnp.stack(v) for n, v in gw.items()}
    gsmall['meta_tokens'] = dh[:n_meta]

    grads, delta, new_m, new_v = {}, {}, {}, {}
    parts = [gbuf[n] for n in BIG]
    got = _swap_with_sibling(parts)
    as3 = lambda t: t.reshape(t.shape[0], -1, t.shape[-1])
    chip_partial = [_chip_partial(as3(p), as3(g), core) for p, g in zip(parts, got)]
    got2 = _exchange_chips(chip_partial)
    for n, cp, g2 in zip(BIG, chip_partial, got2):
        shp = a[n].shape
        two_d = lambda t: t.reshape(-1, shp[-1])
        outs = _final_sum_adamw(cp, g2, chip, two_d(a[n]), two_d(a['m_' + n]), two_d(a['v_' + n]))
        grads[n], delta[n], new_m[n], new_v[n] = (o.reshape(shp) for o in outs)

    small_names = REPLICATED + SMALL_SHARDED
    small_full_shapes = [gsmall[n].shape for n in small_names]
    gsm = _sum_devices(_all_gather([_pack([gsmall[n] for n in small_names])])[0])
    for n, g in zip(small_names, _unpack(gsm, small_full_shapes)):
        if n == 'lru_conv_w':
            g = lax.dynamic_index_in_dim(jnp.moveaxis(g.reshape(L, CONV_WIDTH, N_DEV, W // N_DEV), 2, 0), me, 0, False)
        elif n == 'meta_tokens':
            g = lax.dynamic_index_in_dim(jnp.moveaxis(g.reshape(n_meta, N_DEV, D // N_DEV), 1, 0), me, 0, False)
        grads[n] = g
    for group in (REPLICATED, SMALL_SHARDED):
        shapes = [a[n].shape for n in group]
        packed = [_pack([a[p + n] for n in group]) for p in ('', 'm_', 'v_')]
        packed.append(_pack([grads[n] for n in group]))
        outs = [_unpack(o, shapes) for o in _adamw(*packed)]
        for i, n in enumerate(group):
            delta[n], new_m[n], new_v[n] = outs[0][i], outs[1][i], outs[2][i]

    return (loss, grad_x, *[grads[n] for n in WEIGHTS], *[delta[n] for n in WEIGHTS],
            *[new_m[n] for n in WEIGHTS], *[new_v[n] for n in WEIGHTS])


def kernel(x, meta_tokens, ffn1_pre_g, ffn1_w_gate, ffn1_w_up, ffn1_w_down, ffn1_post_g, mix_pre_g, w_in, lru_conv_w, lru_conv_b, lru_w_a, lru_b_a, lru_w_x, lru_b_x, lru_lambda, mla_q_norm_g, mla_w_uq, mla_kv_norm_g, mla_w_ukv, lru_out_g, mla_out_g, w_out, mix_post_g, ffn2_pre_g, ffn2_w_gate, ffn2_w_up, ffn2_w_down, ffn2_post_g, loss_target, m_meta_tokens, m_ffn1_pre_g, m_ffn1_w_gate, m_ffn1_w_up, m_ffn1_w_down, m_ffn1_post_g, m_mix_pre_g, m_w_in, m_lru_conv_w, m_lru_conv_b, m_lru_w_a, m_lru_b_a, m_lru_w_x, m_lru_b_x, m_lru_lambda, m_mla_q_norm_g, m_mla_w_uq, m_mla_kv_norm_g, m_mla_w_ukv, m_lru_out_g, m_mla_out_g, m_w_out, m_mix_post_g, m_ffn2_pre_g, m_ffn2_w_gate, m_ffn2_w_up, m_ffn2_w_down, m_ffn2_post_g, v_meta_tokens, v_ffn1_pre_g, v_ffn1_w_gate, v_ffn1_w_up, v_ffn1_w_down, v_ffn1_post_g, v_mix_pre_g, v_w_in, v_lru_conv_w, v_lru_conv_b, v_lru_w_a, v_lru_b_a, v_lru_w_x, v_lru_b_x, v_lru_lambda, v_mla_q_norm_g, v_mla_w_uq, v_mla_kv_norm_g, v_mla_w_ukv, v_lru_out_g, v_mla_out_g, v_w_out, v_mix_post_g, v_ffn2_pre_g, v_ffn2_w_gate, v_ffn2_w_up, v_ffn2_w_down, v_ffn2_post_g):
    return _step(dict(locals()))
```

```python
import functools
import math

import jax
import jax.numpy as jnp
from jax import lax
from jax.experimental import pallas as pl
from jax.experimental.pallas import tpu as pltpu

F32 = jnp.float32
BF16 = jnp.bfloat16

EPS = 1e-6
N_DEV = 8
LANES = 128
ROW_TILE = 384
SCAN_CHUNKS = 8
VMEM_LIMIT = 56 * 1024 * 1024
MM_ACC_BYTES = 8 * 1024 * 1024

LRU_C = 8.0
CONV_WIDTH = 4
MLA_HEADS = 8
QK_NOPE = 64
QK_ROPE = 32
ROPE_THETA = 10000.0

ADAM_LR = 0.001
ADAM_B1 = 0.9
ADAM_B2 = 0.999
ADAM_EPS = 1e-08
ADAM_WD = 0.01
ADAM_STEP = 10

MESH_AXES = ("x", "y", "c")
MESH = pl.DeviceIdType.MESH

WEIGHTS = ['meta_tokens', 'ffn1_pre_g', 'ffn1_w_gate', 'ffn1_w_up', 'ffn1_w_down', 'ffn1_post_g', 'mix_pre_g', 'w_in',
           'lru_conv_w', 'lru_conv_b', 'lru_w_a', 'lru_b_a', 'lru_w_x', 'lru_b_x', 'lru_lambda', 'mla_q_norm_g',
           'mla_w_uq', 'mla_kv_norm_g', 'mla_w_ukv', 'lru_out_g', 'mla_out_g', 'w_out', 'mix_post_g', 'ffn2_pre_g',
           'ffn2_w_gate', 'ffn2_w_up', 'ffn2_w_down', 'ffn2_post_g']
BIG = ['ffn1_w_gate', 'ffn1_w_up', 'ffn1_w_down', 'w_in', 'mla_w_uq', 'mla_w_ukv', 'w_out',
       'ffn2_w_gate', 'ffn2_w_up', 'ffn2_w_down']
SMALL_SHARDED = ['lru_conv_w', 'meta_tokens']
REPLICATED = [n for n in WEIGHTS if n not in BIG and n not in SMALL_SHARDED]


def _cparams(**kw):
    return pltpu.CompilerParams(vmem_limit_bytes=VMEM_LIMIT, **kw)


def _resident(shape):
    nd = len(shape)
    return pl.BlockSpec(shape, lambda *_: (0,) * nd, pipeline_mode=pl.Buffered(1))


def _rows(cols, tm=ROW_TILE):
    return pl.BlockSpec((tm, cols), lambda i: (i, 0))


def _const(shape):
    nd = len(shape)
    return pl.BlockSpec(shape, lambda *_: (0,) * nd)


def _dot(a, b):
    return jnp.dot(a, b, preferred_element_type=F32)


def _dot_nt(a, b):
    return lax.dot_general(a, b, (((1,), (1,)), ((), ())), preferred_element_type=F32)


def _dot_tn(a, b):
    return lax.dot_general(a, b, (((0,), (0,)), ((), ())), preferred_element_type=F32)


def _rms_fwd(x, g):
    rinv = lax.rsqrt(jnp.mean(x * x, axis=-1, keepdims=True) + EPS)
    xn = x * rinv
    return xn * g, xn, rinv


def _rms_bwd(xn, rinv, g, dy):
    dxn = dy * g
    dx = rinv * (dxn - xn * jnp.mean(dxn * xn, axis=-1, keepdims=True))
    return dx, jnp.sum(dy * xn, axis=0, keepdims=True)


def _accumulate(ref, val, first):
    @pl.when(first)
    def _():
        ref[...] = val

    @pl.when(jnp.logical_not(first))
    def _():
        ref[...] += val


def _all_gather(shards):
    n = len(shards)

    def body(*refs):
        x_refs, out_refs = refs[:n], refs[n:2 * n]
        send_sems, recv_sems, local_sems = refs[2 * n:]
        x, y, c = lax.axis_index("x"), lax.axis_index("y"), lax.axis_index("c")
        me, sibling = (x, y, c), (x, y, 1 - c)
        chips = [(1 - x, y), (x, 1 - y), (1 - x, 1 - y)]

        def copy(a, k, block, to, src=None):
            dst = out_refs[a].at[4 * block[0] + 2 * block[1] + block[2]]
            return pltpu.make_async_remote_copy(
                src_ref=dst if src is None else src, dst_ref=dst,
                send_sem=send_sems.at[7 * a + k], recv_sem=recv_sems.at[7 * a + k], device_id=to, device_id_type=MESH)

        mine = [pltpu.make_async_copy(x_refs[a], out_refs[a].at[4 * x + 2 * y + c], local_sems.at[a]) for a in range(n)]
        for cp in mine:
            cp.start()
        first = []
        for a in range(n):
            first.append(copy(a, 0, me, sibling, src=x_refs[a]))
            first += [copy(a, 1 + j, me, (*chip, c), src=x_refs[a]) for j, chip in enumerate(chips)]
        for cp in first:
            cp.start()
        passed = []
        for j, chip in enumerate(chips):
            for a in range(n):
                copy(a, 1 + j, (*chip, c), me).wait_recv()
                cp = copy(a, 4 + j, (*chip, c), sibling)
                cp.start()
                passed.append(cp)
        for a in range(n):
            copy(a, 0, sibling, me).wait_recv()
        for j, chip in enumerate(chips):
            for a in range(n):
                copy(a, 4 + j, (*chip, 1 - c), me).wait_recv()
        for cp in first + passed:
            cp.wait_send()
        for cp in mine:
            cp.wait()

    hbm = pl.BlockSpec(memory_space=pl.ANY)
    return pl.pallas_call(
        body, name="all_gather",
        out_shape=tuple(jax.ShapeDtypeStruct((N_DEV,) + s.shape, s.dtype) for s in shards),
        in_specs=[hbm] * n, out_specs=[hbm] * n,
        scratch_shapes=[pltpu.SemaphoreType.DMA((7 * n,)), pltpu.SemaphoreType.DMA((7 * n,)),
                        pltpu.SemaphoreType.DMA((n,))],
    )(*shards)


def _swap_with_sibling(parts):
    n = len(parts)

    def body(*refs):
        p_refs, out_refs = refs[:n], refs[n:2 * n]
        send_sems, recv_sems = refs[2 * n:]
        x, y, c = lax.axis_index("x"), lax.axis_index("y"), lax.axis_index("c")
        copies = []
        for a in range(n):
            for k in range(4):
                cp = pltpu.make_async_remote_copy(
                    src_ref=p_refs[a].at[2 * k + (1 - c)], dst_ref=out_refs[a].at[k],
                    send_sem=send_sems.at[4 * a + k], recv_sem=recv_sems.at[4 * a + k],
                    device_id=(x, y, 1 - c), device_id_type=MESH)
                cp.start()
                copies.append(cp)
        for cp in copies:
            cp.wait()

    hbm = pl.BlockSpec(memory_space=pl.ANY)
    return pl.pallas_call(
        body, name="rs_sibling",
        out_shape=tuple(jax.ShapeDtypeStruct((4,) + p.shape[1:], p.dtype) for p in parts),
        in_specs=[hbm] * n, out_specs=[hbm] * n,
        scratch_shapes=[pltpu.SemaphoreType.DMA((4 * n,)), pltpu.SemaphoreType.DMA((4 * n,))],
    )(*parts)


def _exchange_chips(parts):
    n = len(parts)

    def body(*refs):
        p_refs, out_refs = refs[:n], refs[n:2 * n]
        send_sems, recv_sems = refs[2 * n:]
        x, y, c = lax.axis_index("x"), lax.axis_index("y"), lax.axis_index("c")
        copies = []
        for a in range(n):
            for k, (tx, ty) in enumerate([(1 - x, y), (x, 1 - y), (1 - x, 1 - y)]):
                cp = pltpu.make_async_remote_copy(
                    src_ref=p_refs[a].at[2 * tx + ty], dst_ref=out_refs[a].at[k],
                    send_sem=send_sems.at[3 * a + k], recv_sem=recv_sems.at[3 * a + k],
                    device_id=(tx, ty, c), device_id_type=MESH)
                cp.start()
                copies.append(cp)
        for cp in copies:
            cp.wait()

    hbm = pl.BlockSpec(memory_space=pl.ANY)
    return pl.pallas_call(
        body, name="rs_chips",
        out_shape=tuple(jax.ShapeDtypeStruct((3,) + p.shape[1:], p.dtype) for p in parts),
        in_specs=[hbm] * n, out_specs=[hbm] * n,
        scratch_shapes=[pltpu.SemaphoreType.DMA((3 * n,)), pltpu.SemaphoreType.DMA((3 * n,))],
    )(*parts)


def _row_block(rows, cap=2048):
    best = 8
    for t in range(8, min(rows, cap) + 1, 8):
        if rows % t == 0:
            best = t
    return best


def _elementwise_rows(rows, cols):
    return _row_block(rows, cap=max(8, (1 << 17) // cols // 8 * 8))


def _chip_partial(parts, got, core):
    _, rows, cols = got.shape
    tr = _elementwise_rows(rows, cols)
    p4 = parts.reshape(4, 2, rows, cols)

    def body(c_ref, a_ref, b_ref, o_ref):
        o_ref[...] = (a_ref[...].astype(F32) + b_ref[...].astype(F32)).astype(o_ref.dtype)

    return pl.pallas_call(
        body, name="rs_chip_partial",
        out_shape=jax.ShapeDtypeStruct((4, rows, cols), parts.dtype),
        grid_spec=pltpu.PrefetchScalarGridSpec(
            num_scalar_prefetch=1, grid=(4, rows // tr),
            in_specs=[pl.BlockSpec((None, None, tr, cols), lambda k, r, c: (k, c[0], r, 0)),
                      pl.BlockSpec((None, tr, cols), lambda k, r, c: (k, r, 0))],
            out_specs=pl.BlockSpec((None, tr, cols), lambda k, r, c: (k, r, 0))),
        compiler_params=_cparams(),
    )(core, p4, got)


def _adamw_math(w, m, v, g):
    mm = ADAM_B1 * m + (1.0 - ADAM_B1) * g
    vv = ADAM_B2 * v + (1.0 - ADAM_B2) * jnp.square(g)
    m_hat = mm / (1.0 - ADAM_B1 ** ADAM_STEP)
    v_hat = vv / (1.0 - ADAM_B2 ** ADAM_STEP)
    return -ADAM_LR * (m_hat / (jnp.sqrt(v_hat) + ADAM_EPS) + ADAM_WD * w), mm, vv


def _final_sum_adamw(partial, got, chip, w, m, v):
    _, rows, cols = got.shape
    tr = _elementwise_rows(rows, cols)

    def body(c_ref, a_ref, b_ref, w_ref, m_ref, v_ref, g_ref, d_ref, nm_ref, nv_ref):
        g = ((a_ref[...].astype(F32) + b_ref[0].astype(F32)) + b_ref[1].astype(F32)) + b_ref[2].astype(F32)
        g_ref[...] = g
        d_ref[...], nm_ref[...], nv_ref[...] = _adamw_math(w_ref[...], m_ref[...], v_ref[...], g)

    flat = pl.BlockSpec((tr, cols), lambda r, c: (r, 0))
    return pl.pallas_call(
        body, name="rs_final_adamw",
        out_shape=(jax.ShapeDtypeStruct((rows, cols), F32),) * 4,
        grid_spec=pltpu.PrefetchScalarGridSpec(
            num_scalar_prefetch=1, grid=(rows // tr,),
            in_specs=[pl.BlockSpec((None, tr, cols), lambda r, c: (c[0], r, 0)),
                      pl.BlockSpec((3, tr, cols), lambda r, c: (0, r, 0)), flat, flat, flat],
            out_specs=[flat] * 4),
        compiler_params=_cparams(),
    )(chip, partial, got, w, m, v)


def _sum_devices(gathered):
    _, rows, lanes = gathered.shape
    tr = _row_block(rows)

    def body(g_ref, o_ref):
        acc = g_ref[0]
        for j in range(1, N_DEV):
            acc = acc + g_ref[j]
        o_ref[...] = acc

    return pl.pallas_call(
        body, name="small_grad_sum",
        out_shape=jax.ShapeDtypeStruct((rows, lanes), F32),
        grid=(rows // tr,),
        in_specs=[pl.BlockSpec((N_DEV, tr, lanes), lambda r: (0, r, 0))],
        out_specs=pl.BlockSpec((tr, lanes), lambda r: (r, 0)),
        compiler_params=_cparams(),
    )(gathered)


def _assemble(g, kind, n_in=0):
    _, L, A, B = g.shape
    if kind == 'cols':
        oshape = (A, N_DEV * B)
    elif kind == 'rows':
        oshape = (N_DEV * A, B)
    elif kind == 'w_in':
        oshape = (A, n_in + LANES)
    else:
        oshape = (A, N_DEV * LANES)

    def body(x_ref, o_ref):
        if kind == 'cols':
            for j in range(N_DEV):
                o_ref[:, j * B:(j + 1) * B] = x_ref[j]
        elif kind == 'rows':
            for j in range(N_DEV):
                o_ref[j * A:(j + 1) * A, :] = x_ref[j]
        elif kind == 'w_in':
            nat = jnp.concatenate([x_ref[j] for j in range(N_DEV)], axis=1)
            o_ref[:, 0:n_in] = nat[:, 0:n_in]
            o_ref[:, n_in:] = jnp.concatenate(
                [jnp.zeros((A, QK_NOPE), g.dtype), nat[:, n_in:],
                 jnp.zeros((A, LANES - QK_NOPE - QK_ROPE), g.dtype)], axis=1)
        else:
            o_ref[...] = jnp.zeros(oshape, g.dtype)
            for j in range(N_DEV):
                o_ref[:, j * LANES:j * LANES + B] = x_ref[j]

    return pl.pallas_call(
        body, name="assemble_" + kind,
        out_shape=jax.ShapeDtypeStruct((L,) + oshape, g.dtype),
        grid=(L,),
        in_specs=[pl.BlockSpec((N_DEV, None, A, B), lambda l: (0, l, 0, 0))],
        out_specs=pl.BlockSpec((None,) + oshape, lambda l: (l, 0, 0)),
        compiler_params=_cparams(),
    )(g)


def _wgrad(a, b, buf, layer, kind, n_in=0):
    K, M = a.shape
    N = b.shape[1]
    shard = buf.shape[2:]
    tk = ROW_TILE
    nk = K // tk
    if kind == 'cols':
        S = shard[1]
        tn = S
        for t in range(S, N + 1, S):
            if N % t == 0 and t % LANES == 0 and M * t * 4 <= MM_ACC_BYTES:
                tn = t
        oblock = pl.BlockSpec((tn // S, None) + shard, lambda n, k, l=layer: (n, l, 0, 0))
    elif kind == 'rows':
        tn = LANES
        for t in range(LANES, N + 1, LANES):
            if N % t == 0 and M * t * 4 <= MM_ACC_BYTES:
                tn = t
        oblock = pl.BlockSpec((N_DEV, None, shard[0], tn), lambda n, k, l=layer: (0, l, 0, n))
    else:
        tn = N
        oblock = pl.BlockSpec((N_DEV, None) + shard, lambda n, k, l=layer: (0, l, 0, 0))

    def body(a_ref, b_ref, buf_ref, o_ref, acc_ref):
        k = pl.program_id(1)

        @pl.when(k == 0)
        def _():
            acc_ref[...] = jnp.zeros_like(acc_ref)

        acc_ref[...] += _dot_tn(a_ref[...], b_ref[...])

        @pl.when(k == nk - 1)
        def _():
            acc = acc_ref[...]
            if kind == 'cols':
                S = shard[1]
                for j in range(tn // S):
                    o_ref[j] = acc[:, j * S:(j + 1) * S].astype(o_ref.dtype)
            elif kind == 'rows':
                S = shard[0]
                for j in range(N_DEV):
                    o_ref[j] = acc[j * S:(j + 1) * S, :].astype(o_ref.dtype)
            elif kind == 'w_in':
                S = shard[1]
                nat = jnp.concatenate([acc[:, 0:n_in], acc[:, n_in + QK_NOPE:n_in + QK_NOPE + QK_ROPE]], axis=1)
                for j in range(N_DEV):
                    o_ref[j] = nat[:, j * S:(j + 1) * S].astype(o_ref.dtype)
            else:
                S = shard[1]
                for j in range(N_DEV):
                    o_ref[j] = acc[:, j * LANES:j * LANES + S].astype(o_ref.dtype)

    return pl.pallas_call(
        body, name="wgrad_" + kind,
        out_shape=jax.ShapeDtypeStruct(buf.shape, buf.dtype),
        grid=(N // tn, nk),
        in_specs=[pl.BlockSpec((tk, M), lambda n, k: (k, 0)), pl.BlockSpec((tk, tn), lambda n, k: (k, n)),
                  pl.BlockSpec(memory_space=pl.ANY)],
        out_specs=oblock,
        scratch_shapes=[pltpu.VMEM((M, tn), F32)],
        input_output_aliases={2: 0},
        compiler_params=_cparams(),
    )(a, b, buf)


def _adamw(w, m, v, g):
    rows, cols = w.shape
    tr = _elementwise_rows(rows, cols) if rows % 8 == 0 else rows

    def body(w_ref, m_ref, v_ref, g_ref, d_ref, nm_ref, nv_ref):
        d_ref[...], nm_ref[...], nv_ref[...] = _adamw_math(w_ref[...], m_ref[...], v_ref[...], g_ref[...])

    spec = pl.BlockSpec((tr, cols), lambda r: (r, 0))
    return pl.pallas_call(
        body, name="adamw",
        out_shape=(jax.ShapeDtypeStruct((rows, cols), F32),) * 3,
        grid=(rows // tr,),
        in_specs=[spec] * 4, out_specs=[spec] * 3,
        compiler_params=_cparams(),
    )(w, m, v, g)


def _hidden_chunk(F):
    return F // 2 if (F // 2) % LANES == 0 else F


def _ffn_fwd(h, pre_g, post_g, wg, wu, wd):
    Tp, D = h.shape
    F = wg.shape[1]
    fc = _hidden_chunk(F)

    def body(h_ref, pg_ref, qg_ref, wg_ref, wu_ref, wd_ref, ho_ref, a_ref, b_ref, f_ref):
        hx = h_ref[...]
        u, _, _ = _rms_fwd(hx, pg_ref[...])
        ub = u.astype(BF16)
        f = jnp.zeros((ROW_TILE, D), F32)
        for c in range(F // fc):
            sl = slice(c * fc, (c + 1) * fc)
            a = _dot(ub, wg_ref[:, sl])
            b = _dot(ub, wu_ref[:, sl])
            a_ref[:, sl] = a.astype(BF16)
            b_ref[:, sl] = b.astype(BF16)
            act = (a * jax.nn.sigmoid(a) * b).astype(BF16)
            f = f + _dot(act, wd_ref[sl, :])
        f_ref[...] = f
        y, _, _ = _rms_fwd(f, qg_ref[...])
        ho_ref[...] = hx + 0.5 * y

    return pl.pallas_call(
        body, name="ffn_fwd",
        out_shape=(jax.ShapeDtypeStruct((Tp, D), F32), jax.ShapeDtypeStruct((Tp, F), BF16),
                   jax.ShapeDtypeStruct((Tp, F), BF16), jax.ShapeDtypeStruct((Tp, D), F32)),
        grid=(Tp // ROW_TILE,),
        in_specs=[_rows(D), _const((1, D)), _const((1, D)), _resident((D, F)), _resident((D, F)), _resident((F, D))],
        out_specs=[_rows(D), _rows(F), _rows(F), _rows(D)],
        compiler_params=_cparams(),
    )(h, pre_g, post_g, wg, wu, wd)


def _ffn_bwd_down(dh, a, b, f, post_g, wd):
    Tp, D = dh.shape
    F = a.shape[1]
    fc = _hidden_chunk(F)

    def body(dh_ref, a_ref, b_ref, f_ref, qg_ref, wd_ref, da_ref, db_ref, act_ref, df_ref, dqg_ref):
        qg = qg_ref[...]
        _, fn, frinv = _rms_fwd(f_ref[...], qg)
        df, dqg = _rms_bwd(fn, frinv, qg, 0.5 * dh_ref[...])
        dfb = df.astype(BF16)
        df_ref[...] = dfb
        for c in range(F // fc):
            sl = slice(c * fc, (c + 1) * fc)
            dact = _dot_nt(dfb, wd_ref[sl, :])
            av = a_ref[:, sl].astype(F32)
            bv = b_ref[:, sl].astype(F32)
            sig = jax.nn.sigmoid(av)
            silu = av * sig
            act_ref[:, sl] = (silu * bv).astype(BF16)
            da_ref[:, sl] = (dact * bv * (sig * (1.0 + av * (1.0 - sig)))).astype(BF16)
            db_ref[:, sl] = (dact * silu).astype(BF16)
        _accumulate(dqg_ref, dqg, pl.program_id(0) == 0)

    return pl.pallas_call(
        body, name="ffn_bwd_down",
        out_shape=(jax.ShapeDtypeStruct((Tp, F), BF16), jax.ShapeDtypeStruct((Tp, F), BF16),
                   jax.ShapeDtypeStruct((Tp, F), BF16), jax.ShapeDtypeStruct((Tp, D), BF16),
                   jax.ShapeDtypeStruct((1, D), F32)),
        grid=(Tp // ROW_TILE,),
        in_specs=[_rows(D), _rows(F), _rows(F), _rows(D), _const((1, D)), _resident((F, D))],
        out_specs=[_rows(F), _rows(F), _rows(F), _rows(D), _const((1, D))],
        compiler_params=_cparams(),
    )(dh, a, b, f, post_g, wd)


def _ffn_bwd_up(dh, h, da, db, pre_g, wg, wu):
    Tp, D = dh.shape
    F = da.shape[1]

    def body(dh_ref, h_ref, da_ref, db_ref, pg_ref, wg_ref, wu_ref, dhi_ref, u_ref, dpg_ref):
        pg = pg_ref[...]
        u, hn, hrinv = _rms_fwd(h_ref[...], pg)
        u_ref[...] = u.astype(BF16)
        du = _dot_nt(da_ref[...], wg_ref[...]) + _dot_nt(db_ref[...], wu_ref[...])
        dx, dpg = _rms_bwd(hn, hrinv, pg, du)
        dhi_ref[...] = dh_ref[...] + dx
        _accumulate(dpg_ref, dpg, pl.program_id(0) == 0)

    return pl.pallas_call(
        body, name="ffn_bwd_up",
        out_shape=(jax.ShapeDtypeStruct((Tp, D), F32), jax.ShapeDtypeStruct((Tp, D), BF16),
                   jax.ShapeDtypeStruct((1, D), F32)),
        grid=(Tp // ROW_TILE,),
        in_specs=[_rows(D), _rows(D), _rows(F), _rows(F), _const((1, D)), _resident((D, F)), _resident((D, F))],
        out_specs=[_rows(D), _rows(D), _const((1, D))],
        compiler_params=_cparams(),
    )(dh, h, da, db, pre_g, wg, wu)


def _mix_in_fwd(h, g, win, widths):
    Tp, D = h.shape
    ncol = win.shape[1]
    offs = [0]
    for wd_ in widths:
        offs.append(offs[-1] + wd_)

    def body(h_ref, g_ref, w_ref, *out_refs):
        y, _, _ = _rms_fwd(h_ref[...], g_ref[...])
        z = _dot(y.astype(BF16), w_ref[...])
        for o_ref, lo, wd_ in zip(out_refs, offs, widths):
            o_ref[...] = z[:, lo:lo + wd_]

    return pl.pallas_call(
        body, name="mix_in_fwd",
        out_shape=tuple(jax.ShapeDtypeStruct((Tp, wd_), F32) for wd_ in widths),
        grid=(Tp // ROW_TILE,),
        in_specs=[_rows(D), _const((1, D)), _resident((D, ncol))],
        out_specs=[_rows(wd_) for wd_ in widths],
        compiler_params=_cparams(),
    )(h, g, win)


def _mix_in_bwd(dres, h, dparts, g, win):
    Tp, D = h.shape
    ncol = win.shape[1]
    widths = [p.shape[1] for p in dparts]
    n = len(dparts)

    def body(*refs):
        dres_ref, h_ref = refs[0], refs[1]
        part_refs = refs[2:2 + n]
        g_ref, w_ref = refs[2 + n], refs[3 + n]
        dh_ref, dz_ref, u_ref, dg_ref = refs[4 + n:]
        dz = jnp.concatenate([r[...] for r in part_refs], axis=1).astype(BF16)
        dz_ref[...] = dz
        gg = g_ref[...]
        u, hn, rinv = _rms_fwd(h_ref[...], gg)
        u_ref[...] = u.astype(BF16)
        dx, dg = _rms_bwd(hn, rinv, gg, _dot_nt(dz, w_ref[...]))
        dh_ref[...] = dres_ref[...] + dx
        _accumulate(dg_ref, dg, pl.program_id(0) == 0)

    return pl.pallas_call(
        body, name="mix_in_bwd",
        out_shape=(jax.ShapeDtypeStruct((Tp, D), F32), jax.ShapeDtypeStruct((Tp, ncol), BF16),
                   jax.ShapeDtypeStruct((Tp, D), BF16), jax.ShapeDtypeStruct((1, D), F32)),
        grid=(Tp // ROW_TILE,),
        in_specs=[_rows(D), _rows(D)] + [_rows(wd_) for wd_ in widths] + [_const((1, D)), _resident((D, ncol))],
        out_specs=[_rows(D), _rows(ncol), _rows(D), _const((1, D))],
        compiler_params=_cparams(),
    )(dres, h, *dparts, g, win)


def _mix_out_fwd(h, y_lru, y_mla, lru_g, mla_g, post_g, wout):
    Tp, D = h.shape
    W = y_lru.shape[1]

    def body(h_ref, yl_ref, ym_ref, lg_ref, mg_ref, pg_ref, w_ref, ho_ref, y_ref):
        yl, _, _ = _rms_fwd(yl_ref[...], lg_ref[...])
        ym, _, _ = _rms_fwd(ym_ref[...], mg_ref[...])
        y = _dot(yl.astype(BF16), w_ref[0:W, :]) + _dot(ym.astype(BF16), w_ref[W:, :])
        y_ref[...] = y
        yn, _, _ = _rms_fwd(y, pg_ref[...])
        ho_ref[...] = h_ref[...] + yn

    return pl.pallas_call(
        body, name="mix_out_fwd",
        out_shape=(jax.ShapeDtypeStruct((Tp, D), F32), jax.ShapeDtypeStruct((Tp, D), F32)),
        grid=(Tp // ROW_TILE,),
        in_specs=[_rows(D), _rows(W), _rows(W), _const((1, W)), _const((1, W)), _const((1, D)), _resident(wout.shape)],
        out_specs=[_rows(D), _rows(D)],
        compiler_params=_cparams(),
    )(h, y_lru, y_mla, lru_g, mla_g, post_g, wout)


def _mix_out_bwd(dh, y, y_lru, y_mla, lru_g, mla_g, post_g, wout):
    Tp, D = dh.shape
    W = y_lru.shape[1]

    def body(dh_ref, y_ref, yl_ref, ym_ref, lg_ref, mg_ref, pg_ref, w_ref,
             dyl_ref, dym_ref, dy_ref, cat_ref, dlg_ref, dmg_ref, dpg_ref):
        first = pl.program_id(0) == 0
        pg, lg, mg = pg_ref[...], lg_ref[...], mg_ref[...]
        _, yn, yrinv = _rms_fwd(y_ref[...], pg)
        dy, dpg = _rms_bwd(yn, yrinv, pg, dh_ref[...])
        dyb = dy.astype(BF16)
        dy_ref[...] = dyb
        dcat = _dot_nt(dyb, w_ref[...])
        yl, yln, ylr = _rms_fwd(yl_ref[...], lg)
        ym, ymn, ymr = _rms_fwd(ym_ref[...], mg)
        cat_ref[:, 0:W] = yl.astype(BF16)
        cat_ref[:, W:] = ym.astype(BF16)
        dyl, dlg = _rms_bwd(yln, ylr, lg, dcat[:, 0:W])
        dym, dmg = _rms_bwd(ymn, ymr, mg, dcat[:, W:])
        dyl_ref[...] = dyl
        dym_ref[...] = dym
        _accumulate(dlg_ref, dlg, first)
        _accumulate(dmg_ref, dmg, first)
        _accumulate(dpg_ref, dpg, first)

    return pl.pallas_call(
        body, name="mix_out_bwd",
        out_shape=(jax.ShapeDtypeStruct((Tp, W), F32), jax.ShapeDtypeStruct((Tp, W), F32),
                   jax.ShapeDtypeStruct((Tp, D), BF16), jax.ShapeDtypeStruct((Tp, 2 * W), BF16),
                   jax.ShapeDtypeStruct((1, W), F32), jax.ShapeDtypeStruct((1, W), F32),
                   jax.ShapeDtypeStruct((1, D), F32)),
        grid=(Tp // ROW_TILE,),
        in_specs=[_rows(D), _rows(D), _rows(W), _rows(W), _const((1, W)), _const((1, W)), _const((1, D)),
                  _resident(wout.shape)],
        out_specs=[_rows(W), _rows(W), _rows(D), _rows(2 * W), _const((1, W)), _const((1, W)), _const((1, D))],
        compiler_params=_cparams(),
    )(dh, y, y_lru, y_mla, lru_g, mla_g, post_g, wout)


def _softplus_neg(lam):
    return jnp.maximum(-lam, 0.0) + jnp.log1p(jnp.exp(-jnp.abs(lam)))


def _neg_expm1(y):
    series = -y * (1.0 + 0.5 * y * (1.0 + (1.0 / 3.0) * y * (1.0 + 0.25 * y)))
    return jnp.where(y > -0.01, series, 1.0 - jnp.exp(y))


_GELU_K = math.sqrt(2.0 / math.pi)


def _gelu(x):
    return 0.5 * x * (1.0 + jnp.tanh(_GELU_K * (x + 0.044715 * (x * x * x))))


def _gelu_grad(x):
    t = jnp.tanh(_GELU_K * (x + 0.044715 * (x * x * x)))
    return 0.5 * (1.0 + t) + 0.5 * x * (1.0 - t * t) * (_GELU_K * (1.0 + 3.0 * 0.044715 * (x * x)))


def _lru_conv(xpad_ref, lo, n, cw, cb):
    xc = xpad_ref[pl.ds(8 + lo, n), :] * cw[3:4, :] + cb
    for k in range(CONV_WIDTH - 1):
        xc = xc + xpad_ref[pl.ds(8 + lo - (CONV_WIDTH - 1 - k), n), :] * cw[k:k + 1, :]
    return xc


def _lru_gates(xc, wa, wx, ba, bx, sp):
    xb = xc.astype(BF16)
    r = jax.nn.sigmoid(_dot(xb, wa) + ba)
    i = jax.nn.sigmoid(_dot(xb, wx) + bx)
    la = (-LRU_C * r) * sp
    a = jnp.exp(la)
    mult = jnp.sqrt(_neg_expm1(2.0 * la))
    return r, i, a, mult


def _scan_carries(last_h, last_p, reverse):
    row = lax.broadcasted_iota(jnp.int32, last_h.shape, 0)
    carry = jnp.zeros_like(last_h)
    for _ in range(SCAN_CHUNKS - 1):
        nxt = last_h + last_p * carry
        if reverse:
            carry = jnp.where(row < SCAN_CHUNKS - 1, pltpu.roll(nxt, SCAN_CHUNKS - 1, axis=0), 0.0)
        else:
            carry = jnp.where(row > 0, pltpu.roll(nxt, 1, axis=0), 0.0)
    return carry


def _lru_forward_scan(a_ref, h_ref, p_ref, a_off, h_off, rc):
    zero = jnp.zeros((SCAN_CHUNKS, LANES), F32)

    def step(i, carry):
        hh, pp = carry
        av = a_ref[pl.ds(a_off + i, SCAN_CHUNKS, stride=rc), :]
        hh = av * hh + h_ref[pl.ds(h_off + i, SCAN_CHUNKS, stride=rc), :]
        pp = av * pp
        h_ref[pl.ds(h_off + i, SCAN_CHUNKS, stride=rc), :] = hh
        p_ref[pl.ds(i, SCAN_CHUNKS, stride=rc), :] = pp
        return hh, pp

    last_h, last_p = lax.fori_loop(0, rc, step, (zero, zero + 1.0))
    carry = _scan_carries(last_h, last_p, reverse=False)
    for c in range(SCAN_CHUNKS):
        rows = pl.ds(h_off + c * rc, rc)
        h_ref[rows, :] = h_ref[rows, :] + p_ref[pl.ds(c * rc, rc), :] * carry[c:c + 1, :]


def _rglru_fwd(xr, gr, cw, cb, wa, wx, ba, bx, lam):
    Tp, W = xr.shape
    nb = W // LANES
    rc = Tp // SCAN_CHUNKS

    def body(xr_ref, gr_ref, cw_ref, cb_ref, wa_ref, wx_ref, ba_ref, bx_ref, lam_ref, y_ref,
             xpad, a_s, h_s, p_s):
        xpad[0:8, :] = jnp.zeros((8, LANES), F32)
        xpad[pl.ds(8, Tp), :] = xr_ref[...]
        cw_, cb_ = cw_ref[...], cb_ref[...]
        sp = _softplus_neg(lam_ref[...])
        for c in range(SCAN_CHUNKS):
            xc = _lru_conv(xpad, c * rc, rc, cw_, cb_)
            _, i, a, mult = _lru_gates(xc, wa_ref[...], wx_ref[...], ba_ref[...], bx_ref[...], sp)
            a_s[pl.ds(c * rc, rc), :] = a
            h_s[pl.ds(c * rc, rc), :] = mult * (i * xc)
        _lru_forward_scan(a_s, h_s, p_s, 0, 0, rc)
        for c in range(SCAN_CHUNKS):
            rows = pl.ds(c * rc, rc)
            y_ref[rows, :] = h_s[rows, :] * _gelu(gr_ref[rows, :])

    col = pl.BlockSpec((Tp, LANES), lambda j: (0, j))
    vec = pl.BlockSpec((1, LANES), lambda j: (0, j))
    mat = pl.BlockSpec((None, LANES, LANES), lambda j: (j, 0, 0))
    return pl.pallas_call(
        body, name="rglru_fwd",
        out_shape=jax.ShapeDtypeStruct((Tp, W), F32),
        grid=(nb,),
        in_specs=[col, col, pl.BlockSpec((CONV_WIDTH, LANES), lambda j: (0, j)), vec, mat, mat, vec, vec, vec],
        out_specs=col,
        scratch_shapes=[pltpu.VMEM((Tp + 8, LANES), F32), pltpu.VMEM((Tp, LANES), F32),
                        pltpu.VMEM((Tp, LANES), F32), pltpu.VMEM((Tp, LANES), F32)],
        compiler_params=_cparams(),
    )(xr, gr, cw, cb, wa, wx, ba, bx, lam)


def _rglru_bwd(dy, xr, gr, cw, cb, wa, wx, ba, bx, lam):
    Tp, W = xr.shape
    nb = W // LANES
    rc = Tp // SCAN_CHUNKS

    def body(dy_ref, xr_ref, gr_ref, cw_ref, cb_ref, wa_ref, wx_ref, ba_ref, bx_ref, lam_ref,
             dxr_ref, dgr_ref, dcw_ref, dcb_ref, dwa_ref, dwx_ref, dba_ref, dbx_ref, dlam_ref,
             xpad, a_s, h_s, p_s, xc_s, g_s, dxc_s):
        zeros8 = jnp.zeros((8, LANES), F32)
        xpad[0:8, :] = zeros8
        xpad[pl.ds(8, Tp), :] = xr_ref[...]
        a_s[pl.ds(Tp, 8), :] = zeros8
        h_s[0:8, :] = zeros8
        dxc_s[pl.ds(Tp, 8), :] = zeros8
        cw_, cb_ = cw_ref[...], cb_ref[...]
        lam_ = lam_ref[...]
        sp = _softplus_neg(lam_)
        gate_args = (wa_ref[...], wx_ref[...], ba_ref[...], bx_ref[...], sp)

        for c in range(SCAN_CHUNKS):
            rows = pl.ds(c * rc, rc)
            xc = _lru_conv(xpad, c * rc, rc, cw_, cb_)
            xc_s[rows, :] = xc
            _, i, a, mult = _lru_gates(xc, *gate_args)
            a_s[rows, :] = a
            h_s[pl.ds(8 + c * rc, rc), :] = mult * (i * xc)
        _lru_forward_scan(a_s, h_s, p_s, 0, 8, rc)

        for c in range(SCAN_CHUNKS):
            rows = pl.ds(c * rc, rc)
            dyv, grv = dy_ref[rows, :], gr_ref[rows, :]
            g_s[rows, :] = dyv * _gelu(grv)
            dgr_ref[rows, :] = dyv * h_s[pl.ds(8 + c * rc, rc), :] * _gelu_grad(grv)

        zero = jnp.zeros((SCAN_CHUNKS, LANES), F32)

        def rstep(k, carry):
            gg, qq = carry
            i = rc - 1 - k
            av = a_s[pl.ds(i + 1, SCAN_CHUNKS, stride=rc), :]
            gg = g_s[pl.ds(i, SCAN_CHUNKS, stride=rc), :] + av * gg
            qq = av * qq
            g_s[pl.ds(i, SCAN_CHUNKS, stride=rc), :] = gg
            p_s[pl.ds(i, SCAN_CHUNKS, stride=rc), :] = qq
            return gg, qq

        first_g, first_q = lax.fori_loop(0, rc, rstep, (zero, zero + 1.0))
        carry = _scan_carries(first_g, first_q, reverse=True)
        for c in range(SCAN_CHUNKS):
            rows = pl.ds(c * rc, rc)
            g_s[rows, :] = g_s[rows, :] + p_s[rows, :] * carry[c:c + 1, :]

        dwa = jnp.zeros((LANES, LANES), F32)
        dwx = jnp.zeros((LANES, LANES), F32)
        dba = jnp.zeros((1, LANES), F32)
        dbx = jnp.zeros((1, LANES), F32)
        dsp = jnp.zeros((1, LANES), F32)
        for c in range(SCAN_CHUNKS):
            rows = pl.ds(c * rc, rc)
            xc = xc_s[rows, :]
            r, i, a, mult = _lru_gates(xc, *gate_args)
            gg = g_s[rows, :]
            da = gg * h_s[pl.ds(7 + c * rc, rc), :]
            d_s = gg * mult
            dmult = gg * (i * xc)
            dla = da * a - dmult * (a * a) / mult
            dr = dla * (-LRU_C * sp)
            dsp = dsp + jnp.sum(dla * (-LRU_C * r), axis=0, keepdims=True)
            dpr = dr * r * (1.0 - r)
            dpi = (d_s * xc) * i * (1.0 - i)
            dprb, dpib, xcb = dpr.astype(BF16), dpi.astype(BF16), xc.astype(BF16)
            dxc_s[rows, :] = d_s * i + _dot_nt(dprb, wa_ref[...]) + _dot_nt(dpib, wx_ref[...])
            dwa = dwa + _dot_tn(xcb, dprb)
            dwx = dwx + _dot_tn(xcb, dpib)
            dba = dba + jnp.sum(dpr, axis=0, keepdims=True)
            dbx = dbx + jnp.sum(dpi, axis=0, keepdims=True)
        dwa_ref[...] = dwa
        dwx_ref[...] = dwx
        dba_ref[...] = dba
        dbx_ref[...] = dbx
        dlam_ref[...] = dsp * (-jax.nn.sigmoid(-lam_))

        dcw = [jnp.zeros((1, LANES), F32) for _ in range(CONV_WIDTH)]
        dcb = jnp.zeros((1, LANES), F32)
        for c in range(SCAN_CHUNKS):
            rows = pl.ds(c * rc, rc)
            dxc = dxc_s[rows, :]
            dcb = dcb + jnp.sum(dxc, axis=0, keepdims=True)
            dxr = dxc * cw_[3:4, :]
            for k in range(CONV_WIDTH):
                back = CONV_WIDTH - 1 - k
                dcw[k] = dcw[k] + jnp.sum(dxc * xpad[pl.ds(8 + c * rc - back, rc), :], axis=0, keepdims=True)
                if back:
                    dxr = dxr + dxc_s[pl.ds(c * rc + back, rc), :] * cw_[k:k + 1, :]
            dxr_ref[rows, :] = dxr
        dcw_ref[...] = jnp.concatenate(dcw, axis=0)
        dcb_ref[...] = dcb

    col = pl.BlockSpec((Tp, LANES), lambda j: (0, j))
    vec = pl.BlockSpec((1, LANES), lambda j: (0, j))
    mat = pl.BlockSpec((None, LANES, LANES), lambda j: (j, 0, 0))
    cwspec = pl.BlockSpec((CONV_WIDTH, LANES), lambda j: (0, j))
    return pl.pallas_call(
        body, name="rglru_bwd",
        out_shape=(jax.ShapeDtypeStruct((Tp, W), F32), jax.ShapeDtypeStruct((Tp, W), F32),
                   jax.ShapeDtypeStruct((CONV_WIDTH, W), F32), jax.ShapeDtypeStruct((1, W), F32),
                   jax.ShapeDtypeStruct((nb, LANES, LANES), F32), jax.ShapeDtypeStruct((nb, LANES, LANES), F32),
                   jax.ShapeDtypeStruct((1, W), F32), jax.ShapeDtypeStruct((1, W), F32),
                   jax.ShapeDtypeStruct((1, W), F32)),
        grid=(nb,),
        in_specs=[col, col, col, cwspec, vec, mat, mat, vec, vec, vec],
        out_specs=[col, col, cwspec, vec, mat, mat, vec, vec, vec],
        scratch_shapes=[pltpu.VMEM((Tp + 8, LANES), F32), pltpu.VMEM((Tp + 8, LANES), F32),
                        pltpu.VMEM((Tp + 8, LANES), F32), pltpu.VMEM((Tp, LANES), F32),
                        pltpu.VMEM((Tp, LANES), F32), pltpu.VMEM((Tp, LANES), F32),
                        pltpu.VMEM((Tp + 8, LANES), F32)],
        compiler_params=_cparams(),
    )(dy, xr, gr, cw, cb, wa, wx, ba, bx, lam)


def _rope(x, cosm, sin_hi, sin_lo):
    return x * cosm + pltpu.roll(x, QK_ROPE // 2, axis=1) * sin_hi + pltpu.roll(x, LANES - QK_ROPE // 2, axis=1) * sin_lo


def _rope_inverse(x, cosm, sin_hi, sin_lo):
    return x * cosm - pltpu.roll(x, QK_ROPE // 2, axis=1) * sin_hi - pltpu.roll(x, LANES - QK_ROPE // 2, axis=1) * sin_lo


def _mla_proj_fwd(cq, ckv, kr, qg, kg, wuq, wkv, cosm, sin_hi, sin_lo):
    Tp, QL = cq.shape
    KL = ckv.shape[1]
    H = MLA_HEADS
    half = LANES // 2

    def body(cq_ref, ckv_ref, kr_ref, qg_ref, kg_ref, wuq_ref, wkv_ref, c_ref, s1_ref, s2_ref,
             q_ref, k_ref, v_ref):
        tabs = (c_ref[...], s1_ref[...], s2_ref[...])
        lane = lax.broadcasted_iota(jnp.int32, (ROW_TILE, LANES), 1)
        low = lane < half
        cqn, _, _ = _rms_fwd(cq_ref[...], qg_ref[...])
        q = _dot(cqn.astype(BF16), wuq_ref[...])
        ckvn, _, _ = _rms_fwd(ckv_ref[...], kg_ref[...])
        kv = _dot(ckvn.astype(BF16), wkv_ref[...])
        krr = _rope(kr_ref[...], *tabs)
        for hd in range(H):
            sl = slice(hd * LANES, (hd + 1) * LANES)
            q_ref[hd] = _rope(q[:, sl], *tabs).astype(BF16)
            k_ref[hd] = jnp.where(low, kv[:, sl], krr).astype(BF16)
        for p in range(H // 2):
            even = kv[:, 2 * p * LANES:(2 * p + 1) * LANES]
            odd = kv[:, (2 * p + 1) * LANES:(2 * p + 2) * LANES]
            v_ref[:, p * LANES:(p + 1) * LANES] = jnp.where(low, pltpu.roll(even, half, axis=1), odd).astype(BF16)

    heads = pl.BlockSpec((H, ROW_TILE, LANES), lambda i: (0, i, 0))
    return pl.pallas_call(
        body, name="mla_proj_fwd",
        out_shape=(jax.ShapeDtypeStruct((H, Tp, LANES), BF16), jax.ShapeDtypeStruct((H, Tp, LANES), BF16),
                   jax.ShapeDtypeStruct((Tp, H * half), BF16)),
        grid=(Tp // ROW_TILE,),
        in_specs=[_rows(QL), _rows(KL), _rows(LANES), _const((1, QL)), _const((1, KL)),
                  _resident(wuq.shape), _resident(wkv.shape), _rows(LANES), _rows(LANES), _rows(LANES)],
        out_specs=[heads, heads, _rows(H * half)],
        compiler_params=_cparams(),
    )(cq, ckv, kr, qg, kg, wuq, wkv, cosm, sin_hi, sin_lo)


def _mla_proj_bwd(dq, dk, dv, cq, ckv, qg, kg, wuq, wkv, cosm, sin_hi, sin_lo):
    H, Tp, _ = dq.shape
    QL, KL = cq.shape[1], ckv.shape[1]
    half = LANES // 2

    def body(dq_ref, dk_ref, dv_ref, cq_ref, ckv_ref, qg_ref, kg_ref, wuq_ref, wkv_ref,
             c_ref, s1_ref, s2_ref,
             dcq_ref, dckv_ref, dkr_ref, dqa_ref, dkva_ref, cqn_ref, ckvn_ref, dqg_ref, dkg_ref):
        first = pl.program_id(0) == 0
        tabs = (c_ref[...], s1_ref[...], s2_ref[...])
        lane = lax.broadcasted_iota(jnp.int32, (ROW_TILE, LANES), 1)
        low = lane < half
        rope_lanes = jnp.logical_and(lane >= QK_NOPE, lane < QK_NOPE + QK_ROPE)
        dkr = jnp.zeros((ROW_TILE, LANES), F32)
        for hd in range(H):
            sl = slice(hd * LANES, (hd + 1) * LANES)
            dqa_ref[:, sl] = _rope_inverse(dq_ref[hd], *tabs).astype(BF16)
            dkh = dk_ref[hd]
            dvp = dv_ref[:, (hd // 2) * LANES:(hd // 2 + 1) * LANES]
            dvh = pltpu.roll(dvp, half, axis=1) if hd % 2 == 0 else dvp
            dkva_ref[:, sl] = jnp.where(low, dkh, dvh).astype(BF16)
            dkr = dkr + jnp.where(rope_lanes, dkh, 0.0)
        dkr_ref[...] = _rope_inverse(dkr, *tabs)
        qg, kg = qg_ref[...], kg_ref[...]
        cqs, cqn, cqr = _rms_fwd(cq_ref[...], qg)
        cqn_ref[...] = cqs.astype(BF16)
        dcq, dqg = _rms_bwd(cqn, cqr, qg, _dot_nt(dqa_ref[...], wuq_ref[...]))
        dcq_ref[...] = dcq
        cks, ckn, ckr = _rms_fwd(ckv_ref[...], kg)
        ckvn_ref[...] = cks.astype(BF16)
        dckv, dkg = _rms_bwd(ckn, ckr, kg, _dot_nt(dkva_ref[...], wkv_ref[...]))
        dckv_ref[...] = dckv
        _accumulate(dqg_ref, dqg, first)
        _accumulate(dkg_ref, dkg, first)

    heads = pl.BlockSpec((H, ROW_TILE, LANES), lambda i: (0, i, 0))
    return pl.pallas_call(
        body, name="mla_proj_bwd",
        out_shape=(jax.ShapeDtypeStruct((Tp, QL), F32), jax.ShapeDtypeStruct((Tp, KL), F32),
                   jax.ShapeDtypeStruct((Tp, LANES), F32), jax.ShapeDtypeStruct((Tp, H * LANES), BF16),
                   jax.ShapeDtypeStruct((Tp, H * LANES), BF16),
                   jax.ShapeDtypeStruct((Tp, QL), BF16), jax.ShapeDtypeStruct((Tp, KL), BF16),
                   jax.ShapeDtypeStruct((1, QL), F32), jax.ShapeDtypeStruct((1, KL), F32)),
        grid=(Tp // ROW_TILE,),
        in_specs=[heads, heads, _rows(H * half), _rows(QL), _rows(KL), _const((1, QL)), _const((1, KL)),
                  _resident(wuq.shape), _resident(wkv.shape), _rows(LANES), _rows(LANES), _rows(LANES)],
        out_specs=[_rows(QL), _rows(KL), _rows(LANES), _rows(H * LANES), _rows(H * LANES),
                   _rows(QL), _rows(KL), _const((1, QL)), _const((1, KL))],
        compiler_params=_cparams(),
    )(dq, dk, dv, cq, ckv, qg, kg, wuq, wkv, cosm, sin_hi, sin_lo)


_NEG = -1e30
_ATT_SCALE = (QK_NOPE + QK_ROPE) ** -0.5
_ATT_LOG2 = _ATT_SCALE * math.log2(math.e)


def _causal(s, q0, k0, transposed):
    r = lax.broadcasted_iota(jnp.int32, s.shape, 0)
    c = lax.broadcasted_iota(jnp.int32, s.shape, 1)
    keep = (k0 + r <= q0 + c) if transposed else (k0 + c <= q0 + r)
    return jnp.where(keep, s, _NEG)


def _attn_fwd(q, k, v):
    H, Tp, _ = q.shape
    t = ROW_TILE
    half = LANES // 2

    def body(q_ref, k_ref, v_ref, y_ref, lse_ref):
        i = pl.program_id(1)
        qs = [q_ref[0], q_ref[1]]

        def step(j, carry, masked):
            vv = v_ref[pl.ds(j * t, t), :]
            out = []
            for hh in range(2):
                m, l, acc = carry[hh]
                kv = k_ref[hh, pl.ds(j * t, t), :]
                s = _dot_nt(qs[hh], kv) * _ATT_LOG2
                if masked:
                    s = _causal(s, i * t, j * t, False)
                m_new = jnp.maximum(m, jnp.max(s, axis=-1, keepdims=True))
                alpha = jnp.exp2(m - m_new)
                p = jnp.exp2(s - m_new)
                l = alpha * l + jnp.sum(p, axis=-1, keepdims=True)
                acc = alpha * acc + _dot(p.astype(BF16), vv)
                out.append((m_new, l, acc))
            return tuple(out)

        one = (jnp.full((t, 1), _NEG, F32), jnp.zeros((t, 1), F32), jnp.zeros((t, LANES), F32))
        carry = lax.fori_loop(0, i, functools.partial(step, masked=False), (one, one))
        carry = step(i, carry, True)
        outs = []
        for hh in range(2):
            m, l, acc = carry[hh]
            outs.append(acc / l)
            lse_ref[hh] = m + jnp.log2(l)
        lane = lax.broadcasted_iota(jnp.int32, (t, LANES), 1)
        y_ref[...] = jnp.where(lane < half, outs[0], outs[1])

    return pl.pallas_call(
        body, name="attn_fwd",
        out_shape=(jax.ShapeDtypeStruct((Tp, H * half), F32), jax.ShapeDtypeStruct((H, Tp, 1), F32)),
        grid=(H // 2, Tp // t),
        in_specs=[pl.BlockSpec((2, t, LANES), lambda p, i: (p, i, 0)),
                  pl.BlockSpec((2, Tp, LANES), lambda p, i: (p, 0, 0)),
                  pl.BlockSpec((Tp, LANES), lambda p, i: (0, p))],
        out_specs=[pl.BlockSpec((t, LANES), lambda p, i: (i, p)),
                   pl.BlockSpec((2, t, 1), lambda p, i: (p, i, 0))],
        compiler_params=_cparams(),
    )(q, k, v)


def _attn_bwd_q(q, k, v, dy, y, lse):
    H, Tp, _ = q.shape
    t = ROW_TILE
    half = LANES // 2

    def body(q_ref, k_ref, v_ref, dy_ref, y_ref, lse_ref, dq_ref, delta_ref):
        i = pl.program_id(1)
        lane = lax.broadcasted_iota(jnp.int32, (t, LANES), 1)
        qs, dobs, deltas, lses = [], [], [], []
        for hh in range(2):
            mine = (lane < half) if hh == 0 else (lane >= half)
            do = jnp.where(mine, dy_ref[...], 0.0)
            delta = jnp.sum(do * y_ref[...], axis=-1, keepdims=True)
            delta_ref[hh] = delta
            qs.append(q_ref[hh])
            dobs.append(do.astype(BF16))
            deltas.append(delta)
            lses.append(lse_ref[hh])

        def step(j, dqs, masked):
            vv = v_ref[pl.ds(j * t, t), :]
            out = []
            for hh in range(2):
                kv = k_ref[hh, pl.ds(j * t, t), :]
                s = _dot_nt(qs[hh], kv) * _ATT_LOG2
                if masked:
                    s = _causal(s, i * t, j * t, False)
                p = jnp.exp2(s - lses[hh])
                dp = _dot_nt(dobs[hh], vv)
                ds = p * (dp - deltas[hh])
                out.append(dqs[hh] + _dot(ds.astype(BF16), kv))
            return tuple(out)

        zero = jnp.zeros((t, LANES), F32)
        dqs = lax.fori_loop(0, i, functools.partial(step, masked=False), (zero, zero))
        dqs = step(i, dqs, True)
        for hh in range(2):
            dq_ref[hh] = dqs[hh] * _ATT_SCALE

    return pl.pallas_call(
        body, name="attn_bwd_q",
        out_shape=(jax.ShapeDtypeStruct((H, Tp, LANES), F32), jax.ShapeDtypeStruct((H, Tp, 1), F32)),
        grid=(H // 2, Tp // t),
        in_specs=[pl.BlockSpec((2, t, LANES), lambda p, i: (p, i, 0)),
                  pl.BlockSpec((2, Tp, LANES), lambda p, i: (p, 0, 0)),
                  pl.BlockSpec((Tp, LANES), lambda p, i: (0, p)),
                  pl.BlockSpec((t, LANES), lambda p, i: (i, p)),
                  pl.BlockSpec((t, LANES), lambda p, i: (i, p)),
                  pl.BlockSpec((2, t, 1), lambda p, i: (p, i, 0))],
        out_specs=[pl.BlockSpec((2, t, LANES), lambda p, i: (p, i, 0)),
                   pl.BlockSpec((2, t, 1), lambda p, i: (p, i, 0))],
        compiler_params=_cparams(),
    )(q, k, v, dy, y, lse)


def _attn_bwd_kv(q, k, v, dy, lse_row, delta_row):
    H, Tp, _ = q.shape
    t = ROW_TILE
    half = LANES // 2
    nq = Tp // t

    def body(q_ref, k_ref, v_ref, dy_ref, lse_ref, delta_ref, dk_ref, dv_ref):
        j = pl.program_id(1)
        lane = lax.broadcasted_iota(jnp.int32, (t, LANES), 1)
        vv = v_ref[...]
        ks = [k_ref[0], k_ref[1]]

        def step(i, carry, masked):
            cols = pl.ds(pl.multiple_of(i * t, LANES), t)
            dyv = dy_ref[pl.ds(i * t, t), :]
            out = []
            for hh in range(2):
                dk, dv = carry[hh]
                mine = (lane < half) if hh == 0 else (lane >= half)
                qv = q_ref[hh, pl.ds(i * t, t), :]
                dob = jnp.where(mine, dyv, 0.0).astype(BF16)
                st = _dot_nt(ks[hh], qv) * _ATT_LOG2
                if masked:
                    st = _causal(st, i * t, j * t, True)
                pt = jnp.exp2(st - lse_ref[hh, :, cols])
                dv = dv + _dot(pt.astype(BF16), dob)
                dpt = _dot_nt(vv, dob)
                dst = pt * (dpt - delta_ref[hh, :, cols])
                dk = dk + _dot(dst.astype(BF16), qv)
                out.append((dk, dv))
            return tuple(out)

        zero = jnp.zeros((t, LANES), F32)
        carry = step(j, ((zero, zero), (zero, zero)), True)
        carry = lax.fori_loop(j + 1, nq, functools.partial(step, masked=False), carry)
        for hh in range(2):
            dk_ref[hh] = carry[hh][0] * _ATT_SCALE
        dv_ref[...] = carry[0][1] + carry[1][1]

    return pl.pallas_call(
        body, name="attn_bwd_kv",
        out_shape=(jax.ShapeDtypeStruct((H, Tp, LANES), F32), jax.ShapeDtypeStruct((Tp, H * half), F32)),
        grid=(H // 2, nq),
        in_specs=[pl.BlockSpec((2, Tp, LANES), lambda p, j: (p, 0, 0)),
                  pl.BlockSpec((2, t, LANES), lambda p, j: (p, j, 0)),
                  pl.BlockSpec((t, LANES), lambda p, j: (j, p)),
                  pl.BlockSpec((Tp, LANES), lambda p, j: (0, p)),
                  pl.BlockSpec((2, 1, Tp), lambda p, j: (p, 0, 0)),
                  pl.BlockSpec((2, 1, Tp), lambda p, j: (p, 0, 0))],
        out_specs=[pl.BlockSpec((2, t, LANES), lambda p, j: (p, j, 0)),
                   pl.BlockSpec((t, LANES), lambda p, j: (j, p))],
        compiler_params=_cparams(),
    )(q, k, v, dy, lse_row, delta_row)


def _loss_head(h, target, lo, hi):
    Tp, D = h.shape

    def body(h_ref, t_ref, dh_ref, loss_ref):
        i = pl.program_id(0)
        row = i * ROW_TILE + lax.broadcasted_iota(jnp.int32, (ROW_TILE, 1), 0)
        live = jnp.logical_and(row >= lo, row < hi)
        err = jnp.where(live, h_ref[...] - t_ref[...], 0.0)
        dh_ref[...] = err / D
        part = 0.5 * jnp.sum(jnp.mean(err * err, axis=-1, keepdims=True), axis=0, keepdims=True)
        _accumulate(loss_ref, jnp.broadcast_to(part, (8, LANES)), i == 0)

    return pl.pallas_call(
        body, name="loss_head",
        out_shape=(jax.ShapeDtypeStruct((Tp, D), F32), jax.ShapeDtypeStruct((8, LANES), F32)),
        grid=(Tp // ROW_TILE,),
        in_specs=[_rows(D), _rows(D)],
        out_specs=[_rows(D), _const((8, LANES))],
        compiler_params=_cparams(),
    )(h, target)


def _pack(arrays, row_multiple=8):
    flat = jnp.concatenate([a.reshape(-1).astype(F32) for a in arrays])
    quantum = LANES * row_multiple
    total = -(-flat.shape[0] // quantum) * quantum
    flat = jnp.pad(flat, (0, total - flat.shape[0]))
    return flat.reshape(-1, LANES)


def _unpack(buf, shapes):
    flat = buf.reshape(-1)
    out, off = [], 0
    for shp in shapes:
        n = math.prod(shp)
        out.append(flat[off:off + n].reshape(tuple(shp)))
        off += n
    return out


def _block_diag_blocks(w):
    nh, d, _ = w.shape
    per = LANES // d
    eye = jnp.eye(per, dtype=w.dtype)
    g = w.reshape(nh // per, per, d, d)
    return jnp.einsum('bpij,pq->bpiqj', g, eye).reshape(nh // per, LANES, LANES)


def _block_diag_extract(blocks, d):
    nb = blocks.shape[0]
    per = LANES // d
    return jnp.stack([blocks[b, p * d:(p + 1) * d, p * d:(p + 1) * d] for b in range(nb) for p in range(per)])


def _step(a):
    x = a['x'][0]
    target = a['loss_target'][0]
    seq, D = x.shape
    n_meta = a['meta_tokens'].shape[0]
    T = seq + n_meta
    Tp = -(-T // ROW_TILE) * ROW_TILE
    L = a['ffn1_pre_g'].shape[0]
    W = a['lru_conv_b'].shape[1]
    QL = a['mla_q_norm_g'].shape[1]
    KL = a['mla_kv_norm_g'].shape[1]
    H = MLA_HEADS
    LD = a['lru_w_a'].shape[2]
    n_in = 2 * W + QL + KL
    assert W % LANES == 0 and QL % LANES == 0 and KL % LANES == 0 and LANES % LD == 0
    assert a['mla_w_ukv'].shape[2] == LANES and a['mla_w_uq'].shape[2] == QK_NOPE + QK_ROPE

    ax, ay, ac = (lax.axis_index(n) for n in MESH_AXES)
    me = 4 * ax + 2 * ay + ac
    core = ac.reshape(1).astype(jnp.int32)
    chip = (2 * ax + ay).reshape(1).astype(jnp.int32)

    gathered = dict(zip(BIG + SMALL_SHARDED,
                        _all_gather([a[n].astype(BF16) for n in BIG] + [a[n] for n in SMALL_SHARDED])))
    kinds = {'ffn1_w_gate': 'cols', 'ffn1_w_up': 'cols', 'ffn1_w_down': 'rows', 'w_in': 'w_in', 'mla_w_uq': 'heads',
             'mla_w_ukv': 'cols', 'w_out': 'rows', 'ffn2_w_gate': 'cols', 'ffn2_w_up': 'cols', 'ffn2_w_down': 'rows'}
    big = {n: _assemble(gathered[n], kinds[n], n_in) for n in BIG}
    gcw, gmeta = gathered['lru_conv_w'], gathered['meta_tokens']
    conv_w = jnp.moveaxis(gcw, 0, 2).reshape(L, CONV_WIDTH, W)
    meta = jnp.moveaxis(gmeta, 0, 1).reshape(n_meta, D)
    wa_blk = jax.vmap(_block_diag_blocks)(a['lru_w_a']).astype(BF16)
    wx_blk = jax.vmap(_block_diag_blocks)(a['lru_w_x']).astype(BF16)
    widths = [W, W, QL, KL, LANES]

    QH = QK_NOPE + QK_ROPE
    pos = jnp.arange(Tp, dtype=F32)
    inv_freq = 1.0 / (ROPE_THETA ** (jnp.arange(0, QK_ROPE, 2, dtype=F32) / QK_ROPE))
    ang = pos[:, None] * inv_freq[None, :]
    cos, sin = jnp.cos(ang), jnp.sin(ang)
    hr = QK_ROPE // 2
    z = lambda n: jnp.zeros((Tp, n), F32)
    cosm = jnp.concatenate([jnp.ones((Tp, QK_NOPE), F32), cos, cos, z(LANES - QH)], axis=1)
    sin_hi = jnp.concatenate([z(QK_NOPE + hr), sin, z(LANES - QH)], axis=1)
    sin_lo = jnp.concatenate([z(QK_NOPE), -sin, z(LANES - QK_NOPE - hr)], axis=1)
    tabs = (cosm, sin_hi, sin_lo)

    vec = lambda name, l: a[name][l][None, :]

    h = jnp.concatenate([meta, x, jnp.zeros((Tp - T, D), F32)], axis=0)
    tpad = jnp.concatenate([jnp.zeros((n_meta, D), F32), target, jnp.zeros((Tp - T, D), F32)], axis=0)
    saved = []
    for l in range(L):
        s = {'h0': h}
        h, s['a1'], s['b1'], s['f1'] = _ffn_fwd(h, vec('ffn1_pre_g', l), vec('ffn1_post_g', l),
                                                big['ffn1_w_gate'][l], big['ffn1_w_up'][l], big['ffn1_w_down'][l])
        s['h1'] = h
        xr, gr, cq, ckv, kr = _mix_in_fwd(h, vec('mix_pre_g', l), big['w_in'][l], widths)
        s.update(xr=xr, gr=gr, cq=cq, ckv=ckv)
        y_lru = _rglru_fwd(xr, gr, conv_w[l], vec('lru_conv_b', l), wa_blk[l], wx_blk[l], vec('lru_b_a', l),
                           vec('lru_b_x', l), vec('lru_lambda', l))
        q, k, v = _mla_proj_fwd(cq, ckv, kr, vec('mla_q_norm_g', l), vec('mla_kv_norm_g', l),
                                big['mla_w_uq'][l], big['mla_w_ukv'][l], *tabs)
        y_mla, lse = _attn_fwd(q, k, v)
        s.update(q=q, k=k, v=v, lse=lse, y_lru=y_lru, y_mla=y_mla)
        h, s['y'] = _mix_out_fwd(h, y_lru, y_mla, vec('lru_out_g', l), vec('mla_out_g', l), vec('mix_post_g', l),
                                 big['w_out'][l])
        s['h2'] = h
        h, s['a2'], s['b2'], s['f2'] = _ffn_fwd(h, vec('ffn2_pre_g', l), vec('ffn2_post_g', l),
                                                big['ffn2_w_gate'][l], big['ffn2_w_up'][l], big['ffn2_w_down'][l])
        saved.append(s)

    dh, loss_tile = _loss_head(h, tpad, n_meta, T)
    loss = lax.psum(loss_tile[0, 0], MESH_AXES)

    gw = {n: [None] * L for n in REPLICATED + ['lru_conv_w']}
    gbuf = {n: jnp.zeros((N_DEV,) + a[n].shape, BF16) for n in BIG}

    def ffn_bwd(f, l, dh, h_in, a_, b_, f_):
        da, db, act, dfb, dpost = _ffn_bwd_down(dh, a_, b_, f_, vec(f + '_post_g', l), big[f + '_w_down'][l])
        dh_in, u, dpre = _ffn_bwd_up(dh, h_in, da, db, vec(f + '_pre_g', l), big[f + '_w_gate'][l], big[f + '_w_up'][l])
        gbuf[f + '_w_gate'] = _wgrad(u, da, gbuf[f + '_w_gate'], l, 'cols')
        gbuf[f + '_w_up'] = _wgrad(u, db, gbuf[f + '_w_up'], l, 'cols')
        gbuf[f + '_w_down'] = _wgrad(act, dfb, gbuf[f + '_w_down'], l, 'rows')
        gw[f + '_pre_g'][l] = dpre[0]
        gw[f + '_post_g'][l] = dpost[0]
        return dh_in

    for l in reversed(range(L)):
        s = saved[l]
        dh = ffn_bwd('ffn2', l, dh, s['h2'], s['a2'], s['b2'], s['f2'])
        dyl, dym, dyb, cat, dlg, dmg, dpg = _mix_out_bwd(
            dh, s['y'], s['y_lru'], s['y_mla'], vec('lru_out_g', l), vec('mla_out_g', l), vec('mix_post_g', l),
            big['w_out'][l])
        gbuf['w_out'] = _wgrad(cat, dyb, gbuf['w_out'], l, 'rows')
        gw['lru_out_g'][l], gw['mla_out_g'][l], gw['mix_post_g'][l] = dlg[0], dmg[0], dpg[0]
        dxr, dgr, dcw, dcb, dwa, dwx, dba, dbx, dlam = _rglru_bwd(
            dyl, s['xr'], s['gr'], conv_w[l], vec('lru_conv_b', l), wa_blk[l], wx_blk[l],
            vec('lru_b_a', l), vec('lru_b_x', l), vec('lru_lambda', l))
        gw['lru_conv_w'][l], gw['lru_conv_b'][l] = dcw, dcb[0]
        gw['lru_w_a'][l], gw['lru_w_x'][l] = _block_diag_extract(dwa, LD), _block_diag_extract(dwx, LD)
        gw['lru_b_a'][l], gw['lru_b_x'][l], gw['lru_lambda'][l] = dba[0], dbx[0], dlam[0]
        dq, delta = _attn_bwd_q(s['q'], s['k'], s['v'], dym, s['y_mla'], s['lse'])
        dk, dv = _attn_bwd_kv(s['q'], s['k'], s['v'], dym, s['lse'].reshape(H, 1, Tp), delta.reshape(H, 1, Tp))
        dcq, dckv, dkr, dqa, dkva, cqn, ckvn, dqg, dkg = _mla_proj_bwd(
            dq, dk, dv, s['cq'], s['ckv'], vec('mla_q_norm_g', l), vec('mla_kv_norm_g', l),
            big['mla_w_uq'][l], big['mla_w_ukv'][l], *tabs)
        gw['mla_q_norm_g'][l], gw['mla_kv_norm_g'][l] = dqg[0], dkg[0]
        gbuf['mla_w_uq'] = _wgrad(cqn, dqa, gbuf['mla_w_uq'], l, 'heads')
        gbuf['mla_w_ukv'] = _wgrad(ckvn, dkva, gbuf['mla_w_ukv'], l, 'cols')
        dh, dz, hn, dmg_ = _mix_in_bwd(dh, s['h1'], [dxr, dgr, dcq, dckv, dkr], vec('mix_pre_g', l), big['w_in'][l])
        gw['mix_pre_g'][l] = dmg_[0]
        gbuf['w_in'] = _wgrad(hn, dz, gbuf['w_in'], l, 'w_in', n_in)
        dh = ffn_bwd('ffn1', l, dh, s['h0'], s['a1'], s['b1'], s['f1'])

    grad_x = dh[n_meta:T][None]
    gsmall = {n: jnp.stack(v) for n, v in gw.items()}
    gsmall['meta_tokens'] = dh[:n_meta]

    grads, delta, new_m, new_v = {}, {}, {}, {}
    parts = [gbuf[n] for n in BIG]
    got = _swap_with_sibling(parts)
    as3 = lambda t: t.reshape(t.shape[0], -1, t.shape[-1])
    chip_partial = [_chip_partial(as3(p), as3(g), core) for p, g in zip(parts, got)]
    got2 = _exchange_chips(chip_partial)
    for n, cp, g2 in zip(BIG, chip_partial, got2):
        shp = a[n].shape
        two_d = lambda t: t.reshape(-1, shp[-1])
        outs = _final_sum_adamw(cp, g2, chip, two_d(a[n]), two_d(a['m_' + n]), two_d(a['v_' + n]))
        grads[n], delta[n], new_m[n], new_v[n] = (o.reshape(shp) for o in outs)

    small_names = REPLICATED + SMALL_SHARDED
    small_full_shapes = [gsmall[n].shape for n in small_names]
    gsm = _sum_devices(_all_gather([_pack([gsmall[n] for n in small_names])])[0])
    for n, g in zip(small_names, _unpack(gsm, small_full_shapes)):
        if n == 'lru_conv_w':
            g = lax.dynamic_index_in_dim(jnp.moveaxis(g.reshape(L, CONV_WIDTH, N_DEV, W // N_DEV), 2, 0), me, 0, False)
        elif n == 'meta_tokens':
            g = lax.dynamic_index_in_dim(jnp.moveaxis(g.reshape(n_meta, N_DEV, D // N_DEV), 1, 0), me, 0, False)
        grads[n] = g
    for group in (REPLICATED, SMALL_SHARDED):
        shapes = [a[n].shape for n in group]
        packed = [_pack([a[p + n] for n in group]) for p in ('', 'm_', 'v_')]
        packed.append(_pack([grads[n] for n in group]))
        outs = [_unpack(o, shapes) for o in _adamw(*packed)]
        for i, n in enumerate(group):
            delta[n], new_m[n], new_v[n] = outs[0][i], outs[1][i], outs[2][i]

    return (loss, grad_x, *[grads[n] for n in WEIGHTS], *[delta[n] for n in WEIGHTS],
            *[new_m[n] for n in WEIGHTS], *[new_v[n] for n in WEIGHTS])


def kernel(x, meta_tokens, ffn1_pre_g, ffn1_w_gate, ffn1_w_up, ffn1_w_down, ffn1_post_g, mix_pre_g, w_in, lru_conv_w, lru_conv_b, lru_w_a, lru_b_a, lru_w_x, lru_b_x, lru_lambda, mla_q_norm_g, mla_w_uq, mla_kv_norm_g, mla_w_ukv, lru_out_g, mla_out_g, w_out, mix_post_g, ffn2_pre_g, ffn2_w_gate, ffn2_w_up, ffn2_w_down, ffn2_post_g, loss_target, m_meta_tokens, m_ffn1_pre_g, m_ffn1_w_gate, m_ffn1_w_up, m_ffn1_w_down, m_ffn1_post_g, m_mix_pre_g, m_w_in, m_lru_conv_w, m_lru_conv_b, m_lru_w_a, m_lru_b_a, m_lru_w_x, m_lru_b_x, m_lru_lambda, m_mla_q_norm_g, m_mla_w_uq, m_mla_kv_norm_g, m_mla_w_ukv, m_lru_out_g, m_mla_out_g, m_w_out, m_mix_post_g, m_ffn2_pre_g, m_ffn2_w_gate, m_ffn2_w_up, m_ffn2_w_down, m_ffn2_post_g, v_meta_tokens, v_ffn1_pre_g, v_ffn1_w_gate, v_ffn1_w_up, v_ffn1_w_down, v_ffn1_post_g, v_mix_pre_g, v_w_in, v_lru_conv_w, v_lru_conv_b, v_lru_w_a, v_lru_b_a, v_lru_w_x, v_lru_b_x, v_lru_lambda, v_mla_q_norm_g, v_mla_w_uq, v_mla_kv_norm_g, v_mla_w_ukv, v_lru_out_g, v_mla_out_g, v_w_out, v_mix_post_g, v_ffn2_pre_g, v_ffn2_w_gate, v_ffn2_w_up, v_ffn2_w_down, v_ffn2_post_g):
    return _step(dict(locals()))
```

```python
import functools
import math

import jax
import jax.numpy as jnp
from jax import lax
from jax.experimental import pallas as pl
from jax.experimental.pallas import tpu as pltpu

F32 = jnp.float32
BF16 = jnp.bfloat16

EPS = 1e-6
N_DEV = 8
LANES = 128
ROW_TILE = 384
SCAN_CHUNKS = 8
VMEM_LIMIT = 56 * 1024 * 1024
MM_ACC_BYTES = 8 * 1024 * 1024

LRU_C = 8.0
CONV_WIDTH = 4
MLA_HEADS = 8
QK_NOPE = 64
QK_ROPE = 32
ROPE_THETA = 10000.0

ADAM_LR = 0.001
ADAM_B1 = 0.9
ADAM_B2 = 0.999
ADAM_EPS = 1e-08
ADAM_WD = 0.01
ADAM_STEP = 10

MESH_AXES = ("x", "y", "c")
MESH = pl.DeviceIdType.MESH

WEIGHTS = ['meta_tokens', 'ffn1_pre_g', 'ffn1_w_gate', 'ffn1_w_up', 'ffn1_w_down', 'ffn1_post_g', 'mix_pre_g', 'w_in',
           'lru_conv_w', 'lru_conv_b', 'lru_w_a', 'lru_b_a', 'lru_w_x', 'lru_b_x', 'lru_lambda', 'mla_q_norm_g',
           'mla_w_uq', 'mla_kv_norm_g', 'mla_w_ukv', 'lru_out_g', 'mla_out_g', 'w_out', 'mix_post_g', 'ffn2_pre_g',
           'ffn2_w_gate', 'ffn2_w_up', 'ffn2_w_down', 'ffn2_post_g']
BIG = ['ffn1_w_gate', 'ffn1_w_up', 'ffn1_w_down', 'w_in', 'mla_w_uq', 'mla_w_ukv', 'w_out',
       'ffn2_w_gate', 'ffn2_w_up', 'ffn2_w_down']
SMALL_SHARDED = ['lru_conv_w', 'meta_tokens']
REPLICATED = [n for n in WEIGHTS if n not in BIG and n not in SMALL_SHARDED]


def _cparams(**kw):
    return pltpu.CompilerParams(vmem_limit_bytes=VMEM_LIMIT, **kw)


def _resident(shape):
    nd = len(shape)
    return pl.BlockSpec(shape, lambda *_: (0,) * nd, pipeline_mode=pl.Buffered(1))


def _rows(cols, tm=ROW_TILE):
    return pl.BlockSpec((tm, cols), lambda i: (i, 0))


def _const(shape):
    nd = len(shape)
    return pl.BlockSpec(shape, lambda *_: (0,) * nd)


def _dot(a, b):
    return jnp.dot(a, b, preferred_element_type=F32)


def _dot_nt(a, b):
    return lax.dot_general(a, b, (((1,), (1,)), ((), ())), preferred_element_type=F32)


def _dot_tn(a, b):
    return lax.dot_general(a, b, (((0,), (0,)), ((), ())), preferred_element_type=F32)


def _rms_fwd(x, g):
    rinv = lax.rsqrt(jnp.mean(x * x, axis=-1, keepdims=True) + EPS)
    xn = x * rinv
    return xn * g, xn, rinv


def _rms_bwd(xn, rinv, g, dy):
    dxn = dy * g
    dx = rinv * (dxn - xn * jnp.mean(dxn * xn, axis=-1, keepdims=True))
    return dx, jnp.sum(dy * xn, axis=0, keepdims=True)


def _accumulate(ref, val, first):
    @pl.when(first)
    def _():
        ref[...] = val

    @pl.when(jnp.logical_not(first))
    def _():
        ref[...] += val


class _Gather:
    def __init__(self, shards):
        self.n = len(shards)
        self.out_shape = tuple(jax.ShapeDtypeStruct((N_DEV,) + s.shape, s.dtype) for s in shards)
        self.scratch = [pltpu.SemaphoreType.DMA((7 * self.n,)), pltpu.SemaphoreType.DMA((7 * self.n,)),
                        pltpu.SemaphoreType.DMA((self.n,))]

    def bind(self, x_refs, out_refs, sems):
        self.x_refs, self.out_refs = x_refs, out_refs
        self.send_sems, self.recv_sems, self.local_sems = sems
        x, y, c = lax.axis_index("x"), lax.axis_index("y"), lax.axis_index("c")
        self.c = c
        self.me, self.sibling = (x, y, c), (x, y, 1 - c)
        self.chips = [(1 - x, y), (x, 1 - y), (1 - x, 1 - y)]

    def _copy(self, a, k, block, to, own=False):
        dst = self.out_refs[a].at[4 * block[0] + 2 * block[1] + block[2]]
        return pltpu.make_async_remote_copy(
            src_ref=self.x_refs[a] if own else dst, dst_ref=dst,
            send_sem=self.send_sems.at[7 * a + k], recv_sem=self.recv_sems.at[7 * a + k],
            device_id=to, device_id_type=MESH)

    def _mine(self):
        return [pltpu.make_async_copy(self.x_refs[a], self.out_refs[a].at[4 * self.me[0] + 2 * self.me[1] + self.c],
                                      self.local_sems.at[a]) for a in range(self.n)]

    def _first(self):
        first = []
        for a in range(self.n):
            first.append(self._copy(a, 0, self.me, self.sibling, own=True))
            first += [self._copy(a, 1 + j, self.me, (*chip, self.c), own=True) for j, chip in enumerate(self.chips)]
        return first

    def start(self):
        for cp in self._mine() + self._first():
            cp.start()

    def finish(self):
        c, n = self.c, self.n
        passed = []
        for j, chip in enumerate(self.chips):
            for a in range(n):
                self._copy(a, 1 + j, (*chip, c), self.me).wait_recv()
                cp = self._copy(a, 4 + j, (*chip, c), self.sibling)
                cp.start()
                passed.append(cp)
        for a in range(n):
            self._copy(a, 0, self.sibling, self.me).wait_recv()
        for j, chip in enumerate(self.chips):
            for a in range(n):
                self._copy(a, 4 + j, (*chip, 1 - c), self.me).wait_recv()
        for cp in self._first() + passed:
            cp.wait_send()
        for cp in self._mine():
            cp.wait()


class _Exchange:
    def __init__(self, parts):
        self.n = len(parts)
        self.out_shape = tuple(jax.ShapeDtypeStruct((3,) + p.shape[1:], p.dtype) for p in parts)
        self.scratch = [pltpu.SemaphoreType.DMA((3 * self.n,)), pltpu.SemaphoreType.DMA((3 * self.n,))]

    def bind(self, p_refs, out_refs, sems):
        self.p_refs, self.out_refs = p_refs, out_refs
        self.send_sems, self.recv_sems = sems

    def _copies(self):
        x, y, c = lax.axis_index("x"), lax.axis_index("y"), lax.axis_index("c")
        copies = []
        for a in range(self.n):
            for k, (tx, ty) in enumerate([(1 - x, y), (x, 1 - y), (1 - x, 1 - y)]):
                copies.append(pltpu.make_async_remote_copy(
                    src_ref=self.p_refs[a].at[2 * tx + ty], dst_ref=self.out_refs[a].at[k],
                    send_sem=self.send_sems.at[3 * a + k], recv_sem=self.recv_sems.at[3 * a + k],
                    device_id=(tx, ty, c), device_id_type=MESH))
        return copies

    def start(self):
        for cp in self._copies():
            cp.start()

    def finish(self):
        for cp in self._copies():
            cp.wait()


_HBM = pl.BlockSpec(memory_space=pl.ANY)


def _comm_call(comm, arrays, name):
    n = comm.n

    def body(*refs):
        comm.bind(refs[:n], refs[n:2 * n], refs[2 * n:])
        comm.start()
        comm.finish()

    return pl.pallas_call(
        body, name=name, out_shape=comm.out_shape,
        in_specs=[_HBM] * n, out_specs=[_HBM] * n, scratch_shapes=comm.scratch,
    )(*arrays)


def _all_gather(shards):
    return _comm_call(_Gather(shards), shards, "all_gather")


def _exchange_chips(parts):
    return _comm_call(_Exchange(parts), parts, "rs_chips")


def _swap_with_sibling(parts):
    n = len(parts)

    def body(*refs):
        p_refs, out_refs = refs[:n], refs[n:2 * n]
        send_sems, recv_sems = refs[2 * n:]
        x, y, c = lax.axis_index("x"), lax.axis_index("y"), lax.axis_index("c")
        copies = []
        for a in range(n):
            for k in range(4):
                cp = pltpu.make_async_remote_copy(
                    src_ref=p_refs[a].at[2 * k + (1 - c)], dst_ref=out_refs[a].at[k],
                    send_sem=send_sems.at[4 * a + k], recv_sem=recv_sems.at[4 * a + k],
                    device_id=(x, y, 1 - c), device_id_type=MESH)
                cp.start()
                copies.append(cp)
        for cp in copies:
            cp.wait()

    hbm = pl.BlockSpec(memory_space=pl.ANY)
    return pl.pallas_call(
        body, name="rs_sibling",
        out_shape=tuple(jax.ShapeDtypeStruct((4,) + p.shape[1:], p.dtype) for p in parts),
        in_specs=[hbm] * n, out_specs=[hbm] * n,
        scratch_shapes=[pltpu.SemaphoreType.DMA((4 * n,)), pltpu.SemaphoreType.DMA((4 * n,))],
    )(*parts)


def _row_block(rows, cap=2048):
    best = 8
    for t in range(8, min(rows, cap) + 1, 8):
        if rows % t == 0:
            best = t
    return best


def _elementwise_rows(rows, cols):
    return _row_block(rows, cap=max(8, (1 << 17) // cols // 8 * 8))


def _chip_partial(parts, got, core):
    _, rows, cols = got.shape
    tr = _elementwise_rows(rows, cols)
    p4 = parts.reshape(4, 2, rows, cols)

    def body(c_ref, a_ref, b_ref, o_ref):
        o_ref[...] = (a_ref[...].astype(F32) + b_ref[...].astype(F32)).astype(o_ref.dtype)

    return pl.pallas_call(
        body, name="rs_chip_partial",
        out_shape=jax.ShapeDtypeStruct((4, rows, cols), parts.dtype),
        grid_spec=pltpu.PrefetchScalarGridSpec(
            num_scalar_prefetch=1, grid=(4, rows // tr),
            in_specs=[pl.BlockSpec((None, None, tr, cols), lambda k, r, c: (k, c[0], r, 0)),
                      pl.BlockSpec((None, tr, cols), lambda k, r, c: (k, r, 0))],
            out_specs=pl.BlockSpec((None, tr, cols), lambda k, r, c: (k, r, 0))),
        compiler_params=_cparams(),
    )(core, p4, got)


def _adamw_math(w, m, v, g):
    mm = ADAM_B1 * m + (1.0 - ADAM_B1) * g
    vv = ADAM_B2 * v + (1.0 - ADAM_B2) * jnp.square(g)
    m_hat = mm / (1.0 - ADAM_B1 ** ADAM_STEP)
    v_hat = vv / (1.0 - ADAM_B2 ** ADAM_STEP)
    return -ADAM_LR * (m_hat / (jnp.sqrt(v_hat) + ADAM_EPS) + ADAM_WD * w), mm, vv


def _final_sum_adamw(partial, got, chip, w, m, v, bufs, layer):
    _, rows, cols = got.shape
    tr = _elementwise_rows(rows, cols)

    def body(c_ref, a_ref, b_ref, w_ref, m_ref, v_ref, *rest):
        g_ref, d_ref, nm_ref, nv_ref = rest[4:]
        g = ((a_ref[...].astype(F32) + b_ref[0].astype(F32)) + b_ref[1].astype(F32)) + b_ref[2].astype(F32)
        g_ref[...] = g
        d_ref[...], nm_ref[...], nv_ref[...] = _adamw_math(w_ref[...], m_ref[...], v_ref[...], g)

    one_layer = pl.BlockSpec((None, tr, cols), lambda r, c: (layer, r, 0))
    return pl.pallas_call(
        body, name="rs_final_adamw",
        out_shape=tuple(jax.ShapeDtypeStruct(b.shape, F32) for b in bufs),
        grid_spec=pltpu.PrefetchScalarGridSpec(
            num_scalar_prefetch=1, grid=(rows // tr,),
            in_specs=[pl.BlockSpec((None, tr, cols), lambda r, c: (c[0], r, 0)),
                      pl.BlockSpec((3, tr, cols), lambda r, c: (0, r, 0)), one_layer, one_layer, one_layer] + [_HBM] * 4,
            out_specs=[one_layer] * 4),
        input_output_aliases={6: 0, 7: 1, 8: 2, 9: 3},
        compiler_params=_cparams(),
    )(chip, partial, got, w, m, v, *bufs)


def _sum_devices(gathered):
    _, rows, lanes = gathered.shape
    tr = _row_block(rows)

    def body(g_ref, o_ref):
        acc = g_ref[0]
        for j in range(1, N_DEV):
            acc = acc + g_ref[j]
        o_ref[...] = acc

    return pl.pallas_call(
        body, name="small_grad_sum",
        out_shape=jax.ShapeDtypeStruct((rows, lanes), F32),
        grid=(rows // tr,),
        in_specs=[pl.BlockSpec((N_DEV, tr, lanes), lambda r: (0, r, 0))],
        out_specs=pl.BlockSpec((tr, lanes), lambda r: (r, 0)),
        compiler_params=_cparams(),
    )(gathered)


def _assemble(g, kind, n_in=0):
    _, A, B = g.shape
    if kind == 'cols':
        oshape = (A, N_DEV * B)
    elif kind == 'rows':
        oshape = (N_DEV * A, B)
    elif kind == 'w_in':
        oshape = (A, n_in + LANES)
    else:
        oshape = (A, N_DEV * LANES)

    def body(x_ref, o_ref):
        if kind == 'cols':
            for j in range(N_DEV):
                o_ref[:, j * B:(j + 1) * B] = x_ref[j]
        elif kind == 'rows':
            for j in range(N_DEV):
                o_ref[j * A:(j + 1) * A, :] = x_ref[j]
        elif kind == 'w_in':
            nat = jnp.concatenate([x_ref[j] for j in range(N_DEV)], axis=1)
            o_ref[:, 0:n_in] = nat[:, 0:n_in]
            o_ref[:, n_in:] = jnp.concatenate(
                [jnp.zeros((A, QK_NOPE), g.dtype), nat[:, n_in:],
                 jnp.zeros((A, LANES - QK_NOPE - QK_ROPE), g.dtype)], axis=1)
        else:
            o_ref[...] = jnp.zeros(oshape, g.dtype)
            for j in range(N_DEV):
                o_ref[:, j * LANES:j * LANES + B] = x_ref[j]

    return pl.pallas_call(
        body, name="assemble_" + kind,
        out_shape=jax.ShapeDtypeStruct(oshape, g.dtype),
        compiler_params=_cparams(),
    )(g)


def _wgrad(a, b, shard, kind, n_in=0):
    K, M = a.shape
    N = b.shape[1]
    shard = tuple(shard)
    tk = ROW_TILE
    nk = K // tk
    if kind == 'cols':
        S = shard[1]
        tn = S
        for t in range(S, N + 1, S):
            if N % t == 0 and t % LANES == 0 and M * t * 4 <= MM_ACC_BYTES:
                tn = t
        oblock = pl.BlockSpec((tn // S,) + shard, lambda n, k: (n, 0, 0))
    elif kind == 'rows':
        tn = LANES
        for t in range(LANES, N + 1, LANES):
            if N % t == 0 and M * t * 4 <= MM_ACC_BYTES:
                tn = t
        oblock = pl.BlockSpec((N_DEV, shard[0], tn), lambda n, k: (0, 0, n))
    else:
        tn = N
        oblock = pl.BlockSpec((N_DEV,) + shard, lambda n, k: (0, 0, 0))

    def body(a_ref, b_ref, o_ref, acc_ref):
        k = pl.program_id(1)

        @pl.when(k == 0)
        def _():
            acc_ref[...] = jnp.zeros_like(acc_ref)

        acc_ref[...] += _dot_tn(a_ref[...], b_ref[...])

        @pl.when(k == nk - 1)
        def _():
            acc = acc_ref[...]
            if kind == 'cols':
                S = shard[1]
                for j in range(tn // S):
                    o_ref[j] = acc[:, j * S:(j + 1) * S].astype(o_ref.dtype)
            elif kind == 'rows':
                S = shard[0]
                for j in range(N_DEV):
                    o_ref[j] = acc[j * S:(j + 1) * S, :].astype(o_ref.dtype)
            elif kind == 'w_in':
                S = shard[1]
                nat = jnp.concatenate([acc[:, 0:n_in], acc[:, n_in + QK_NOPE:n_in + QK_NOPE + QK_ROPE]], axis=1)
                for j in range(N_DEV):
                    o_ref[j] = nat[:, j * S:(j + 1) * S].astype(o_ref.dtype)
            else:
                S = shard[1]
                for j in range(N_DEV):
                    o_ref[j] = acc[:, j * LANES:j * LANES + S].astype(o_ref.dtype)

    return pl.pallas_call(
        body, name="wgrad_" + kind,
        out_shape=jax.ShapeDtypeStruct((N_DEV,) + shard, BF16),
        grid=(N // tn, nk),
        in_specs=[pl.BlockSpec((tk, M), lambda n, k: (k, 0)), pl.BlockSpec((tk, tn), lambda n, k: (k, n))],
        out_specs=oblock,
        scratch_shapes=[pltpu.VMEM((M, tn), F32)],
        compiler_params=_cparams(),
    )(a, b)


def _adamw(w, m, v, g):
    rows, cols = w.shape
    tr = _elementwise_rows(rows, cols) if rows % 8 == 0 else rows

    def body(w_ref, m_ref, v_ref, g_ref, d_ref, nm_ref, nv_ref):
        d_ref[...], nm_ref[...], nv_ref[...] = _adamw_math(w_ref[...], m_ref[...], v_ref[...], g_ref[...])

    spec = pl.BlockSpec((tr, cols), lambda r: (r, 0))
    return pl.pallas_call(
        body, name="adamw",
        out_shape=(jax.ShapeDtypeStruct((rows, cols), F32),) * 3,
        grid=(rows // tr,),
        in_specs=[spec] * 4, out_specs=[spec] * 3,
        compiler_params=_cparams(),
    )(w, m, v, g)


def _hidden_chunk(F):
    return F // 2 if (F // 2) % LANES == 0 else F


def _ffn_fwd(h, pre_g, post_g, wg, wu, wd):
    Tp, D = h.shape
    F = wg.shape[1]
    fc = _hidden_chunk(F)

    def body(h_ref, pg_ref, qg_ref, wg_ref, wu_ref, wd_ref, ho_ref, a_ref, b_ref, f_ref):
        hx = h_ref[...]
        u, _, _ = _rms_fwd(hx, pg_ref[...])
        ub = u.astype(BF16)
        f = jnp.zeros((ROW_TILE, D), F32)
        for c in range(F // fc):
            sl = slice(c * fc, (c + 1) * fc)
            a = _dot(ub, wg_ref[:, sl])
            b = _dot(ub, wu_ref[:, sl])
            a_ref[:, sl] = a.astype(BF16)
            b_ref[:, sl] = b.astype(BF16)
            act = (a * jax.nn.sigmoid(a) * b).astype(BF16)
            f = f + _dot(act, wd_ref[sl, :])
        f_ref[...] = f
        y, _, _ = _rms_fwd(f, qg_ref[...])
        ho_ref[...] = hx + 0.5 * y

    return pl.pallas_call(
        body, name="ffn_fwd",
        out_shape=(jax.ShapeDtypeStruct((Tp, D), F32), jax.ShapeDtypeStruct((Tp, F), BF16),
                   jax.ShapeDtypeStruct((Tp, F), BF16), jax.ShapeDtypeStruct((Tp, D), F32)),
        grid=(Tp // ROW_TILE,),
        in_specs=[_rows(D), _const((1, D)), _const((1, D)), _resident((D, F)), _resident((D, F)), _resident((F, D))],
        out_specs=[_rows(D), _rows(F), _rows(F), _rows(D)],
        compiler_params=_cparams(),
    )(h, pre_g, post_g, wg, wu, wd)


def _ffn_bwd_down(dh, a, b, f, post_g, wd):
    Tp, D = dh.shape
    F = a.shape[1]
    fc = _hidden_chunk(F)

    def body(dh_ref, a_ref, b_ref, f_ref, qg_ref, wd_ref, da_ref, db_ref, act_ref, df_ref, dqg_ref):
        qg = qg_ref[...]
        _, fn, frinv = _rms_fwd(f_ref[...], qg)
        df, dqg = _rms_bwd(fn, frinv, qg, 0.5 * dh_ref[...])
        dfb = df.astype(BF16)
        df_ref[...] = dfb
        for c in range(F // fc):
            sl = slice(c * fc, (c + 1) * fc)
            dact = _dot_nt(dfb, wd_ref[sl, :])
            av = a_ref[:, sl].astype(F32)
            bv = b_ref[:, sl].astype(F32)
            sig = jax.nn.sigmoid(av)
            silu = av * sig
            act_ref[:, sl] = (silu * bv).astype(BF16)
            da_ref[:, sl] = (dact * bv * (sig * (1.0 + av * (1.0 - sig)))).astype(BF16)
            db_ref[:, sl] = (dact * silu).astype(BF16)
        _accumulate(dqg_ref, dqg, pl.program_id(0) == 0)

    return pl.pallas_call(
        body, name="ffn_bwd_down",
        out_shape=(jax.ShapeDtypeStruct((Tp, F), BF16), jax.ShapeDtypeStruct((Tp, F), BF16),
                   jax.ShapeDtypeStruct((Tp, F), BF16), jax.ShapeDtypeStruct((Tp, D), BF16),
                   jax.ShapeDtypeStruct((1, D), F32)),
        grid=(Tp // ROW_TILE,),
        in_specs=[_rows(D), _rows(F), _rows(F), _rows(D), _const((1, D)), _resident((F, D))],
        out_specs=[_rows(F), _rows(F), _rows(F), _rows(D), _const((1, D))],
        compiler_params=_cparams(),
    )(dh, a, b, f, post_g, wd)


def _ffn_bwd_up(dh, h, da, db, pre_g, wg, wu):
    Tp, D = dh.shape
    F = da.shape[1]

    def body(dh_ref, h_ref, da_ref, db_ref, pg_ref, wg_ref, wu_ref, dhi_ref, u_ref, dpg_ref):
        pg = pg_ref[...]
        u, hn, hrinv = _rms_fwd(h_ref[...], pg)
        u_ref[...] = u.astype(BF16)
        du = _dot_nt(da_ref[...], wg_ref[...]) + _dot_nt(db_ref[...], wu_ref[...])
        dx, dpg = _rms_bwd(hn, hrinv, pg, du)
        dhi_ref[...] = dh_ref[...] + dx
        _accumulate(dpg_ref, dpg, pl.program_id(0) == 0)

    return pl.pallas_call(
        body, name="ffn_bwd_up",
        out_shape=(jax.ShapeDtypeStruct((Tp, D), F32), jax.ShapeDtypeStruct((Tp, D), BF16),
                   jax.ShapeDtypeStruct((1, D), F32)),
        grid=(Tp // ROW_TILE,),
        in_specs=[_rows(D), _rows(D), _rows(F), _rows(F), _const((1, D)), _resident((D, F)), _resident((D, F))],
        out_specs=[_rows(D), _rows(D), _const((1, D))],
        compiler_params=_cparams(),
    )(dh, h, da, db, pre_g, wg, wu)


def _mix_in_fwd(h, g, win, widths):
    Tp, D = h.shape
    ncol = win.shape[1]
    offs = [0]
    for wd_ in widths:
        offs.append(offs[-1] + wd_)

    def body(h_ref, g_ref, w_ref, *out_refs):
        y, _, _ = _rms_fwd(h_ref[...], g_ref[...])
        z = _dot(y.astype(BF16), w_ref[...])
        for o_ref, lo, wd_ in zip(out_refs, offs, widths):
            o_ref[...] = z[:, lo:lo + wd_]

    return pl.pallas_call(
        body, name="mix_in_fwd",
        out_shape=tuple(jax.ShapeDtypeStruct((Tp, wd_), F32) for wd_ in widths),
        grid=(Tp // ROW_TILE,),
        in_specs=[_rows(D), _const((1, D)), _resident((D, ncol))],
        out_specs=[_rows(wd_) for wd_ in widths],
        compiler_params=_cparams(),
    )(h, g, win)


def _mix_in_bwd(dres, h, dparts, g, win):
    Tp, D = h.shape
    ncol = win.shape[1]
    widths = [p.shape[1] for p in dparts]
    n = len(dparts)

    def body(*refs):
        dres_ref, h_ref = refs[0], refs[1]
        part_refs = refs[2:2 + n]
        g_ref, w_ref = refs[2 + n], refs[3 + n]
        dh_ref, dz_ref, u_ref, dg_ref = refs[4 + n:]
        dz = jnp.concatenate([r[...] for r in part_refs], axis=1).astype(BF16)
        dz_ref[...] = dz
        gg = g_ref[...]
        u, hn, rinv = _rms_fwd(h_ref[...], gg)
        u_ref[...] = u.astype(BF16)
        dx, dg = _rms_bwd(hn, rinv, gg, _dot_nt(dz, w_ref[...]))
        dh_ref[...] = dres_ref[...] + dx
        _accumulate(dg_ref, dg, pl.program_id(0) == 0)

    return pl.pallas_call(
        body, name="mix_in_bwd",
        out_shape=(jax.ShapeDtypeStruct((Tp, D), F32), jax.ShapeDtypeStruct((Tp, ncol), BF16),
                   jax.ShapeDtypeStruct((Tp, D), BF16), jax.ShapeDtypeStruct((1, D), F32)),
        grid=(Tp // ROW_TILE,),
        in_specs=[_rows(D), _rows(D)] + [_rows(wd_) for wd_ in widths] + [_const((1, D)), _resident((D, ncol))],
        out_specs=[_rows(D), _rows(ncol), _rows(D), _const((1, D))],
        compiler_params=_cparams(),
    )(dres, h, *dparts, g, win)


def _mix_out_fwd(h, y_lru, y_mla, lru_g, mla_g, post_g, wout):
    Tp, D = h.shape
    W = y_lru.shape[1]

    def body(h_ref, yl_ref, ym_ref, lg_ref, mg_ref, pg_ref, w_ref, ho_ref, y_ref):
        yl, _, _ = _rms_fwd(yl_ref[...], lg_ref[...])
        ym, _, _ = _rms_fwd(ym_ref[...], mg_ref[...])
        y = _dot(yl.astype(BF16), w_ref[0:W, :]) + _dot(ym.astype(BF16), w_ref[W:, :])
        y_ref[...] = y
        yn, _, _ = _rms_fwd(y, pg_ref[...])
        ho_ref[...] = h_ref[...] + yn

    return pl.pallas_call(
        body, name="mix_out_fwd",
        out_shape=(jax.ShapeDtypeStruct((Tp, D), F32), jax.ShapeDtypeStruct((Tp, D), F32)),
        grid=(Tp // ROW_TILE,),
        in_specs=[_rows(D), _rows(W), _rows(W), _const((1, W)), _const((1, W)), _const((1, D)), _resident(wout.shape)],
        out_specs=[_rows(D), _rows(D)],
        compiler_params=_cparams(),
    )(h, y_lru, y_mla, lru_g, mla_g, post_g, wout)


def _mix_out_bwd(dh, y, y_lru, y_mla, lru_g, mla_g, post_g, wout):
    Tp, D = dh.shape
    W = y_lru.shape[1]

    def body(dh_ref, y_ref, yl_ref, ym_ref, lg_ref, mg_ref, pg_ref, w_ref,
             dyl_ref, dym_ref, dy_ref, cat_ref, dlg_ref, dmg_ref, dpg_ref):
        first = pl.program_id(0) == 0
        pg, lg, mg = pg_ref[...], lg_ref[...], mg_ref[...]
        _, yn, yrinv = _rms_fwd(y_ref[...], pg)
        dy, dpg = _rms_bwd(yn, yrinv, pg, dh_ref[...])
        dyb = dy.astype(BF16)
        dy_ref[...] = dyb
        dcat = _dot_nt(dyb, w_ref[...])
        yl, yln, ylr = _rms_fwd(yl_ref[...], lg)
        ym, ymn, ymr = _rms_fwd(ym_ref[...], mg)
        cat_ref[:, 0:W] = yl.astype(BF16)
        cat_ref[:, W:] = ym.astype(BF16)
        dyl, dlg = _rms_bwd(yln, ylr, lg, dcat[:, 0:W])
        dym, dmg = _rms_bwd(ymn, ymr, mg, dcat[:, W:])
        dyl_ref[...] = dyl
        dym_ref[...] = dym
        _accumulate(dlg_ref, dlg, first)
        _accumulate(dmg_ref, dmg, first)
        _accumulate(dpg_ref, dpg, first)

    return pl.pallas_call(
        body, name="mix_out_bwd",
        out_shape=(jax.ShapeDtypeStruct((Tp, W), F32), jax.ShapeDtypeStruct((Tp, W), F32),
                   jax.ShapeDtypeStruct((Tp, D), BF16), jax.ShapeDtypeStruct((Tp, 2 * W), BF16),
                   jax.ShapeDtypeStruct((1, W), F32), jax.ShapeDtypeStruct((1, W), F32),
                   jax.ShapeDtypeStruct((1, D), F32)),
        grid=(Tp // ROW_TILE,),
        in_specs=[_rows(D), _rows(D), _rows(W), _rows(W), _const((1, W)), _const((1, W)), _const((1, D)),
                  _resident(wout.shape)],
        out_specs=[_rows(W), _rows(W), _rows(D), _rows(2 * W), _const((1, W)), _const((1, W)), _const((1, D))],
        compiler_params=_cparams(),
    )(dh, y, y_lru, y_mla, lru_g, mla_g, post_g, wout)


def _softplus_neg(lam):
    return jnp.maximum(-lam, 0.0) + jnp.log1p(jnp.exp(-jnp.abs(lam)))


def _neg_expm1(y):
    series = -y * (1.0 + 0.5 * y * (1.0 + (1.0 / 3.0) * y * (1.0 + 0.25 * y)))
    return jnp.where(y > -0.01, series, 1.0 - jnp.exp(y))


_GELU_K = math.sqrt(2.0 / math.pi)


def _gelu(x):
    return 0.5 * x * (1.0 + jnp.tanh(_GELU_K * (x + 0.044715 * (x * x * x))))


def _gelu_grad(x):
    t = jnp.tanh(_GELU_K * (x + 0.044715 * (x * x * x)))
    return 0.5 * (1.0 + t) + 0.5 * x * (1.0 - t * t) * (_GELU_K * (1.0 + 3.0 * 0.044715 * (x * x)))


def _lru_conv(xpad_ref, lo, n, cw, cb):
    xc = xpad_ref[pl.ds(8 + lo, n), :] * cw[3:4, :] + cb
    for k in range(CONV_WIDTH - 1):
        xc = xc + xpad_ref[pl.ds(8 + lo - (CONV_WIDTH - 1 - k), n), :] * cw[k:k + 1, :]
    return xc


def _lru_gates(xc, wa, wx, ba, bx, sp):
    xb = xc.astype(BF16)
    r = jax.nn.sigmoid(_dot(xb, wa) + ba)
    i = jax.nn.sigmoid(_dot(xb, wx) + bx)
    la = (-LRU_C * r) * sp
    a = jnp.exp(la)
    mult = jnp.sqrt(_neg_expm1(2.0 * la))
    return r, i, a, mult


def _scan_carries(last_h, last_p, reverse):
    row = lax.broadcasted_iota(jnp.int32, last_h.shape, 0)
    carry = jnp.zeros_like(last_h)
    for _ in range(SCAN_CHUNKS - 1):
        nxt = last_h + last_p * carry
        if reverse:
            carry = jnp.where(row < SCAN_CHUNKS - 1, pltpu.roll(nxt, SCAN_CHUNKS - 1, axis=0), 0.0)
        else:
            carry = jnp.where(row > 0, pltpu.roll(nxt, 1, axis=0), 0.0)
    return carry


def _lru_forward_scan(a_ref, h_ref, p_ref, a_off, h_off, rc):
    zero = jnp.zeros((SCAN_CHUNKS, LANES), F32)

    def step(i, carry):
        hh, pp = carry
        av = a_ref[pl.ds(a_off + i, SCAN_CHUNKS, stride=rc), :]
        hh = av * hh + h_ref[pl.ds(h_off + i, SCAN_CHUNKS, stride=rc), :]
        pp = av * pp
        h_ref[pl.ds(h_off + i, SCAN_CHUNKS, stride=rc), :] = hh
        p_ref[pl.ds(i, SCAN_CHUNKS, stride=rc), :] = pp
        return hh, pp

    last_h, last_p = lax.fori_loop(0, rc, step, (zero, zero + 1.0))
    carry = _scan_carries(last_h, last_p, reverse=False)
    for c in range(SCAN_CHUNKS):
        rows = pl.ds(h_off + c * rc, rc)
        h_ref[rows, :] = h_ref[rows, :] + p_ref[pl.ds(c * rc, rc), :] * carry[c:c + 1, :]


def _rglru_fwd(xr, gr, cw, cb, wa, wx, ba, bx, lam):
    Tp, W = xr.shape
    nb = W // LANES
    rc = Tp // SCAN_CHUNKS

    def body(xr_ref, gr_ref, cw_ref, cb_ref, wa_ref, wx_ref, ba_ref, bx_ref, lam_ref, y_ref,
             xpad, a_s, h_s, p_s):
        xpad[0:8, :] = jnp.zeros((8, LANES), F32)
        xpad[pl.ds(8, Tp), :] = xr_ref[...]
        cw_, cb_ = cw_ref[...], cb_ref[...]
        sp = _softplus_neg(lam_ref[...])
        for c in range(SCAN_CHUNKS):
            xc = _lru_conv(xpad, c * rc, rc, cw_, cb_)
            _, i, a, mult = _lru_gates(xc, wa_ref[...], wx_ref[...], ba_ref[...], bx_ref[...], sp)
            a_s[pl.ds(c * rc, rc), :] = a
            h_s[pl.ds(c * rc, rc), :] = mult * (i * xc)
        _lru_forward_scan(a_s, h_s, p_s, 0, 0, rc)
        for c in range(SCAN_CHUNKS):
            rows = pl.ds(c * rc, rc)
            y_ref[rows, :] = h_s[rows, :] * _gelu(gr_ref[rows, :])

    col = pl.BlockSpec((Tp, LANES), lambda j: (0, j))
    vec = pl.BlockSpec((1, LANES), lambda j: (0, j))
    mat = pl.BlockSpec((None, LANES, LANES), lambda j: (j, 0, 0))
    return pl.pallas_call(
        body, name="rglru_fwd",
        out_shape=jax.ShapeDtypeStruct((Tp, W), F32),
        grid=(nb,),
        in_specs=[col, col, pl.BlockSpec((CONV_WIDTH, LANES), lambda j: (0, j)), vec, mat, mat, vec, vec, vec],
        out_specs=col,
        scratch_shapes=[pltpu.VMEM((Tp + 8, LANES), F32), pltpu.VMEM((Tp, LANES), F32),
                        pltpu.VMEM((Tp, LANES), F32), pltpu.VMEM((Tp, LANES), F32)],
        compiler_params=_cparams(),
    )(xr, gr, cw, cb, wa, wx, ba, bx, lam)


def _rglru_bwd(dy, xr, gr, cw, cb, wa, wx, ba, bx, lam):
    Tp, W = xr.shape
    nb = W // LANES
    rc = Tp // SCAN_CHUNKS

    def body(dy_ref, xr_ref, gr_ref, cw_ref, cb_ref, wa_ref, wx_ref, ba_ref, bx_ref, lam_ref,
             dxr_ref, dgr_ref, dcw_ref, dcb_ref, dwa_ref, dwx_ref, dba_ref, dbx_ref, dlam_ref,
             xpad, a_s, h_s, p_s, xc_s, g_s, dxc_s):
        zeros8 = jnp.zeros((8, LANES), F32)
        xpad[0:8, :] = zeros8
        xpad[pl.ds(8, Tp), :] = xr_ref[...]
        a_s[pl.ds(Tp, 8), :] = zeros8
        h_s[0:8, :] = zeros8
        dxc_s[pl.ds(Tp, 8), :] = zeros8
        cw_, cb_ = cw_ref[...], cb_ref[...]
        lam_ = lam_ref[...]
        sp = _softplus_neg(lam_)
        gate_args = (wa_ref[...], wx_ref[...], ba_ref[...], bx_ref[...], sp)

        for c in range(SCAN_CHUNKS):
            rows = pl.ds(c * rc, rc)
            xc = _lru_conv(xpad, c * rc, rc, cw_, cb_)
            xc_s[rows, :] = xc
            _, i, a, mult = _lru_gates(xc, *gate_args)
            a_s[rows, :] = a
            h_s[pl.ds(8 + c * rc, rc), :] = mult * (i * xc)
        _lru_forward_scan(a_s, h_s, p_s, 0, 8, rc)

        for c in range(SCAN_CHUNKS):
            rows = pl.ds(c * rc, rc)
            dyv, grv = dy_ref[rows, :], gr_ref[rows, :]
            g_s[rows, :] = dyv * _gelu(grv)
            dgr_ref[rows, :] = dyv * h_s[pl.ds(8 + c * rc, rc), :] * _gelu_grad(grv)

        zero = jnp.zeros((SCAN_CHUNKS, LANES), F32)

        def rstep(k, carry):
            gg, qq = carry
            i = rc - 1 - k
            av = a_s[pl.ds(i + 1, SCAN_CHUNKS, stride=rc), :]
            gg = g_s[pl.ds(i, SCAN_CHUNKS, stride=rc), :] + av * gg
            qq = av * qq
            g_s[pl.ds(i, SCAN_CHUNKS, stride=rc), :] = gg
            p_s[pl.ds(i, SCAN_CHUNKS, stride=rc), :] = qq
            return gg, qq

        first_g, first_q = lax.fori_loop(0, rc, rstep, (zero, zero + 1.0))
        carry = _scan_carries(first_g, first_q, reverse=True)
        for c in range(SCAN_CHUNKS):
            rows = pl.ds(c * rc, rc)
            g_s[rows, :] = g_s[rows, :] + p_s[rows, :] * carry[c:c + 1, :]

        dwa = jnp.zeros((LANES, LANES), F32)
        dwx = jnp.zeros((LANES, LANES), F32)
        dba = jnp.zeros((1, LANES), F32)
        dbx = jnp.zeros((1, LANES), F32)
        dsp = jnp.zeros((1, LANES), F32)
        for c in range(SCAN_CHUNKS):
            rows = pl.ds(c * rc, rc)
            xc = xc_s[rows, :]
            r, i, a, mult = _lru_gates(xc, *gate_args)
            gg = g_s[rows, :]
            da = gg * h_s[pl.ds(7 + c * rc, rc), :]
            d_s = gg * mult
            dmult = gg * (i * xc)
            dla = da * a - dmult * (a * a) / mult
            dr = dla * (-LRU_C * sp)
            dsp = dsp + jnp.sum(dla * (-LRU_C * r), axis=0, keepdims=True)
            dpr = dr * r * (1.0 - r)
            dpi = (d_s * xc) * i * (1.0 - i)
            dprb, dpib, xcb = dpr.astype(BF16), dpi.astype(BF16), xc.astype(BF16)
            dxc_s[rows, :] = d_s * i + _dot_nt(dprb, wa_ref[...]) + _dot_nt(dpib, wx_ref[...])
            dwa = dwa + _dot_tn(xcb, dprb)
            dwx = dwx + _dot_tn(xcb, dpib)
            dba = dba + jnp.sum(dpr, axis=0, keepdims=True)
            dbx = dbx + jnp.sum(dpi, axis=0, keepdims=True)
        dwa_ref[...] = dwa
        dwx_ref[...] = dwx
        dba_ref[...] = dba
        dbx_ref[...] = dbx
        dlam_ref[...] = dsp * (-jax.nn.sigmoid(-lam_))

        dcw = [jnp.zeros((1, LANES), F32) for _ in range(CONV_WIDTH)]
        dcb = jnp.zeros((1, LANES), F32)
        for c in range(SCAN_CHUNKS):
            rows = pl.ds(c * rc, rc)
            dxc = dxc_s[rows, :]
            dcb = dcb + jnp.sum(dxc, axis=0, keepdims=True)
            dxr = dxc * cw_[3:4, :]
            for k in range(CONV_WIDTH):
                back = CONV_WIDTH - 1 - k
                dcw[k] = dcw[k] + jnp.sum(dxc * xpad[pl.ds(8 + c * rc - back, rc), :], axis=0, keepdims=True)
                if back:
                    dxr = dxr + dxc_s[pl.ds(c * rc + back, rc), :] * cw_[k:k + 1, :]
            dxr_ref[rows, :] = dxr
        dcw_ref[...] = jnp.concatenate(dcw, axis=0)
        dcb_ref[...] = dcb

    col = pl.BlockSpec((Tp, LANES), lambda j: (0, j))
    vec = pl.BlockSpec((1, LANES), lambda j: (0, j))
    mat = pl.BlockSpec((None, LANES, LANES), lambda j: (j, 0, 0))
    cwspec = pl.BlockSpec((CONV_WIDTH, LANES), lambda j: (0, j))
    return pl.pallas_call(
        body, name="rglru_bwd",
        out_shape=(jax.ShapeDtypeStruct((Tp, W), F32), jax.ShapeDtypeStruct((Tp, W), F32),
                   jax.ShapeDtypeStruct((CONV_WIDTH, W), F32), jax.ShapeDtypeStruct((1, W), F32),
                   jax.ShapeDtypeStruct((nb, LANES, LANES), F32), jax.ShapeDtypeStruct((nb, LANES, LANES), F32),
                   jax.ShapeDtypeStruct((1, W), F32), jax.ShapeDtypeStruct((1, W), F32),
                   jax.ShapeDtypeStruct((1, W), F32)),
        grid=(nb,),
        in_specs=[col, col, col, cwspec, vec, mat, mat, vec, vec, vec],
        out_specs=[col, col, cwspec, vec, mat, mat, vec, vec, vec],
        scratch_shapes=[pltpu.VMEM((Tp + 8, LANES), F32), pltpu.VMEM((Tp + 8, LANES), F32),
                        pltpu.VMEM((Tp + 8, LANES), F32), pltpu.VMEM((Tp, LANES), F32),
                        pltpu.VMEM((Tp, LANES), F32), pltpu.VMEM((Tp, LANES), F32),
                        pltpu.VMEM((Tp + 8, LANES), F32)],
        compiler_params=_cparams(),
    )(dy, xr, gr, cw, cb, wa, wx, ba, bx, lam)


def _rope(x, cosm, sin_hi, sin_lo):
    return x * cosm + pltpu.roll(x, QK_ROPE // 2, axis=1) * sin_hi + pltpu.roll(x, LANES - QK_ROPE // 2, axis=1) * sin_lo


def _rope_inverse(x, cosm, sin_hi, sin_lo):
    return x * cosm - pltpu.roll(x, QK_ROPE // 2, axis=1) * sin_hi - pltpu.roll(x, LANES - QK_ROPE // 2, axis=1) * sin_lo


def _mla_proj_fwd(cq, ckv, kr, qg, kg, wuq, wkv, cosm, sin_hi, sin_lo):
    Tp, QL = cq.shape
    KL = ckv.shape[1]
    H = MLA_HEADS
    half = LANES // 2

    def body(cq_ref, ckv_ref, kr_ref, qg_ref, kg_ref, wuq_ref, wkv_ref, c_ref, s1_ref, s2_ref,
             q_ref, k_ref, v_ref):
        tabs = (c_ref[...], s1_ref[...], s2_ref[...])
        lane = lax.broadcasted_iota(jnp.int32, (ROW_TILE, LANES), 1)
        low = lane < half
        cqn, _, _ = _rms_fwd(cq_ref[...], qg_ref[...])
        q = _dot(cqn.astype(BF16), wuq_ref[...])
        ckvn, _, _ = _rms_fwd(ckv_ref[...], kg_ref[...])
        kv = _dot(ckvn.astype(BF16), wkv_ref[...])
        krr = _rope(kr_ref[...], *tabs)
        for hd in range(H):
            sl = slice(hd * LANES, (hd + 1) * LANES)
            q_ref[hd] = _rope(q[:, sl], *tabs).astype(BF16)
            k_ref[hd] = jnp.where(low, kv[:, sl], krr).astype(BF16)
        for p in range(H // 2):
            even = kv[:, 2 * p * LANES:(2 * p + 1) * LANES]
            odd = kv[:, (2 * p + 1) * LANES:(2 * p + 2) * LANES]
            v_ref[:, p * LANES:(p + 1) * LANES] = jnp.where(low, pltpu.roll(even, half, axis=1), odd).astype(BF16)

    heads = pl.BlockSpec((H, ROW_TILE, LANES), lambda i: (0, i, 0))
    return pl.pallas_call(
        body, name="mla_proj_fwd",
        out_shape=(jax.ShapeDtypeStruct((H, Tp, LANES), BF16), jax.ShapeDtypeStruct((H, Tp, LANES), BF16),
                   jax.ShapeDtypeStruct((Tp, H * half), BF16)),
        grid=(Tp // ROW_TILE,),
        in_specs=[_rows(QL), _rows(KL), _rows(LANES), _const((1, QL)), _const((1, KL)),
                  _resident(wuq.shape), _resident(wkv.shape), _rows(LANES), _rows(LANES), _rows(LANES)],
        out_specs=[heads, heads, _rows(H * half)],
        compiler_params=_cparams(),
    )(cq, ckv, kr, qg, kg, wuq, wkv, cosm, sin_hi, sin_lo)


def _mla_proj_bwd(dq, dk, dv, cq, ckv, qg, kg, wuq, wkv, cosm, sin_hi, sin_lo):
    H, Tp, _ = dq.shape
    QL, KL = cq.shape[1], ckv.shape[1]
    half = LANES // 2

    def body(dq_ref, dk_ref, dv_ref, cq_ref, ckv_ref, qg_ref, kg_ref, wuq_ref, wkv_ref,
             c_ref, s1_ref, s2_ref,
             dcq_ref, dckv_ref, dkr_ref, dqa_ref, dkva_ref, cqn_ref, ckvn_ref, dqg_ref, dkg_ref):
        first = pl.program_id(0) == 0
        tabs = (c_ref[...], s1_ref[...], s2_ref[...])
        lane = lax.broadcasted_iota(jnp.int32, (ROW_TILE, LANES), 1)
        low = lane < half
        rope_lanes = jnp.logical_and(lane >= QK_NOPE, lane < QK_NOPE + QK_ROPE)
        dkr = jnp.zeros((ROW_TILE, LANES), F32)
        for hd in range(H):
            sl = slice(hd * LANES, (hd + 1) * LANES)
            dqa_ref[:, sl] = _rope_inverse(dq_ref[hd], *tabs).astype(BF16)
            dkh = dk_ref[hd]
            dvp = dv_ref[:, (hd // 2) * LANES:(hd // 2 + 1) * LANES]
            dvh = pltpu.roll(dvp, half, axis=1) if hd % 2 == 0 else dvp
            dkva_ref[:, sl] = jnp.where(low, dkh, dvh).astype(BF16)
            dkr = dkr + jnp.where(rope_lanes, dkh, 0.0)
        dkr_ref[...] = _rope_inverse(dkr, *tabs)
        qg, kg = qg_ref[...], kg_ref[...]
        cqs, cqn, cqr = _rms_fwd(cq_ref[...], qg)
        cqn_ref[...] = cqs.astype(BF16)
        dcq, dqg = _rms_bwd(cqn, cqr, qg, _dot_nt(dqa_ref[...], wuq_ref[...]))
        dcq_ref[...] = dcq
        cks, ckn, ckr = _rms_fwd(ckv_ref[...], kg)
        ckvn_ref[...] = cks.astype(BF16)
        dckv, dkg = _rms_bwd(ckn, ckr, kg, _dot_nt(dkva_ref[...], wkv_ref[...]))
        dckv_ref[...] = dckv
        _accumulate(dqg_ref, dqg, first)
        _accumulate(dkg_ref, dkg, first)

    heads = pl.BlockSpec((H, ROW_TILE, LANES), lambda i: (0, i, 0))
    return pl.pallas_call(
        body, name="mla_proj_bwd",
        out_shape=(jax.ShapeDtypeStruct((Tp, QL), F32), jax.ShapeDtypeStruct((Tp, KL), F32),
                   jax.ShapeDtypeStruct((Tp, LANES), F32), jax.ShapeDtypeStruct((Tp, H * LANES), BF16),
                   jax.ShapeDtypeStruct((Tp, H * LANES), BF16),
                   jax.ShapeDtypeStruct((Tp, QL), BF16), jax.ShapeDtypeStruct((Tp, KL), BF16),
                   jax.ShapeDtypeStruct((1, QL), F32), jax.ShapeDtypeStruct((1, KL), F32)),
        grid=(Tp // ROW_TILE,),
        in_specs=[heads, heads, _rows(H * half), _rows(QL), _rows(KL), _const((1, QL)), _const((1, KL)),
                  _resident(wuq.shape), _resident(wkv.shape), _rows(LANES), _rows(LANES), _rows(LANES)],
        out_specs=[_rows(QL), _rows(KL), _rows(LANES), _rows(H * LANES), _rows(H * LANES),
                   _rows(QL), _rows(KL), _const((1, QL)), _const((1, KL))],
        compiler_params=_cparams(),
    )(dq, dk, dv, cq, ckv, qg, kg, wuq, wkv, cosm, sin_hi, sin_lo)


_NEG = -1e30
_ATT_SCALE = (QK_NOPE + QK_ROPE) ** -0.5
_ATT_LOG2 = _ATT_SCALE * math.log2(math.e)


def _causal(s, q0, k0, transposed):
    r = lax.broadcasted_iota(jnp.int32, s.shape, 0)
    c = lax.broadcasted_iota(jnp.int32, s.shape, 1)
    keep = (k0 + r <= q0 + c) if transposed else (k0 + c <= q0 + r)
    return jnp.where(keep, s, _NEG)


def _hosted(comm, n_in, n_out, refs, first, last):
    if comm is None:
        return refs, lambda: None
    n = comm.n
    own = refs[:n_in] + refs[n_in + n:n_in + n + n_out]
    comm.bind(refs[n_in:n_in + n], refs[n_in + n + n_out:n_in + 2 * n + n_out], refs[n_in + 2 * n + n_out:])
    pl.when(first)(comm.start)
    return own, lambda: pl.when(last)(comm.finish)


def _attn_fwd(q, k, v, comm=None, comm_arrays=()):
    H, Tp, _ = q.shape
    t = ROW_TILE
    half = LANES // 2
    n_extra = 0 if comm is None else comm.n

    def body(*refs):
        p_id, i = pl.program_id(0), pl.program_id(1)
        first = jnp.logical_and(p_id == 0, i == 0)
        last = jnp.logical_and(p_id == H // 2 - 1, i == Tp // t - 1)
        (q_ref, k_ref, v_ref, y_ref, lse_ref), finish = _hosted(comm, 3, 2, refs, first, last)
        qs = [q_ref[0], q_ref[1]]

        def step(j, carry, masked):
            vv = v_ref[pl.ds(j * t, t), :]
            out = []
            for hh in range(2):
                m, l, acc = carry[hh]
                kv = k_ref[hh, pl.ds(j * t, t), :]
                s = _dot_nt(qs[hh], kv) * _ATT_LOG2
                if masked:
                    s = _causal(s, i * t, j * t, False)
                m_new = jnp.maximum(m, jnp.max(s, axis=-1, keepdims=True))
                alpha = jnp.exp2(m - m_new)
                p = jnp.exp2(s - m_new)
                l = alpha * l + jnp.sum(p, axis=-1, keepdims=True)
                acc = alpha * acc + _dot(p.astype(BF16), vv)
                out.append((m_new, l, acc))
            return tuple(out)

        one = (jnp.full((t, 1), _NEG, F32), jnp.zeros((t, 1), F32), jnp.zeros((t, LANES), F32))
        carry = lax.fori_loop(0, i, functools.partial(step, masked=False), (one, one))
        carry = step(i, carry, True)
        outs = []
        for hh in range(2):
            m, l, acc = carry[hh]
            outs.append(acc / l)
            lse_ref[hh] = m + jnp.log2(l)
        lane = lax.broadcasted_iota(jnp.int32, (t, LANES), 1)
        y_ref[...] = jnp.where(lane < half, outs[0], outs[1])
        finish()

    return pl.pallas_call(
        body, name="attn_fwd" if comm is None else "attn_fwd_gather",
        out_shape=(jax.ShapeDtypeStruct((Tp, H * half), F32), jax.ShapeDtypeStruct((H, Tp, 1), F32))
        + (() if comm is None else comm.out_shape),
        grid=(H // 2, Tp // t),
        in_specs=[pl.BlockSpec((2, t, LANES), lambda p, i: (p, i, 0)),
                  pl.BlockSpec((2, Tp, LANES), lambda p, i: (p, 0, 0)),
                  pl.BlockSpec((Tp, LANES), lambda p, i: (0, p))] + [_HBM] * n_extra,
        out_specs=[pl.BlockSpec((t, LANES), lambda p, i: (i, p)),
                   pl.BlockSpec((2, t, 1), lambda p, i: (p, i, 0))] + [_HBM] * n_extra,
        scratch_shapes=[] if comm is None else comm.scratch,
        compiler_params=_cparams(),
    )(q, k, v, *comm_arrays)


def _attn_bwd_q(q, k, v, dy, y, lse):
    H, Tp, _ = q.shape
    t = ROW_TILE
    half = LANES // 2

    def body(q_ref, k_ref, v_ref, dy_ref, y_ref, lse_ref, dq_ref, delta_ref):
        i = pl.program_id(1)
        lane = lax.broadcasted_iota(jnp.int32, (t, LANES), 1)
        qs, dobs, deltas, lses = [], [], [], []
        for hh in range(2):
            mine = (lane < half) if hh == 0 else (lane >= half)
            do = jnp.where(mine, dy_ref[...], 0.0)
            delta = jnp.sum(do * y_ref[...], axis=-1, keepdims=True)
            delta_ref[hh] = delta
            qs.append(q_ref[hh])
            dobs.append(do.astype(BF16))
            deltas.append(delta)
            lses.append(lse_ref[hh])

        def step(j, dqs, masked):
            vv = v_ref[pl.ds(j * t, t), :]
            out = []
            for hh in range(2):
                kv = k_ref[hh, pl.ds(j * t, t), :]
                s = _dot_nt(qs[hh], kv) * _ATT_LOG2
                if masked:
                    s = _causal(s, i * t, j * t, False)
                p = jnp.exp2(s - lses[hh])
                dp = _dot_nt(dobs[hh], vv)
                ds = p * (dp - deltas[hh])
                out.append(dqs[hh] + _dot(ds.astype(BF16), kv))
            return tuple(out)

        zero = jnp.zeros((t, LANES), F32)
        dqs = lax.fori_loop(0, i, functools.partial(step, masked=False), (zero, zero))
        dqs = step(i, dqs, True)
        for hh in range(2):
            dq_ref[hh] = dqs[hh] * _ATT_SCALE

    return pl.pallas_call(
        body, name="attn_bwd_q",
        out_shape=(jax.ShapeDtypeStruct((H, Tp, LANES), F32), jax.ShapeDtypeStruct((H, Tp, 1), F32)),
        grid=(H // 2, Tp // t),
        in_specs=[pl.BlockSpec((2, t, LANES), lambda p, i: (p, i, 0)),
                  pl.BlockSpec((2, Tp, LANES), lambda p, i: (p, 0, 0)),
                  pl.BlockSpec((Tp, LANES), lambda p, i: (0, p)),
                  pl.BlockSpec((t, LANES), lambda p, i: (i, p)),
                  pl.BlockSpec((t, LANES), lambda p, i: (i, p)),
                  pl.BlockSpec((2, t, 1), lambda p, i: (p, i, 0))],
        out_specs=[pl.BlockSpec((2, t, LANES), lambda p, i: (p, i, 0)),
                   pl.BlockSpec((2, t, 1), lambda p, i: (p, i, 0))],
        compiler_params=_cparams(),
    )(q, k, v, dy, y, lse)


def _attn_bwd_kv(q, k, v, dy, lse_row, delta_row, comm=None, comm_arrays=()):
    H, Tp, _ = q.shape
    t = ROW_TILE
    half = LANES // 2
    nq = Tp // t
    n_extra = 0 if comm is None else comm.n

    def body(*refs):
        p_id, j = pl.program_id(0), pl.program_id(1)
        first = jnp.logical_and(p_id == 0, j == 0)
        last = jnp.logical_and(p_id == H // 2 - 1, j == nq - 1)
        (q_ref, k_ref, v_ref, dy_ref, lse_ref, delta_ref, dk_ref, dv_ref), finish = _hosted(comm, 6, 2, refs, first, last)
        lane = lax.broadcasted_iota(jnp.int32, (t, LANES), 1)
        vv = v_ref[...]
        ks = [k_ref[0], k_ref[1]]

        def step(i, carry, masked):
            cols = pl.ds(pl.multiple_of(i * t, LANES), t)
            dyv = dy_ref[pl.ds(i * t, t), :]
            out = []
            for hh in range(2):
                dk, dv = carry[hh]
                mine = (lane < half) if hh == 0 else (lane >= half)
                qv = q_ref[hh, pl.ds(i * t, t), :]
                dob = jnp.where(mine, dyv, 0.0).astype(BF16)
                st = _dot_nt(ks[hh], qv) * _ATT_LOG2
                if masked:
                    st = _causal(st, i * t, j * t, True)
                pt = jnp.exp2(st - lse_ref[hh, :, cols])
                dv = dv + _dot(pt.astype(BF16), dob)
                dpt = _dot_nt(vv, dob)
                dst = pt * (dpt - delta_ref[hh, :, cols])
                dk = dk + _dot(dst.astype(BF16), qv)
                out.append((dk, dv))
            return tuple(out)

        zero = jnp.zeros((t, LANES), F32)
        carry = step(j, ((zero, zero), (zero, zero)), True)
        carry = lax.fori_loop(j + 1, nq, functools.partial(step, masked=False), carry)
        for hh in range(2):
            dk_ref[hh] = carry[hh][0] * _ATT_SCALE
        dv_ref[...] = carry[0][1] + carry[1][1]
        finish()

    return pl.pallas_call(
        body, name="attn_bwd_kv" if comm is None else "attn_bwd_kv_exchange",
        out_shape=(jax.ShapeDtypeStruct((H, Tp, LANES), F32), jax.ShapeDtypeStruct((Tp, H * half), F32))
        + (() if comm is None else comm.out_shape),
        grid=(H // 2, nq),
        in_specs=[pl.BlockSpec((2, Tp, LANES), lambda p, j: (p, 0, 0)),
                  pl.BlockSpec((2, t, LANES), lambda p, j: (p, j, 0)),
                  pl.BlockSpec((t, LANES), lambda p, j: (j, p)),
                  pl.BlockSpec((Tp, LANES), lambda p, j: (0, p)),
                  pl.BlockSpec((2, 1, Tp), lambda p, j: (p, 0, 0)),
                  pl.BlockSpec((2, 1, Tp), lambda p, j: (p, 0, 0))] + [_HBM] * n_extra,
        out_specs=[pl.BlockSpec((2, t, LANES), lambda p, j: (p, j, 0)),
                   pl.BlockSpec((t, LANES), lambda p, j: (j, p))] + [_HBM] * n_extra,
        scratch_shapes=[] if comm is None else comm.scratch,
        compiler_params=_cparams(),
    )(q, k, v, dy, lse_row, delta_row, *comm_arrays)


def _loss_head(h, target, lo, hi):
    Tp, D = h.shape

    def body(h_ref, t_ref, dh_ref, loss_ref):
        i = pl.program_id(0)
        row = i * ROW_TILE + lax.broadcasted_iota(jnp.int32, (ROW_TILE, 1), 0)
        live = jnp.logical_and(row >= lo, row < hi)
        err = jnp.where(live, h_ref[...] - t_ref[...], 0.0)
        dh_ref[...] = err / D
        part = 0.5 * jnp.sum(jnp.mean(err * err, axis=-1, keepdims=True), axis=0, keepdims=True)
        _accumulate(loss_ref, jnp.broadcast_to(part, (8, LANES)), i == 0)

    return pl.pallas_call(
        body, name="loss_head",
        out_shape=(jax.ShapeDtypeStruct((Tp, D), F32), jax.ShapeDtypeStruct((8, LANES), F32)),
        grid=(Tp // ROW_TILE,),
        in_specs=[_rows(D), _rows(D)],
        out_specs=[_rows(D), _const((8, LANES))],
        compiler_params=_cparams(),
    )(h, target)


def _pack(arrays, row_multiple=8):
    flat = jnp.concatenate([a.reshape(-1).astype(F32) for a in arrays])
    quantum = LANES * row_multiple
    total = -(-flat.shape[0] // quantum) * quantum
    flat = jnp.pad(flat, (0, total - flat.shape[0]))
    return flat.reshape(-1, LANES)


def _unpack(buf, shapes):
    flat = buf.reshape(-1)
    out, off = [], 0
    for shp in shapes:
        n = math.prod(shp)
        out.append(flat[off:off + n].reshape(tuple(shp)))
        off += n
    return out


def _block_diag_blocks(w):
    nh, d, _ = w.shape
    per = LANES // d
    eye = jnp.eye(per, dtype=w.dtype)
    g = w.reshape(nh // per, per, d, d)
    return jnp.einsum('bpij,pq->bpiqj', g, eye).reshape(nh // per, LANES, LANES)


def _block_diag_extract(blocks, d):
    nb = blocks.shape[0]
    per = LANES // d
    return jnp.stack([blocks[b, p * d:(p + 1) * d, p * d:(p + 1) * d] for b in range(nb) for p in range(per)])


def _step(a):
    x = a['x'][0]
    target = a['loss_target'][0]
    seq, D = x.shape
    n_meta = a['meta_tokens'].shape[0]
    T = seq + n_meta
    Tp = -(-T // ROW_TILE) * ROW_TILE
    L = a['ffn1_pre_g'].shape[0]
    W = a['lru_conv_b'].shape[1]
    QL = a['mla_q_norm_g'].shape[1]
    KL = a['mla_kv_norm_g'].shape[1]
    H = MLA_HEADS
    LD = a['lru_w_a'].shape[2]
    n_in = 2 * W + QL + KL
    assert W % LANES == 0 and QL % LANES == 0 and KL % LANES == 0 and LANES % LD == 0
    assert a['mla_w_ukv'].shape[2] == LANES and a['mla_w_uq'].shape[2] == QK_NOPE + QK_ROPE

    ax, ay, ac = (lax.axis_index(n) for n in MESH_AXES)
    me = 4 * ax + 2 * ay + ac
    core = ac.reshape(1).astype(jnp.int32)
    chip = (2 * ax + ay).reshape(1).astype(jnp.int32)

    kinds = {'ffn1_w_gate': 'cols', 'ffn1_w_up': 'cols', 'ffn1_w_down': 'rows', 'w_in': 'w_in', 'mla_w_uq': 'heads',
             'mla_w_ukv': 'cols', 'w_out': 'rows', 'ffn2_w_gate': 'cols', 'ffn2_w_up': 'cols', 'ffn2_w_down': 'rows'}
    layer_shards = lambda l: [a[n][l].astype(BF16) for n in BIG]
    assemble = lambda gathered: {n: _assemble(g, kinds[n], n_in) for n, g in zip(BIG, gathered)}
    first = _all_gather(layer_shards(0) + [a[n] for n in SMALL_SHARDED])
    big = [assemble(first[:len(BIG)])]
    gcw, gmeta = first[len(BIG):]
    conv_w = jnp.moveaxis(gcw, 0, 2).reshape(L, CONV_WIDTH, W)
    meta = jnp.moveaxis(gmeta, 0, 1).reshape(n_meta, D)
    wa_blk = jax.vmap(_block_diag_blocks)(a['lru_w_a']).astype(BF16)
    wx_blk = jax.vmap(_block_diag_blocks)(a['lru_w_x']).astype(BF16)
    widths = [W, W, QL, KL, LANES]

    QH = QK_NOPE + QK_ROPE
    pos = jnp.arange(Tp, dtype=F32)
    inv_freq = 1.0 / (ROPE_THETA ** (jnp.arange(0, QK_ROPE, 2, dtype=F32) / QK_ROPE))
    ang = pos[:, None] * inv_freq[None, :]
    cos, sin = jnp.cos(ang), jnp.sin(ang)
    hr = QK_ROPE // 2
    z = lambda n: jnp.zeros((Tp, n), F32)
    cosm = jnp.concatenate([jnp.ones((Tp, QK_NOPE), F32), cos, cos, z(LANES - QH)], axis=1)
    sin_hi = jnp.concatenate([z(QK_NOPE + hr), sin, z(LANES - QH)], axis=1)
    sin_lo = jnp.concatenate([z(QK_NOPE), -sin, z(LANES - QK_NOPE - hr)], axis=1)
    tabs = (cosm, sin_hi, sin_lo)

    vec = lambda name, l: a[name][l][None, :]

    h = jnp.concatenate([meta, x, jnp.zeros((Tp - T, D), F32)], axis=0)
    tpad = jnp.concatenate([jnp.zeros((n_meta, D), F32), target, jnp.zeros((Tp - T, D), F32)], axis=0)
    saved = []
    for l in range(L):
        wl = big[l]
        s = {'h0': h}
        h, s['a1'], s['b1'], s['f1'] = _ffn_fwd(h, vec('ffn1_pre_g', l), vec('ffn1_post_g', l),
                                                wl['ffn1_w_gate'], wl['ffn1_w_up'], wl['ffn1_w_down'])
        s['h1'] = h
        xr, gr, cq, ckv, kr = _mix_in_fwd(h, vec('mix_pre_g', l), wl['w_in'], widths)
        s.update(xr=xr, gr=gr, cq=cq, ckv=ckv)
        y_lru = _rglru_fwd(xr, gr, conv_w[l], vec('lru_conv_b', l), wa_blk[l], wx_blk[l], vec('lru_b_a', l),
                           vec('lru_b_x', l), vec('lru_lambda', l))
        q, k, v = _mla_proj_fwd(cq, ckv, kr, vec('mla_q_norm_g', l), vec('mla_kv_norm_g', l),
                                wl['mla_w_uq'], wl['mla_w_ukv'], *tabs)
        if l + 1 < L:
            nxt = layer_shards(l + 1)
            y_mla, lse, *gathered = _attn_fwd(q, k, v, _Gather(nxt), nxt)
            big.append(assemble(gathered))
        else:
            y_mla, lse = _attn_fwd(q, k, v)
        s.update(q=q, k=k, v=v, lse=lse, y_lru=y_lru, y_mla=y_mla)
        h, s['y'] = _mix_out_fwd(h, y_lru, y_mla, vec('lru_out_g', l), vec('mla_out_g', l), vec('mix_post_g', l),
                                 wl['w_out'])
        s['h2'] = h
        h, s['a2'], s['b2'], s['f2'] = _ffn_fwd(h, vec('ffn2_pre_g', l), vec('ffn2_post_g', l),
                                                wl['ffn2_w_gate'], wl['ffn2_w_up'], wl['ffn2_w_down'])
        saved.append(s)

    dh, loss_tile = _loss_head(h, tpad, n_meta, T)
    loss = lax.psum(loss_tile[0, 0], MESH_AXES)

    gw = {n: [None] * L for n in REPLICATED + ['lru_conv_w']}
    shard2d = {n: a[n].shape[1:] for n in BIG}
    results = {n: [lax.empty(a[n].shape, F32) for _ in range(4)] for n in BIG}

    def ffn_bwd(f, l, gbuf, dh, h_in, a_, b_, f_):
        wl = big[l]
        da, db, act, dfb, dpost = _ffn_bwd_down(dh, a_, b_, f_, vec(f + '_post_g', l), wl[f + '_w_down'])
        dh_in, u, dpre = _ffn_bwd_up(dh, h_in, da, db, vec(f + '_pre_g', l), wl[f + '_w_gate'], wl[f + '_w_up'])
        gbuf[f + '_w_gate'] = _wgrad(u, da, shard2d[f + '_w_gate'], 'cols')
        gbuf[f + '_w_up'] = _wgrad(u, db, shard2d[f + '_w_up'], 'cols')
        gbuf[f + '_w_down'] = _wgrad(act, dfb, shard2d[f + '_w_down'], 'rows')
        gw[f + '_pre_g'][l] = dpre[0]
        gw[f + '_post_g'][l] = dpost[0]
        return dh_in

    def finish_reduce(l, chip_partial, got2):
        for n, cp, g2 in zip(BIG, chip_partial, got2):
            results[n] = _final_sum_adamw(cp, g2, chip, a[n], a['m_' + n], a['v_' + n], results[n], l)

    pending = None
    for l in reversed(range(L)):
        s = saved[l]
        wl = big[l]
        gbuf = {}
        dh = ffn_bwd('ffn2', l, gbuf, dh, s['h2'], s['a2'], s['b2'], s['f2'])
        dyl, dym, dyb, cat, dlg, dmg, dpg = _mix_out_bwd(
            dh, s['y'], s['y_lru'], s['y_mla'], vec('lru_out_g', l), vec('mla_out_g', l), vec('mix_post_g', l),
            wl['w_out'])
        gbuf['w_out'] = _wgrad(cat, dyb, shard2d['w_out'], 'rows')
        gw['lru_out_g'][l], gw['mla_out_g'][l], gw['mix_post_g'][l] = dlg[0], dmg[0], dpg[0]
        dxr, dgr, dcw, dcb, dwa, dwx, dba, dbx, dlam = _rglru_bwd(
            dyl, s['xr'], s['gr'], conv_w[l], vec('lru_conv_b', l), wa_blk[l], wx_blk[l],
            vec('lru_b_a', l), vec('lru_b_x', l), vec('lru_lambda', l))
        gw['lru_conv_w'][l], gw['lru_conv_b'][l] = dcw, dcb[0]
        gw['lru_w_a'][l], gw['lru_w_x'][l] = _block_diag_extract(dwa, LD), _block_diag_extract(dwx, LD)
        gw['lru_b_a'][l], gw['lru_b_x'][l], gw['lru_lambda'][l] = dba[0], dbx[0], dlam[0]
        dq, delta = _attn_bwd_q(s['q'], s['k'], s['v'], dym, s['y_mla'], s['lse'])
        lse_row, delta_row = s['lse'].reshape(H, 1, Tp), delta.reshape(H, 1, Tp)
        if pending is None:
            dk, dv = _attn_bwd_kv(s['q'], s['k'], s['v'], dym, lse_row, delta_row)
        else:
            dk, dv, *got2 = _attn_bwd_kv(s['q'], s['k'], s['v'], dym, lse_row, delta_row,
                                         _Exchange(pending[1]), pending[1])
            finish_reduce(pending[0], pending[1], got2)
        dcq, dckv, dkr, dqa, dkva, cqn, ckvn, dqg, dkg = _mla_proj_bwd(
            dq, dk, dv, s['cq'], s['ckv'], vec('mla_q_norm_g', l), vec('mla_kv_norm_g', l),
            wl['mla_w_uq'], wl['mla_w_ukv'], *tabs)
        gw['mla_q_norm_g'][l], gw['mla_kv_norm_g'][l] = dqg[0], dkg[0]
        gbuf['mla_w_uq'] = _wgrad(cqn, dqa, shard2d['mla_w_uq'], 'heads')
        gbuf['mla_w_ukv'] = _wgrad(ckvn, dkva, shard2d['mla_w_ukv'], 'cols')
        dh, dz, hn, dmg_ = _mix_in_bwd(dh, s['h1'], [dxr, dgr, dcq, dckv, dkr], vec('mix_pre_g', l), wl['w_in'])
        gw['mix_pre_g'][l] = dmg_[0]
        gbuf['w_in'] = _wgrad(hn, dz, shard2d['w_in'], 'w_in', n_in)
        dh = ffn_bwd('ffn1', l, gbuf, dh, s['h0'], s['a1'], s['b1'], s['f1'])
        parts = [gbuf[n] for n in BIG]
        got = _swap_with_sibling(parts)
        pending = (l, [_chip_partial(p, g, core) for p, g in zip(parts, got)])

    finish_reduce(pending[0], pending[1], _exchange_chips(pending[1]))
    grad_x = dh[n_meta:T][None]
    gsmall = {n: jnp.stack(v) for n, v in gw.items()}
    gsmall['meta_tokens'] = dh[:n_meta]
    grads, delta, new_m, new_v = ({n: results[n][i] for n in BIG} for i in range(4))

    small_names = REPLICATED + SMALL_SHARDED
    small_full_shapes = [gsmall[n].shape for n in small_names]
    gsm = _sum_devices(_all_gather([_pack([gsmall[n] for n in small_names])])[0])
    for n, g in zip(small_names, _unpack(gsm, small_full_shapes)):
        if n == 'lru_conv_w':
            g = lax.dynamic_index_in_dim(jnp.moveaxis(g.reshape(L, CONV_WIDTH, N_DEV, W // N_DEV), 2, 0), me, 0, False)
        elif n == 'meta_tokens':
            g = lax.dynamic_index_in_dim(jnp.moveaxis(g.reshape(n_meta, N_DEV, D // N_DEV), 1, 0), me, 0, False)
        grads[n] = g
    for group in (REPLICATED, SMALL_SHARDED):
        shapes = [a[n].shape for n in group]
        packed = [_pack([a[p + n] for n in group]) for p in ('', 'm_', 'v_')]
        packed.append(_pack([grads[n] for n in group]))
        outs = [_unpack(o, shapes) for o in _adamw(*packed)]
        for i, n in enumerate(group):
            delta[n], new_m[n], new_v[n] = outs[0][i], outs[1][i], outs[2][i]

    return (loss, grad_x, *[grads[n] for n in WEIGHTS], *[delta[n] for n in WEIGHTS],
            *[new_m[n] for n in WEIGHTS], *[new_v[n] for n in WEIGHTS])


def kernel(x, meta_tokens, ffn1_pre_g, ffn1_w_gate, ffn1_w_up, ffn1_w_down, ffn1_post_g, mix_pre_g, w_in, lru_conv_w, lru_conv_b, lru_w_a, lru_b_a, lru_w_x, lru_b_x, lru_lambda, mla_q_norm_g, mla_w_uq, mla_kv_norm_g, mla_w_ukv, lru_out_g, mla_out_g, w_out, mix_post_g, ffn2_pre_g, ffn2_w_gate, ffn2_w_up, ffn2_w_down, ffn2_post_g, loss_target, m_meta_tokens, m_ffn1_pre_g, m_ffn1_w_gate, m_ffn1_w_up, m_ffn1_w_down, m_ffn1_post_g, m_mix_pre_g, m_w_in, m_lru_conv_w, m_lru_conv_b, m_lru_w_a, m_lru_b_a, m_lru_w_x, m_lru_b_x, m_lru_lambda, m_mla_q_norm_g, m_mla_w_uq, m_mla_kv_norm_g, m_mla_w_ukv, m_lru_out_g, m_mla_out_g, m_w_out, m_mix_post_g, m_ffn2_pre_g, m_ffn2_w_gate, m_ffn2_w_up, m_ffn2_w_down, m_ffn2_post_g, v_meta_tokens, v_ffn1_pre_g, v_ffn1_w_gate, v_ffn1_w_up, v_ffn1_w_down, v_ffn1_post_g, v_mix_pre_g, v_w_in, v_lru_conv_w, v_lru_conv_b, v_lru_w_a, v_lru_b_a, v_lru_w_x, v_lru_b_x, v_lru_lambda, v_mla_q_norm_g, v_mla_w_uq, v_mla_kv_norm_g, v_mla_w_ukv, v_lru_out_g, v_mla_out_g, v_w_out, v_mix_post_g, v_ffn2_pre_g, v_ffn2_w_gate, v_ffn2_w_up, v_ffn2_w_down, v_ffn2_post_g):
    return _step(dict(locals()))
```

```python
import functools
import math

import jax
import jax.numpy as jnp
from jax import lax
from jax.experimental import pallas as pl
from jax.experimental.pallas import tpu as pltpu

F32 = jnp.float32
BF16 = jnp.bfloat16

EPS = 1e-6
N_DEV = 8
LANES = 128
ROW_TILE = 384
SCAN_CHUNKS = 8
VMEM_LIMIT = 56 * 1024 * 1024
MM_ACC_BYTES = 8 * 1024 * 1024
WGRAD_K_TILE = 1100

LRU_C = 8.0
CONV_WIDTH = 4
MLA_HEADS = 8
QK_NOPE = 64
QK_ROPE = 32
ROPE_THETA = 10000.0

ADAM_LR = 0.001
ADAM_B1 = 0.9
ADAM_B2 = 0.999
ADAM_EPS = 1e-08
ADAM_WD = 0.01
ADAM_STEP = 10

MESH_AXES = ("x", "y", "c")
MESH = pl.DeviceIdType.MESH

WEIGHTS = ['meta_tokens', 'ffn1_pre_g', 'ffn1_w_gate', 'ffn1_w_up', 'ffn1_w_down', 'ffn1_post_g', 'mix_pre_g', 'w_in',
           'lru_conv_w', 'lru_conv_b', 'lru_w_a', 'lru_b_a', 'lru_w_x', 'lru_b_x', 'lru_lambda', 'mla_q_norm_g',
           'mla_w_uq', 'mla_kv_norm_g', 'mla_w_ukv', 'lru_out_g', 'mla_out_g', 'w_out', 'mix_post_g', 'ffn2_pre_g',
           'ffn2_w_gate', 'ffn2_w_up', 'ffn2_w_down', 'ffn2_post_g']
BIG = ['ffn1_w_gate', 'ffn1_w_up', 'ffn1_w_down', 'w_in', 'mla_w_uq', 'mla_w_ukv', 'w_out',
       'ffn2_w_gate', 'ffn2_w_up', 'ffn2_w_down']
SMALL_SHARDED = ['lru_conv_w', 'meta_tokens']
REPLICATED = [n for n in WEIGHTS if n not in BIG and n not in SMALL_SHARDED]


def _cparams(**kw):
    return pltpu.CompilerParams(vmem_limit_bytes=VMEM_LIMIT, **kw)


def _resident(shape):
    nd = len(shape)
    return pl.BlockSpec(shape, lambda *_: (0,) * nd, pipeline_mode=pl.Buffered(1))


def _rows(cols, tm=ROW_TILE):
    return pl.BlockSpec((tm, cols), lambda i: (i, 0))


def _const(shape):
    nd = len(shape)
    return pl.BlockSpec(shape, lambda *_: (0,) * nd)


def _dot(a, b):
    return jnp.dot(a, b, preferred_element_type=F32)


def _dot_nt(a, b):
    return lax.dot_general(a, b, (((1,), (1,)), ((), ())), preferred_element_type=F32)


def _dot_tn(a, b):
    return lax.dot_general(a, b, (((0,), (0,)), ((), ())), preferred_element_type=F32)


def _rms_fwd(x, g):
    rinv = lax.rsqrt(jnp.mean(x * x, axis=-1, keepdims=True) + EPS)
    xn = x * rinv
    return xn * g, xn, rinv


def _rms_bwd(xn, rinv, g, dy):
    dxn = dy * g
    dx = rinv * (dxn - xn * jnp.mean(dxn * xn, axis=-1, keepdims=True))
    return dx, jnp.sum(dy * xn, axis=0, keepdims=True)


def _accumulate(ref, val, first):
    @pl.when(first)
    def _():
        ref[...] = val

    @pl.when(jnp.logical_not(first))
    def _():
        ref[...] += val


class _Gather:
    def __init__(self, shards):
        self.n = len(shards)
        self.out_shape = tuple(jax.ShapeDtypeStruct((N_DEV,) + s.shape, s.dtype) for s in shards)
        self.scratch = [pltpu.SemaphoreType.DMA((7 * self.n,)), pltpu.SemaphoreType.DMA((7 * self.n,)),
                        pltpu.SemaphoreType.DMA((self.n,))]

    def bind(self, x_refs, out_refs, sems):
        self.x_refs, self.out_refs = x_refs, out_refs
        self.send_sems, self.recv_sems, self.local_sems = sems
        x, y, c = lax.axis_index("x"), lax.axis_index("y"), lax.axis_index("c")
        self.c = c
        self.me, self.sibling = (x, y, c), (x, y, 1 - c)
        self.chips = [(1 - x, y), (x, 1 - y), (1 - x, 1 - y)]

    def _copy(self, a, k, block, to, own=False):
        dst = self.out_refs[a].at[4 * block[0] + 2 * block[1] + block[2]]
        return pltpu.make_async_remote_copy(
            src_ref=self.x_refs[a] if own else dst, dst_ref=dst,
            send_sem=self.send_sems.at[7 * a + k], recv_sem=self.recv_sems.at[7 * a + k],
            device_id=to, device_id_type=MESH)

    def _mine(self):
        return [pltpu.make_async_copy(self.x_refs[a], self.out_refs[a].at[4 * self.me[0] + 2 * self.me[1] + self.c],
                                      self.local_sems.at[a]) for a in range(self.n)]

    def _first(self):
        first = []
        for a in range(self.n):
            first.append(self._copy(a, 0, self.me, self.sibling, own=True))
            first += [self._copy(a, 1 + j, self.me, (*chip, self.c), own=True) for j, chip in enumerate(self.chips)]
        return first

    def start(self):
        for cp in self._mine() + self._first():
            cp.start()

    def finish(self):
        c, n = self.c, self.n
        passed = []
        for j, chip in enumerate(self.chips):
            for a in range(n):
                self._copy(a, 1 + j, (*chip, c), self.me).wait_recv()
                cp = self._copy(a, 4 + j, (*chip, c), self.sibling)
                cp.start()
                passed.append(cp)
        for a in range(n):
            self._copy(a, 0, self.sibling, self.me).wait_recv()
        for j, chip in enumerate(self.chips):
            for a in range(n):
                self._copy(a, 4 + j, (*chip, 1 - c), self.me).wait_recv()
        for cp in self._first() + passed:
            cp.wait_send()
        for cp in self._mine():
            cp.wait()


class _Exchange:
    def __init__(self, parts):
        self.n = len(parts)
        self.out_shape = tuple(jax.ShapeDtypeStruct((3,) + p.shape[1:], p.dtype) for p in parts)
        self.scratch = [pltpu.SemaphoreType.DMA((3 * self.n,)), pltpu.SemaphoreType.DMA((3 * self.n,))]

    def bind(self, p_refs, out_refs, sems):
        self.p_refs, self.out_refs = p_refs, out_refs
        self.send_sems, self.recv_sems = sems

    def _copies(self):
        x, y, c = lax.axis_index("x"), lax.axis_index("y"), lax.axis_index("c")
        copies = []
        for a in range(self.n):
            for k, (tx, ty) in enumerate([(1 - x, y), (x, 1 - y), (1 - x, 1 - y)]):
                copies.append(pltpu.make_async_remote_copy(
                    src_ref=self.p_refs[a].at[2 * tx + ty], dst_ref=self.out_refs[a].at[k],
                    send_sem=self.send_sems.at[3 * a + k], recv_sem=self.recv_sems.at[3 * a + k],
                    device_id=(tx, ty, c), device_id_type=MESH))
        return copies

    def start(self):
        for cp in self._copies():
            cp.start()

    def finish(self):
        for cp in self._copies():
            cp.wait()


_HBM = pl.BlockSpec(memory_space=pl.ANY)


def _comm_call(comm, arrays, name):
    n = comm.n

    def body(*refs):
        comm.bind(refs[:n], refs[n:2 * n], refs[2 * n:])
        comm.start()
        comm.finish()

    return pl.pallas_call(
        body, name=name, out_shape=comm.out_shape,
        in_specs=[_HBM] * n, out_specs=[_HBM] * n, scratch_shapes=comm.scratch,
    )(*arrays)


def _all_gather(shards):
    return _comm_call(_Gather(shards), shards, "all_gather")


def _exchange_chips(parts):
    return _comm_call(_Exchange(parts), parts, "rs_chips")


def _swap_with_sibling(parts):
    n = len(parts)

    def body(*refs):
        p_refs, out_refs = refs[:n], refs[n:2 * n]
        send_sems, recv_sems = refs[2 * n:]
        x, y, c = lax.axis_index("x"), lax.axis_index("y"), lax.axis_index("c")
        copies = []
        for a in range(n):
            for k in range(4):
                cp = pltpu.make_async_remote_copy(
                    src_ref=p_refs[a].at[2 * k + (1 - c)], dst_ref=out_refs[a].at[k],
                    send_sem=send_sems.at[4 * a + k], recv_sem=recv_sems.at[4 * a + k],
                    device_id=(x, y, 1 - c), device_id_type=MESH)
                cp.start()
                copies.append(cp)
        for cp in copies:
            cp.wait()

    hbm = pl.BlockSpec(memory_space=pl.ANY)
    return pl.pallas_call(
        body, name="rs_sibling",
        out_shape=tuple(jax.ShapeDtypeStruct((4,) + p.shape[1:], p.dtype) for p in parts),
        in_specs=[hbm] * n, out_specs=[hbm] * n,
        scratch_shapes=[pltpu.SemaphoreType.DMA((4 * n,)), pltpu.SemaphoreType.DMA((4 * n,))],
    )(*parts)


def _row_block(rows, cap=2048):
    best = 8
    for t in range(8, min(rows, cap) + 1, 8):
        if rows % t == 0:
            best = t
    return best


def _elementwise_rows(rows, cols, elems=1 << 18):
    return _row_block(rows, cap=max(8, elems // cols // 8 * 8))


def _chip_partial(parts, got, core):
    _, rows, cols = got.shape
    tr = _elementwise_rows(rows, cols, 1 << 19)
    p4 = parts.reshape(4, 2, rows, cols)

    def body(c_ref, a_ref, b_ref, o_ref):
        o_ref[...] = (a_ref[...].astype(F32) + b_ref[...].astype(F32)).astype(o_ref.dtype)

    return pl.pallas_call(
        body, name="rs_chip_partial",
        out_shape=jax.ShapeDtypeStruct((4, rows, cols), parts.dtype),
        grid_spec=pltpu.PrefetchScalarGridSpec(
            num_scalar_prefetch=1, grid=(4, rows // tr),
            in_specs=[pl.BlockSpec((None, None, tr, cols), lambda k, r, c: (k, c[0], r, 0)),
                      pl.BlockSpec((None, tr, cols), lambda k, r, c: (k, r, 0))],
            out_specs=pl.BlockSpec((None, tr, cols), lambda k, r, c: (k, r, 0))),
        compiler_params=_cparams(),
    )(core, p4, got)


def _adamw_math(w, m, v, g):
    mm = ADAM_B1 * m + (1.0 - ADAM_B1) * g
    vv = ADAM_B2 * v + (1.0 - ADAM_B2) * jnp.square(g)
    m_hat = mm / (1.0 - ADAM_B1 ** ADAM_STEP)
    v_hat = vv / (1.0 - ADAM_B2 ** ADAM_STEP)
    return -ADAM_LR * (m_hat / (jnp.sqrt(v_hat) + ADAM_EPS) + ADAM_WD * w), mm, vv


def _final_sum_adamw(partial, got, chip, w, m, v, bufs, layer):
    _, rows, cols = got.shape
    tr = _elementwise_rows(rows, cols)

    def body(c_ref, a_ref, b_ref, w_ref, m_ref, v_ref, *rest):
        g_ref, d_ref, nm_ref, nv_ref = rest[4:]
        g = ((a_ref[...].astype(F32) + b_ref[0].astype(F32)) + b_ref[1].astype(F32)) + b_ref[2].astype(F32)
        g_ref[...] = g
        d_ref[...], nm_ref[...], nv_ref[...] = _adamw_math(w_ref[...], m_ref[...], v_ref[...], g)

    one_layer = pl.BlockSpec((None, tr, cols), lambda r, c: (layer, r, 0))
    return pl.pallas_call(
        body, name="rs_final_adamw",
        out_shape=tuple(jax.ShapeDtypeStruct(b.shape, F32) for b in bufs),
        grid_spec=pltpu.PrefetchScalarGridSpec(
            num_scalar_prefetch=1, grid=(rows // tr,),
            in_specs=[pl.BlockSpec((None, tr, cols), lambda r, c: (c[0], r, 0)),
                      pl.BlockSpec((3, tr, cols), lambda r, c: (0, r, 0)), one_layer, one_layer, one_layer] + [_HBM] * 4,
            out_specs=[one_layer] * 4),
        input_output_aliases={6: 0, 7: 1, 8: 2, 9: 3},
        compiler_params=_cparams(),
    )(chip, partial, got, w, m, v, *bufs)


def _sum_devices(gathered):
    _, rows, lanes = gathered.shape
    tr = _row_block(rows)

    def body(g_ref, o_ref):
        acc = g_ref[0]
        for j in range(1, N_DEV):
            acc = acc + g_ref[j]
        o_ref[...] = acc

    return pl.pallas_call(
        body, name="small_grad_sum",
        out_shape=jax.ShapeDtypeStruct((rows, lanes), F32),
        grid=(rows // tr,),
        in_specs=[pl.BlockSpec((N_DEV, tr, lanes), lambda r: (0, r, 0))],
        out_specs=pl.BlockSpec((tr, lanes), lambda r: (r, 0)),
        compiler_params=_cparams(),
    )(gathered)


def _assemble(g, kind, n_in=0):
    _, A, B = g.shape
    if kind == 'cols':
        oshape = (A, N_DEV * B)
    elif kind == 'rows':
        oshape = (N_DEV * A, B)
    elif kind == 'w_in':
        oshape = (A, n_in + LANES)
    else:
        oshape = (A, N_DEV * LANES)

    def body(x_ref, o_ref):
        if kind == 'cols':
            for j in range(N_DEV):
                o_ref[:, j * B:(j + 1) * B] = x_ref[j]
        elif kind == 'rows':
            for j in range(N_DEV):
                o_ref[j * A:(j + 1) * A, :] = x_ref[j]
        elif kind == 'w_in':
            nat = jnp.concatenate([x_ref[j] for j in range(N_DEV)], axis=1)
            o_ref[:, 0:n_in] = nat[:, 0:n_in]
            o_ref[:, n_in:] = jnp.concatenate(
                [jnp.zeros((A, QK_NOPE), g.dtype), nat[:, n_in:],
                 jnp.zeros((A, LANES - QK_NOPE - QK_ROPE), g.dtype)], axis=1)
        else:
            o_ref[...] = jnp.zeros(oshape, g.dtype)
            for j in range(N_DEV):
                o_ref[:, j * LANES:j * LANES + B] = x_ref[j]

    return pl.pallas_call(
        body, name="assemble_" + kind,
        out_shape=jax.ShapeDtypeStruct(oshape, g.dtype),
        compiler_params=_cparams(),
    )(g)


def _wgrad(a, b, shard, kind, n_in=0):
    K, M = a.shape
    N = b.shape[1]
    shard = tuple(shard)
    tk = ROW_TILE
    for cand in range(ROW_TILE, WGRAD_K_TILE + 1, 16):
        if K % cand == 0:
            tk = cand
    nk = K // tk
    if kind == 'cols':
        S = shard[1]
        tn = S
        for t in range(S, N + 1, S):
            if N % t == 0 and t % LANES == 0 and M * t * 4 <= MM_ACC_BYTES:
                tn = t
        oblock = pl.BlockSpec((tn // S,) + shard, lambda n, k: (n, 0, 0))
    elif kind == 'rows':
        tn = LANES
        for t in range(LANES, N + 1, LANES):
            if N % t == 0 and M * t * 4 <= MM_ACC_BYTES:
                tn = t
        oblock = pl.BlockSpec((N_DEV, shard[0], tn), lambda n, k: (0, 0, n))
    else:
        tn = N
        oblock = pl.BlockSpec((N_DEV,) + shard, lambda n, k: (0, 0, 0))

    def body(a_ref, b_ref, o_ref, acc_ref):
        k = pl.program_id(1)

        @pl.when(k == 0)
        def _():
            acc_ref[...] = jnp.zeros_like(acc_ref)

        acc_ref[...] += _dot_tn(a_ref[...], b_ref[...])

        @pl.when(k == nk - 1)
        def _():
            acc = acc_ref[...]
            if kind == 'cols':
                S = shard[1]
                for j in range(tn // S):
                    o_ref[j] = acc[:, j * S:(j + 1) * S].astype(o_ref.dtype)
            elif kind == 'rows':
                S = shard[0]
                for j in range(N_DEV):
                    o_ref[j] = acc[j * S:(j + 1) * S, :].astype(o_ref.dtype)
            elif kind == 'w_in':
                S = shard[1]
                nat = jnp.concatenate([acc[:, 0:n_in], acc[:, n_in + QK_NOPE:n_in + QK_NOPE + QK_ROPE]], axis=1)
                for j in range(N_DEV):
                    o_ref[j] = nat[:, j * S:(j + 1) * S].astype(o_ref.dtype)
            else:
                S = shard[1]
                for j in range(N_DEV):
                    o_ref[j] = acc[:, j * LANES:j * LANES + S].astype(o_ref.dtype)

    return pl.pallas_call(
        body, name="wgrad_" + kind,
        out_shape=jax.ShapeDtypeStruct((N_DEV,) + shard, BF16),
        grid=(N // tn, nk),
        in_specs=[pl.BlockSpec((tk, M), lambda n, k: (k, 0)), pl.BlockSpec((tk, tn), lambda n, k: (k, n))],
        out_specs=oblock,
        scratch_shapes=[pltpu.VMEM((M, tn), F32)],
        compiler_params=_cparams(),
    )(a, b)


def _adamw(w, m, v, g):
    rows, cols = w.shape
    tr = _elementwise_rows(rows, cols) if rows % 8 == 0 else rows

    def body(w_ref, m_ref, v_ref, g_ref, d_ref, nm_ref, nv_ref):
        d_ref[...], nm_ref[...], nv_ref[...] = _adamw_math(w_ref[...], m_ref[...], v_ref[...], g_ref[...])

    spec = pl.BlockSpec((tr, cols), lambda r: (r, 0))
    return pl.pallas_call(
        body, name="adamw",
        out_shape=(jax.ShapeDtypeStruct((rows, cols), F32),) * 3,
        grid=(rows // tr,),
        in_specs=[spec] * 4, out_specs=[spec] * 3,
        compiler_params=_cparams(),
    )(w, m, v, g)


def _hidden_chunk(F):
    return F // 2 if (F // 2) % LANES == 0 else F


def _ffn_fwd(h, pre_g, post_g, wg, wu, wd):
    Tp, D = h.shape
    F = wg.shape[1]
    fc = _hidden_chunk(F)

    def body(h_ref, pg_ref, qg_ref, wg_ref, wu_ref, wd_ref, ho_ref, a_ref, b_ref, f_ref):
        hx = h_ref[...]
        u, _, _ = _rms_fwd(hx, pg_ref[...])
        ub = u.astype(BF16)
        f = jnp.zeros((ROW_TILE, D), F32)
        for c in range(F // fc):
            sl = slice(c * fc, (c + 1) * fc)
            a = _dot(ub, wg_ref[:, sl])
            b = _dot(ub, wu_ref[:, sl])
            a_ref[:, sl] = a.astype(BF16)
            b_ref[:, sl] = b.astype(BF16)
            act = (a * jax.nn.sigmoid(a) * b).astype(BF16)
            f = f + _dot(act, wd_ref[sl, :])
        f_ref[...] = f
        y, _, _ = _rms_fwd(f, qg_ref[...])
        ho_ref[...] = hx + 0.5 * y

    return pl.pallas_call(
        body, name="ffn_fwd",
        out_shape=(jax.ShapeDtypeStruct((Tp, D), F32), jax.ShapeDtypeStruct((Tp, F), BF16),
                   jax.ShapeDtypeStruct((Tp, F), BF16), jax.ShapeDtypeStruct((Tp, D), F32)),
        grid=(Tp // ROW_TILE,),
        in_specs=[_rows(D), _const((1, D)), _const((1, D)), _resident((D, F)), _resident((D, F)), _resident((F, D))],
        out_specs=[_rows(D), _rows(F), _rows(F), _rows(D)],
        compiler_params=_cparams(),
    )(h, pre_g, post_g, wg, wu, wd)


def _ffn_bwd_down(dh, a, b, f, post_g, wd):
    Tp, D = dh.shape
    F = a.shape[1]
    fc = _hidden_chunk(F)

    def body(dh_ref, a_ref, b_ref, f_ref, qg_ref, wd_ref, da_ref, db_ref, act_ref, df_ref, dqg_ref):
        qg = qg_ref[...]
        _, fn, frinv = _rms_fwd(f_ref[...], qg)
        df, dqg = _rms_bwd(fn, frinv, qg, 0.5 * dh_ref[...])
        dfb = df.astype(BF16)
        df_ref[...] = dfb
        for c in range(F // fc):
            sl = slice(c * fc, (c + 1) * fc)
            dact = _dot_nt(dfb, wd_ref[sl, :])
            av = a_ref[:, sl].astype(F32)
            bv = b_ref[:, sl].astype(F32)
            sig = jax.nn.sigmoid(av)
            silu = av * sig
            act_ref[:, sl] = (silu * bv).astype(BF16)
            da_ref[:, sl] = (dact * bv * (sig * (1.0 + av * (1.0 - sig)))).astype(BF16)
            db_ref[:, sl] = (dact * silu).astype(BF16)
        _accumulate(dqg_ref, dqg, pl.program_id(0) == 0)

    return pl.pallas_call(
        body, name="ffn_bwd_down",
        out_shape=(jax.ShapeDtypeStruct((Tp, F), BF16), jax.ShapeDtypeStruct((Tp, F), BF16),
                   jax.ShapeDtypeStruct((Tp, F), BF16), jax.ShapeDtypeStruct((Tp, D), BF16),
                   jax.ShapeDtypeStruct((1, D), F32)),
        grid=(Tp // ROW_TILE,),
        in_specs=[_rows(D), _rows(F), _rows(F), _rows(D), _const((1, D)), _resident((F, D))],
        out_specs=[_rows(F), _rows(F), _rows(F), _rows(D), _const((1, D))],
        compiler_params=_cparams(),
    )(dh, a, b, f, post_g, wd)


def _ffn_bwd_up(dh, h, da, db, pre_g, wg, wu):
    Tp, D = dh.shape
    F = da.shape[1]

    def body(dh_ref, h_ref, da_ref, db_ref, pg_ref, wg_ref, wu_ref, dhi_ref, u_ref, dpg_ref):
        pg = pg_ref[...]
        u, hn, hrinv = _rms_fwd(h_ref[...], pg)
        u_ref[...] = u.astype(BF16)
        du = _dot_nt(da_ref[...], wg_ref[...]) + _dot_nt(db_ref[...], wu_ref[...])
        dx, dpg = _rms_bwd(hn, hrinv, pg, du)
        dhi_ref[...] = dh_ref[...] + dx
        _accumulate(dpg_ref, dpg, pl.program_id(0) == 0)

    return pl.pallas_call(
        body, name="ffn_bwd_up",
        out_shape=(jax.ShapeDtypeStruct((Tp, D), F32), jax.ShapeDtypeStruct((Tp, D), BF16),
                   jax.ShapeDtypeStruct((1, D), F32)),
        grid=(Tp // ROW_TILE,),
        in_specs=[_rows(D), _rows(D), _rows(F), _rows(F), _const((1, D)), _resident((D, F)), _resident((D, F))],
        out_specs=[_rows(D), _rows(D), _const((1, D))],
        compiler_params=_cparams(),
    )(dh, h, da, db, pre_g, wg, wu)


def _mix_in_fwd(h, g, win, widths):
    Tp, D = h.shape
    ncol = win.shape[1]
    offs = [0]
    for wd_ in widths:
        offs.append(offs[-1] + wd_)

    def body(h_ref, g_ref, w_ref, *out_refs):
        y, _, _ = _rms_fwd(h_ref[...], g_ref[...])
        z = _dot(y.astype(BF16), w_ref[...])
        for o_ref, lo, wd_ in zip(out_refs, offs, widths):
            o_ref[...] = z[:, lo:lo + wd_]

    return pl.pallas_call(
        body, name="mix_in_fwd",
        out_shape=tuple(jax.ShapeDtypeStruct((Tp, wd_), F32) for wd_ in widths),
        grid=(Tp // ROW_TILE,),
        in_specs=[_rows(D), _const((1, D)), _resident((D, ncol))],
        out_specs=[_rows(wd_) for wd_ in widths],
        compiler_params=_cparams(),
    )(h, g, win)


def _mix_in_bwd(dres, h, dparts, g, win):
    Tp, D = h.shape
    ncol = win.shape[1]
    widths = [p.shape[1] for p in dparts]
    n = len(dparts)

    def body(*refs):
        dres_ref, h_ref = refs[0], refs[1]
        part_refs = refs[2:2 + n]
        g_ref, w_ref = refs[2 + n], refs[3 + n]
        dh_ref, dz_ref, u_ref, dg_ref = refs[4 + n:]
        dz = jnp.concatenate([r[...] for r in part_refs], axis=1).astype(BF16)
        dz_ref[...] = dz
        gg = g_ref[...]
        u, hn, rinv = _rms_fwd(h_ref[...], gg)
        u_ref[...] = u.astype(BF16)
        dx, dg = _rms_bwd(hn, rinv, gg, _dot_nt(dz, w_ref[...]))
        dh_ref[...] = dres_ref[...] + dx
        _accumulate(dg_ref, dg, pl.program_id(0) == 0)

    return pl.pallas_call(
        body, name="mix_in_bwd",
        out_shape=(jax.ShapeDtypeStruct((Tp, D), F32), jax.ShapeDtypeStruct((Tp, ncol), BF16),
                   jax.ShapeDtypeStruct((Tp, D), BF16), jax.ShapeDtypeStruct((1, D), F32)),
        grid=(Tp // ROW_TILE,),
        in_specs=[_rows(D), _rows(D)] + [_rows(wd_) for wd_ in widths] + [_const((1, D)), _resident((D, ncol))],
        out_specs=[_rows(D), _rows(ncol), _rows(D), _const((1, D))],
        compiler_params=_cparams(),
    )(dres, h, *dparts, g, win)


def _mix_out_fwd(h, y_lru, y_mla, lru_g, mla_g, post_g, wout):
    Tp, D = h.shape
    W = y_lru.shape[1]

    def body(h_ref, yl_ref, ym_ref, lg_ref, mg_ref, pg_ref, w_ref, ho_ref, y_ref):
        yl, _, _ = _rms_fwd(yl_ref[...], lg_ref[...])
        ym, _, _ = _rms_fwd(ym_ref[...], mg_ref[...])
        y = _dot(yl.astype(BF16), w_ref[0:W, :]) + _dot(ym.astype(BF16), w_ref[W:, :])
        y_ref[...] = y
        yn, _, _ = _rms_fwd(y, pg_ref[...])
        ho_ref[...] = h_ref[...] + yn

    return pl.pallas_call(
        body, name="mix_out_fwd",
        out_shape=(jax.ShapeDtypeStruct((Tp, D), F32), jax.ShapeDtypeStruct((Tp, D), F32)),
        grid=(Tp // ROW_TILE,),
        in_specs=[_rows(D), _rows(W), _rows(W), _const((1, W)), _const((1, W)), _const((1, D)), _resident(wout.shape)],
        out_specs=[_rows(D), _rows(D)],
        compiler_params=_cparams(),
    )(h, y_lru, y_mla, lru_g, mla_g, post_g, wout)


def _mix_out_bwd(dh, y, y_lru, y_mla, lru_g, mla_g, post_g, wout):
    Tp, D = dh.shape
    W = y_lru.shape[1]

    def body(dh_ref, y_ref, yl_ref, ym_ref, lg_ref, mg_ref, pg_ref, w_ref,
             dyl_ref, dym_ref, dy_ref, cat_ref, dlg_ref, dmg_ref, dpg_ref):
        first = pl.program_id(0) == 0
        pg, lg, mg = pg_ref[...], lg_ref[...], mg_ref[...]
        _, yn, yrinv = _rms_fwd(y_ref[...], pg)
        dy, dpg = _rms_bwd(yn, yrinv, pg, dh_ref[...])
        dyb = dy.astype(BF16)
        dy_ref[...] = dyb
        dcat = _dot_nt(dyb, w_ref[...])
        yl, yln, ylr = _rms_fwd(yl_ref[...], lg)
        ym, ymn, ymr = _rms_fwd(ym_ref[...], mg)
        cat_ref[:, 0:W] = yl.astype(BF16)
        cat_ref[:, W:] = ym.astype(BF16)
        dyl, dlg = _rms_bwd(yln, ylr, lg, dcat[:, 0:W])
        dym, dmg = _rms_bwd(ymn, ymr, mg, dcat[:, W:])
        dyl_ref[...] = dyl
        dym_ref[...] = dym
        _accumulate(dlg_ref, dlg, first)
        _accumulate(dmg_ref, dmg, first)
        _accumulate(dpg_ref, dpg, first)

    return pl.pallas_call(
        body, name="mix_out_bwd",
        out_shape=(jax.ShapeDtypeStruct((Tp, W), F32), jax.ShapeDtypeStruct((Tp, W), F32),
                   jax.ShapeDtypeStruct((Tp, D), BF16), jax.ShapeDtypeStruct((Tp, 2 * W), BF16),
                   jax.ShapeDtypeStruct((1, W), F32), jax.ShapeDtypeStruct((1, W), F32),
                   jax.ShapeDtypeStruct((1, D), F32)),
        grid=(Tp // ROW_TILE,),
        in_specs=[_rows(D), _rows(D), _rows(W), _rows(W), _const((1, W)), _const((1, W)), _const((1, D)),
                  _resident(wout.shape)],
        out_specs=[_rows(W), _rows(W), _rows(D), _rows(2 * W), _const((1, W)), _const((1, W)), _const((1, D))],
        compiler_params=_cparams(),
    )(dh, y, y_lru, y_mla, lru_g, mla_g, post_g, wout)


def _softplus_neg(lam):
    return jnp.maximum(-lam, 0.0) + jnp.log1p(jnp.exp(-jnp.abs(lam)))


def _neg_expm1(y):
    series = -y * (1.0 + 0.5 * y * (1.0 + (1.0 / 3.0) * y * (1.0 + 0.25 * y)))
    return jnp.where(y > -0.01, series, 1.0 - jnp.exp(y))


_GELU_K = math.sqrt(2.0 / math.pi)


def _gelu(x):
    return 0.5 * x * (1.0 + jnp.tanh(_GELU_K * (x + 0.044715 * (x * x * x))))


def _gelu_grad(x):
    t = jnp.tanh(_GELU_K * (x + 0.044715 * (x * x * x)))
    return 0.5 * (1.0 + t) + 0.5 * x * (1.0 - t * t) * (_GELU_K * (1.0 + 3.0 * 0.044715 * (x * x)))


def _lru_conv(xpad_ref, lo, n, cw, cb):
    xc = xpad_ref[pl.ds(8 + lo, n), :] * cw[3:4, :] + cb
    for k in range(CONV_WIDTH - 1):
        xc = xc + xpad_ref[pl.ds(8 + lo - (CONV_WIDTH - 1 - k), n), :] * cw[k:k + 1, :]
    return xc


def _lru_gates(xc, wa, wx, ba, bx, sp):
    xb = xc.astype(BF16)
    r = jax.nn.sigmoid(_dot(xb, wa) + ba)
    i = jax.nn.sigmoid(_dot(xb, wx) + bx)
    la = (-LRU_C * r) * sp
    a = jnp.exp(la)
    mult = jnp.sqrt(_neg_expm1(2.0 * la))
    return r, i, a, mult


def _scan_carries(last_h, last_p, reverse):
    row = lax.broadcasted_iota(jnp.int32, last_h.shape, 0)
    carry = jnp.zeros_like(last_h)
    for _ in range(SCAN_CHUNKS - 1):
        nxt = last_h + last_p * carry
        if reverse:
            carry = jnp.where(row < SCAN_CHUNKS - 1, pltpu.roll(nxt, SCAN_CHUNKS - 1, axis=0), 0.0)
        else:
            carry = jnp.where(row > 0, pltpu.roll(nxt, 1, axis=0), 0.0)
    return carry


def _lru_forward_scan(a_ref, h_ref, p_ref, a_off, h_off, rc):
    zero = jnp.zeros((SCAN_CHUNKS, LANES), F32)

    def step(i, carry):
        hh, pp = carry
        av = a_ref[pl.ds(a_off + i, SCAN_CHUNKS, stride=rc), :]
        hh = av * hh + h_ref[pl.ds(h_off + i, SCAN_CHUNKS, stride=rc), :]
        pp = av * pp
        h_ref[pl.ds(h_off + i, SCAN_CHUNKS, stride=rc), :] = hh
        p_ref[pl.ds(i, SCAN_CHUNKS, stride=rc), :] = pp
        return hh, pp

    last_h, last_p = lax.fori_loop(0, rc, step, (zero, zero + 1.0))
    carry = _scan_carries(last_h, last_p, reverse=False)
    for c in range(SCAN_CHUNKS):
        rows = pl.ds(h_off + c * rc, rc)
        h_ref[rows, :] = h_ref[rows, :] + p_ref[pl.ds(c * rc, rc), :] * carry[c:c + 1, :]


def _rglru_fwd(xr, gr, cw, cb, wa, wx, ba, bx, lam):
    Tp, W = xr.shape
    nb = W // LANES
    rc = Tp // SCAN_CHUNKS

    def body(xr_ref, gr_ref, cw_ref, cb_ref, wa_ref, wx_ref, ba_ref, bx_ref, lam_ref, y_ref,
             xpad, a_s, h_s, p_s):
        xpad[0:8, :] = jnp.zeros((8, LANES), F32)
        xpad[pl.ds(8, Tp), :] = xr_ref[...]
        cw_, cb_ = cw_ref[...], cb_ref[...]
        sp = _softplus_neg(lam_ref[...])
        for c in range(SCAN_CHUNKS):
            xc = _lru_conv(xpad, c * rc, rc, cw_, cb_)
            _, i, a, mult = _lru_gates(xc, wa_ref[...], wx_ref[...], ba_ref[...], bx_ref[...], sp)
            a_s[pl.ds(c * rc, rc), :] = a
            h_s[pl.ds(c * rc, rc), :] = mult * (i * xc)
        _lru_forward_scan(a_s, h_s, p_s, 0, 0, rc)
        for c in range(SCAN_CHUNKS):
            rows = pl.ds(c * rc, rc)
            y_ref[rows, :] = h_s[rows, :] * _gelu(gr_ref[rows, :])

    col = pl.BlockSpec((Tp, LANES), lambda j: (0, j))
    vec = pl.BlockSpec((1, LANES), lambda j: (0, j))
    mat = pl.BlockSpec((None, LANES, LANES), lambda j: (j, 0, 0))
    return pl.pallas_call(
        body, name="rglru_fwd",
        out_shape=jax.ShapeDtypeStruct((Tp, W), F32),
        grid=(nb,),
        in_specs=[col, col, pl.BlockSpec((CONV_WIDTH, LANES), lambda j: (0, j)), vec, mat, mat, vec, vec, vec],
        out_specs=col,
        scratch_shapes=[pltpu.VMEM((Tp + 8, LANES), F32), pltpu.VMEM((Tp, LANES), F32),
                        pltpu.VMEM((Tp, LANES), F32), pltpu.VMEM((Tp, LANES), F32)],
        compiler_params=_cparams(),
    )(xr, gr, cw, cb, wa, wx, ba, bx, lam)


def _rglru_bwd(dy, xr, gr, cw, cb, wa, wx, ba, bx, lam):
    Tp, W = xr.shape
    nb = W // LANES
    rc = Tp // SCAN_CHUNKS

    def body(dy_ref, xr_ref, gr_ref, cw_ref, cb_ref, wa_ref, wx_ref, ba_ref, bx_ref, lam_ref,
             dxr_ref, dgr_ref, dcw_ref, dcb_ref, dwa_ref, dwx_ref, dba_ref, dbx_ref, dlam_ref,
             xpad, a_s, h_s, p_s, xc_s, g_s, dxc_s):
        zeros8 = jnp.zeros((8, LANES), F32)
        xpad[0:8, :] = zeros8
        xpad[pl.ds(8, Tp), :] = xr_ref[...]
        a_s[pl.ds(Tp, 8), :] = zeros8
        h_s[0:8, :] = zeros8
        dxc_s[pl.ds(Tp, 8), :] = zeros8
        cw_, cb_ = cw_ref[...], cb_ref[...]
        lam_ = lam_ref[...]
        sp = _softplus_neg(lam_)
        gate_args = (wa_ref[...], wx_ref[...], ba_ref[...], bx_ref[...], sp)

        for c in range(SCAN_CHUNKS):
            rows = pl.ds(c * rc, rc)
            xc = _lru_conv(xpad, c * rc, rc, cw_, cb_)
            xc_s[rows, :] = xc
            _, i, a, mult = _lru_gates(xc, *gate_args)
            a_s[rows, :] = a
            h_s[pl.ds(8 + c * rc, rc), :] = mult * (i * xc)
        _lru_forward_scan(a_s, h_s, p_s, 0, 8, rc)

        for c in range(SCAN_CHUNKS):
            rows = pl.ds(c * rc, rc)
            dyv, grv = dy_ref[rows, :], gr_ref[rows, :]
            g_s[rows, :] = dyv * _gelu(grv)
            dgr_ref[rows, :] = dyv * h_s[pl.ds(8 + c * rc, rc), :] * _gelu_grad(grv)

        zero = jnp.zeros((SCAN_CHUNKS, LANES), F32)

        def rstep(k, carry):
            gg, qq = carry
            i = rc - 1 - k
            av = a_s[pl.ds(i + 1, SCAN_CHUNKS, stride=rc), :]
            gg = g_s[pl.ds(i, SCAN_CHUNKS, stride=rc), :] + av * gg
            qq = av * qq
            g_s[pl.ds(i, SCAN_CHUNKS, stride=rc), :] = gg
            p_s[pl.ds(i, SCAN_CHUNKS, stride=rc), :] = qq
            return gg, qq

        first_g, first_q = lax.fori_loop(0, rc, rstep, (zero, zero + 1.0))
        carry = _scan_carries(first_g, first_q, reverse=True)
        for c in range(SCAN_CHUNKS):
            rows = pl.ds(c * rc, rc)
            g_s[rows, :] = g_s[rows, :] + p_s[rows, :] * carry[c:c + 1, :]

        dwa = jnp.zeros((LANES, LANES), F32)
        dwx = jnp.zeros((LANES, LANES), F32)
        dba = jnp.zeros((1, LANES), F32)
        dbx = jnp.zeros((1, LANES), F32)
        dsp = jnp.zeros((1, LANES), F32)
        for c in range(SCAN_CHUNKS):
            rows = pl.ds(c * rc, rc)
            xc = xc_s[rows, :]
            r, i, a, mult = _lru_gates(xc, *gate_args)
            gg = g_s[rows, :]
            da = gg * h_s[pl.ds(7 + c * rc, rc), :]
            d_s = gg * mult
            dmult = gg * (i * xc)
            dla = da * a - dmult * (a * a) / mult
            dr = dla * (-LRU_C * sp)
            dsp = dsp + jnp.sum(dla * (-LRU_C * r), axis=0, keepdims=True)
            dpr = dr * r * (1.0 - r)
            dpi = (d_s * xc) * i * (1.0 - i)
            dprb, dpib, xcb = dpr.astype(BF16), dpi.astype(BF16), xc.astype(BF16)
            dxc_s[rows, :] = d_s * i + _dot_nt(dprb, wa_ref[...]) + _dot_nt(dpib, wx_ref[...])
            dwa = dwa + _dot_tn(xcb, dprb)
            dwx = dwx + _dot_tn(xcb, dpib)
            dba = dba + jnp.sum(dpr, axis=0, keepdims=True)
            dbx = dbx + jnp.sum(dpi, axis=0, keepdims=True)
        dwa_ref[...] = dwa
        dwx_ref[...] = dwx
        dba_ref[...] = dba
        dbx_ref[...] = dbx
        dlam_ref[...] = dsp * (-jax.nn.sigmoid(-lam_))

        dcw = [jnp.zeros((1, LANES), F32) for _ in range(CONV_WIDTH)]
        dcb = jnp.zeros((1, LANES), F32)
        for c in range(SCAN_CHUNKS):
            rows = pl.ds(c * rc, rc)
            dxc = dxc_s[rows, :]
            dcb = dcb + jnp.sum(dxc, axis=0, keepdims=True)
            dxr = dxc * cw_[3:4, :]
            for k in range(CONV_WIDTH):
                back = CONV_WIDTH - 1 - k
                dcw[k] = dcw[k] + jnp.sum(dxc * xpad[pl.ds(8 + c * rc - back, rc), :], axis=0, keepdims=True)
                if back:
                    dxr = dxr + dxc_s[pl.ds(c * rc + back, rc), :] * cw_[k:k + 1, :]
            dxr_ref[rows, :] = dxr
        dcw_ref[...] = jnp.concatenate(dcw, axis=0)
        dcb_ref[...] = dcb

    col = pl.BlockSpec((Tp, LANES), lambda j: (0, j))
    vec = pl.BlockSpec((1, LANES), lambda j: (0, j))
    mat = pl.BlockSpec((None, LANES, LANES), lambda j: (j, 0, 0))
    cwspec = pl.BlockSpec((CONV_WIDTH, LANES), lambda j: (0, j))
    return pl.pallas_call(
        body, name="rglru_bwd",
        out_shape=(jax.ShapeDtypeStruct((Tp, W), F32), jax.ShapeDtypeStruct((Tp, W), F32),
                   jax.ShapeDtypeStruct((CONV_WIDTH, W), F32), jax.ShapeDtypeStruct((1, W), F32),
                   jax.ShapeDtypeStruct((nb, LANES, LANES), F32), jax.ShapeDtypeStruct((nb, LANES, LANES), F32),
                   jax.ShapeDtypeStruct((1, W), F32), jax.ShapeDtypeStruct((1, W), F32),
                   jax.ShapeDtypeStruct((1, W), F32)),
        grid=(nb,),
        in_specs=[col, col, col, cwspec, vec, mat, mat, vec, vec, vec],
        out_specs=[col, col, cwspec, vec, mat, mat, vec, vec, vec],
        scratch_shapes=[pltpu.VMEM((Tp + 8, LANES), F32), pltpu.VMEM((Tp + 8, LANES), F32),
                        pltpu.VMEM((Tp + 8, LANES), F32), pltpu.VMEM((Tp, LANES), F32),
                        pltpu.VMEM((Tp, LANES), F32), pltpu.VMEM((Tp, LANES), F32),
                        pltpu.VMEM((Tp + 8, LANES), F32)],
        compiler_params=_cparams(),
    )(dy, xr, gr, cw, cb, wa, wx, ba, bx, lam)


def _rope(x, cosm, sin_hi, sin_lo):
    return x * cosm + pltpu.roll(x, QK_ROPE // 2, axis=1) * sin_hi + pltpu.roll(x, LANES - QK_ROPE // 2, axis=1) * sin_lo


def _rope_inverse(x, cosm, sin_hi, sin_lo):
    return x * cosm - pltpu.roll(x, QK_ROPE // 2, axis=1) * sin_hi - pltpu.roll(x, LANES - QK_ROPE // 2, axis=1) * sin_lo


def _mla_proj_fwd(cq, ckv, kr, qg, kg, wuq, wkv, cosm, sin_hi, sin_lo):
    Tp, QL = cq.shape
    KL = ckv.shape[1]
    H = MLA_HEADS
    half = LANES // 2

    def body(cq_ref, ckv_ref, kr_ref, qg_ref, kg_ref, wuq_ref, wkv_ref, c_ref, s1_ref, s2_ref,
             q_ref, k_ref, v_ref):
        tabs = (c_ref[...], s1_ref[...], s2_ref[...])
        lane = lax.broadcasted_iota(jnp.int32, (ROW_TILE, LANES), 1)
        low = lane < half
        cqn, _, _ = _rms_fwd(cq_ref[...], qg_ref[...])
        q = _dot(cqn.astype(BF16), wuq_ref[...])
        ckvn, _, _ = _rms_fwd(ckv_ref[...], kg_ref[...])
        kv = _dot(ckvn.astype(BF16), wkv_ref[...])
        krr = _rope(kr_ref[...], *tabs)
        for hd in range(H):
            sl = slice(hd * LANES, (hd + 1) * LANES)
            q_ref[hd] = _rope(q[:, sl], *tabs).astype(BF16)
            k_ref[hd] = jnp.where(low, kv[:, sl], krr).astype(BF16)
        for p in range(H // 2):
            even = kv[:, 2 * p * LANES:(2 * p + 1) * LANES]
            odd = kv[:, (2 * p + 1) * LANES:(2 * p + 2) * LANES]
            v_ref[:, p * LANES:(p + 1) * LANES] = jnp.where(low, pltpu.roll(even, half, axis=1), odd).astype(BF16)

    heads = pl.BlockSpec((H, ROW_TILE, LANES), lambda i: (0, i, 0))
    return pl.pallas_call(
        body, name="mla_proj_fwd",
        out_shape=(jax.ShapeDtypeStruct((H, Tp, LANES), BF16), jax.ShapeDtypeStruct((H, Tp, LANES), BF16),
                   jax.ShapeDtypeStruct((Tp, H * half), BF16)),
        grid=(Tp // ROW_TILE,),
        in_specs=[_rows(QL), _rows(KL), _rows(LANES), _const((1, QL)), _const((1, KL)),
                  _resident(wuq.shape), _resident(wkv.shape), _rows(LANES), _rows(LANES), _rows(LANES)],
        out_specs=[heads, heads, _rows(H * half)],
        compiler_params=_cparams(),
    )(cq, ckv, kr, qg, kg, wuq, wkv, cosm, sin_hi, sin_lo)


def _mla_proj_bwd(dq, dk, dv, cq, ckv, qg, kg, wuq, wkv, cosm, sin_hi, sin_lo):
    H, Tp, _ = dq.shape
    QL, KL = cq.shape[1], ckv.shape[1]
    half = LANES // 2

    def body(dq_ref, dk_ref, dv_ref, cq_ref, ckv_ref, qg_ref, kg_ref, wuq_ref, wkv_ref,
             c_ref, s1_ref, s2_ref,
             dcq_ref, dckv_ref, dkr_ref, dqa_ref, dkva_ref, cqn_ref, ckvn_ref, dqg_ref, dkg_ref):
        first = pl.program_id(0) == 0
        tabs = (c_ref[...], s1_ref[...], s2_ref[...])
        lane = lax.broadcasted_iota(jnp.int32, (ROW_TILE, LANES), 1)
        low = lane < half
        rope_lanes = jnp.logical_and(lane >= QK_NOPE, lane < QK_NOPE + QK_ROPE)
        dkr = jnp.zeros((ROW_TILE, LANES), F32)
        for hd in range(H):
            sl = slice(hd * LANES, (hd + 1) * LANES)
            dqa_ref[:, sl] = _rope_inverse(dq_ref[hd], *tabs).astype(BF16)
            dkh = dk_ref[hd]
            dvp = dv_ref[:, (hd // 2) * LANES:(hd // 2 + 1) * LANES]
            dvh = pltpu.roll(dvp, half, axis=1) if hd % 2 == 0 else dvp
            dkva_ref[:, sl] = jnp.where(low, dkh, dvh).astype(BF16)
            dkr = dkr + jnp.where(rope_lanes, dkh, 0.0)
        dkr_ref[...] = _rope_inverse(dkr, *tabs)
        qg, kg = qg_ref[...], kg_ref[...]
        cqs, cqn, cqr = _rms_fwd(cq_ref[...], qg)
        cqn_ref[...] = cqs.astype(BF16)
        dcq, dqg = _rms_bwd(cqn, cqr, qg, _dot_nt(dqa_ref[...], wuq_ref[...]))
        dcq_ref[...] = dcq
        cks, ckn, ckr = _rms_fwd(ckv_ref[...], kg)
        ckvn_ref[...] = cks.astype(BF16)
        dckv, dkg = _rms_bwd(ckn, ckr, kg, _dot_nt(dkva_ref[...], wkv_ref[...]))
        dckv_ref[...] = dckv
        _accumulate(dqg_ref, dqg, first)
        _accumulate(dkg_ref, dkg, first)

    heads = pl.BlockSpec((H, ROW_TILE, LANES), lambda i: (0, i, 0))
    return pl.pallas_call(
        body, name="mla_proj_bwd",
        out_shape=(jax.ShapeDtypeStruct((Tp, QL), F32), jax.ShapeDtypeStruct((Tp, KL), F32),
                   jax.ShapeDtypeStruct((Tp, LANES), F32), jax.ShapeDtypeStruct((Tp, H * LANES), BF16),
                   jax.ShapeDtypeStruct((Tp, H * LANES), BF16),
                   jax.ShapeDtypeStruct((Tp, QL), BF16), jax.ShapeDtypeStruct((Tp, KL), BF16),
                   jax.ShapeDtypeStruct((1, QL), F32), jax.ShapeDtypeStruct((1, KL), F32)),
        grid=(Tp // ROW_TILE,),
        in_specs=[heads, heads, _rows(H * half), _rows(QL), _rows(KL), _const((1, QL)), _const((1, KL)),
                  _resident(wuq.shape), _resident(wkv.shape), _rows(LANES), _rows(LANES), _rows(LANES)],
        out_specs=[_rows(QL), _rows(KL), _rows(LANES), _rows(H * LANES), _rows(H * LANES),
                   _rows(QL), _rows(KL), _const((1, QL)), _const((1, KL))],
        compiler_params=_cparams(),
    )(dq, dk, dv, cq, ckv, qg, kg, wuq, wkv, cosm, sin_hi, sin_lo)


_NEG = -1e30
_ATT_SCALE = (QK_NOPE + QK_ROPE) ** -0.5
_ATT_LOG2 = _ATT_SCALE * math.log2(math.e)


def _causal(s, q0, k0, transposed):
    r = lax.broadcasted_iota(jnp.int32, s.shape, 0)
    c = lax.broadcasted_iota(jnp.int32, s.shape, 1)
    keep = (k0 + r <= q0 + c) if transposed else (k0 + c <= q0 + r)
    return jnp.where(keep, s, _NEG)


def _hosted(comm, n_in, n_out, refs, first, last):
    if comm is None:
        return refs, lambda: None
    n = comm.n
    own = refs[:n_in] + refs[n_in + n:n_in + n + n_out]
    comm.bind(refs[n_in:n_in + n], refs[n_in + n + n_out:n_in + 2 * n + n_out], refs[n_in + 2 * n + n_out:])
    pl.when(first)(comm.start)
    return own, lambda: pl.when(last)(comm.finish)


def _attn_fwd(q, k, v, comm=None, comm_arrays=()):
    H, Tp, _ = q.shape
    t = ROW_TILE
    half = LANES // 2
    n_extra = 0 if comm is None else comm.n

    def body(*refs):
        p_id, i = pl.program_id(0), pl.program_id(1)
        first = jnp.logical_and(p_id == 0, i == 0)
        last = jnp.logical_and(p_id == H // 2 - 1, i == Tp // t - 1)
        (q_ref, k_ref, v_ref, y_ref, lse_ref), finish = _hosted(comm, 3, 2, refs, first, last)
        qs = [q_ref[0], q_ref[1]]

        def step(k0, carry, width, masked):
            vv = v_ref[pl.ds(k0, width), :]
            out = []
            for hh in range(2):
                m, l, acc = carry[hh]
                kv = k_ref[hh, pl.ds(k0, width), :]
                s = _dot_nt(qs[hh], kv) * _ATT_LOG2
                if masked:
                    s = _causal(s, i * t, k0, False)
                m_new = jnp.maximum(m, jnp.max(s, axis=-1, keepdims=True))
                alpha = jnp.exp2(m - m_new)
                p = jnp.exp2(s - m_new)
                l = alpha * l + jnp.sum(p, axis=-1, keepdims=True)
                acc = alpha * acc + _dot(p.astype(BF16), vv)
                out.append((m_new, l, acc))
            return tuple(out)

        one = (jnp.full((t, 1), _NEG, F32), jnp.zeros((t, 1), F32), jnp.zeros((t, LANES), F32))
        carry = lax.fori_loop(0, i // 2, lambda jj, c: step(pl.multiple_of(jj * 2 * t, t), c, 2 * t, False), (one, one))
        k0 = pl.multiple_of((i // 2) * 2 * t, t)
        carry = lax.cond(i % 2 == 0, lambda c: step(k0, c, t, True), lambda c: step(k0, c, 2 * t, True), carry)
        outs = []
        for hh in range(2):
            m, l, acc = carry[hh]
            outs.append(acc / l)
            lse_ref[hh] = m + jnp.log2(l)
        lane = lax.broadcasted_iota(jnp.int32, (t, LANES), 1)
        y_ref[...] = jnp.where(lane < half, outs[0], outs[1])
        finish()

    return pl.pallas_call(
        body, name="attn_fwd" if comm is None else "attn_fwd_gather",
        out_shape=(jax.ShapeDtypeStruct((Tp, H * half), F32), jax.ShapeDtypeStruct((H, Tp, 1), F32))
        + (() if comm is None else comm.out_shape),
        grid=(H // 2, Tp // t),
        in_specs=[pl.BlockSpec((2, t, LANES), lambda p, i: (p, i, 0)),
                  pl.BlockSpec((2, Tp, LANES), lambda p, i: (p, 0, 0)),
                  pl.BlockSpec((Tp, LANES), lambda p, i: (0, p))] + [_HBM] * n_extra,
        out_specs=[pl.BlockSpec((t, LANES), lambda p, i: (i, p)),
                   pl.BlockSpec((2, t, 1), lambda p, i: (p, i, 0))] + [_HBM] * n_extra,
        scratch_shapes=[] if comm is None else comm.scratch,
        compiler_params=_cparams(),
    )(q, k, v, *comm_arrays)


def _attn_bwd_q(q, k, v, dy, y, lse):
    H, Tp, _ = q.shape
    t = ROW_TILE
    half = LANES // 2

    def body(q_ref, k_ref, v_ref, dy_ref, y_ref, lse_ref, dq_ref, delta_ref):
        i = pl.program_id(1)
        lane = lax.broadcasted_iota(jnp.int32, (t, LANES), 1)
        qs, dobs, deltas, lses = [], [], [], []
        for hh in range(2):
            mine = (lane < half) if hh == 0 else (lane >= half)
            do = jnp.where(mine, dy_ref[...], 0.0)
            delta = jnp.sum(do * y_ref[...], axis=-1, keepdims=True)
            delta_ref[hh] = delta
            qs.append(q_ref[hh])
            dobs.append(do.astype(BF16))
            deltas.append(delta)
            lses.append(lse_ref[hh])

        def step(k0, dqs, width, masked):
            vv = v_ref[pl.ds(k0, width), :]
            out = []
            for hh in range(2):
                kv = k_ref[hh, pl.ds(k0, width), :]
                s = _dot_nt(qs[hh], kv) * _ATT_LOG2
                if masked:
                    s = _causal(s, i * t, k0, False)
                p = jnp.exp2(s - lses[hh])
                dp = _dot_nt(dobs[hh], vv)
                ds = p * (dp - deltas[hh])
                out.append(dqs[hh] + _dot(ds.astype(BF16), kv))
            return tuple(out)

        zero = jnp.zeros((t, LANES), F32)
        dqs = lax.fori_loop(0, i // 2, lambda jj, c: step(pl.multiple_of(jj * 2 * t, t), c, 2 * t, False), (zero, zero))
        k0 = pl.multiple_of((i // 2) * 2 * t, t)
        dqs = lax.cond(i % 2 == 0, lambda c: step(k0, c, t, True), lambda c: step(k0, c, 2 * t, True), dqs)
        for hh in range(2):
            dq_ref[hh] = dqs[hh] * _ATT_SCALE

    return pl.pallas_call(
        body, name="attn_bwd_q",
        out_shape=(jax.ShapeDtypeStruct((H, Tp, LANES), F32), jax.ShapeDtypeStruct((H, Tp, 1), F32)),
        grid=(H // 2, Tp // t),
        in_specs=[pl.BlockSpec((2, t, LANES), lambda p, i: (p, i, 0)),
                  pl.BlockSpec((2, Tp, LANES), lambda p, i: (p, 0, 0)),
                  pl.BlockSpec((Tp, LANES), lambda p, i: (0, p)),
                  pl.BlockSpec((t, LANES), lambda p, i: (i, p)),
                  pl.BlockSpec((t, LANES), lambda p, i: (i, p)),
                  pl.BlockSpec((2, t, 1), lambda p, i: (p, i, 0))],
        out_specs=[pl.BlockSpec((2, t, LANES), lambda p, i: (p, i, 0)),
                   pl.BlockSpec((2, t, 1), lambda p, i: (p, i, 0))],
        compiler_params=_cparams(),
    )(q, k, v, dy, y, lse)


def _attn_bwd_kv(q, k, v, dy, lse_row, delta_row, comm=None, comm_arrays=()):
    H, Tp, _ = q.shape
    t = ROW_TILE
    half = LANES // 2
    nq = Tp // t
    n_extra = 0 if comm is None else comm.n

    def body(*refs):
        p_id, j = pl.program_id(0), pl.program_id(1)
        first = jnp.logical_and(p_id == 0, j == 0)
        last = jnp.logical_and(p_id == H // 2 - 1, j == nq - 1)
        (q_ref, k_ref, v_ref, dy_ref, lse_ref, delta_ref, dk_ref, dv_ref), finish = _hosted(comm, 6, 2, refs, first, last)
        vv = v_ref[...]
        ks = [k_ref[0], k_ref[1]]

        def step(q0, carry, width, masked):
            cols = pl.ds(pl.multiple_of(q0, LANES), width)
            dyv = dy_ref[pl.ds(q0, width), :]
            lane_w = lax.broadcasted_iota(jnp.int32, (width, LANES), 1)
            out = []
            for hh in range(2):
                dk, dv = carry[hh]
                mine = (lane_w < half) if hh == 0 else (lane_w >= half)
                qv = q_ref[hh, pl.ds(q0, width), :]
                dob = jnp.where(mine, dyv, 0.0).astype(BF16)
                st = _dot_nt(ks[hh], qv) * _ATT_LOG2
                if masked:
                    st = _causal(st, q0, j * t, True)
                pt = jnp.exp2(st - lse_ref[hh, :, cols])
                dv = dv + _dot(pt.astype(BF16), dob)
                dpt = _dot_nt(vv, dob)
                dst = pt * (dpt - delta_ref[hh, :, cols])
                dk = dk + _dot(dst.astype(BF16), qv)
                out.append((dk, dv))
            return tuple(out)

        zero = jnp.zeros((t, LANES), F32)
        init = ((zero, zero), (zero, zero))
        q0 = pl.multiple_of(j * t, t)
        even = (nq - j) % 2 == 0
        carry = lax.cond(even, lambda c: step(q0, c, 2 * t, True), lambda c: step(q0, c, t, True), init)
        rest = pl.multiple_of((j + 1 + even.astype(jnp.int32)) * t, t)
        carry = lax.fori_loop(0, (nq - j - 1) // 2,
                              lambda ii, c: step(pl.multiple_of(rest + ii * 2 * t, t), c, 2 * t, False), carry)
        for hh in range(2):
            dk_ref[hh] = carry[hh][0] * _ATT_SCALE
        dv_ref[...] = carry[0][1] + carry[1][1]
        finish()

    return pl.pallas_call(
        body, name="attn_bwd_kv" if comm is None else "attn_bwd_kv_exchange",
        out_shape=(jax.ShapeDtypeStruct((H, Tp, LANES), F32), jax.ShapeDtypeStruct((Tp, H * half), F32))
        + (() if comm is None else comm.out_shape),
        grid=(H // 2, nq),
        in_specs=[pl.BlockSpec((2, Tp, LANES), lambda p, j: (p, 0, 0)),
                  pl.BlockSpec((2, t, LANES), lambda p, j: (p, j, 0)),
                  pl.BlockSpec((t, LANES), lambda p, j: (j, p)),
                  pl.BlockSpec((Tp, LANES), lambda p, j: (0, p)),
                  pl.BlockSpec((2, 1, Tp), lambda p, j: (p, 0, 0)),
                  pl.BlockSpec((2, 1, Tp), lambda p, j: (p, 0, 0))] + [_HBM] * n_extra,
        out_specs=[pl.BlockSpec((2, t, LANES), lambda p, j: (p, j, 0)),
                   pl.BlockSpec((t, LANES), lambda p, j: (j, p))] + [_HBM] * n_extra,
        scratch_shapes=[] if comm is None else comm.scratch,
        compiler_params=_cparams(),
    )(q, k, v, dy, lse_row, delta_row, *comm_arrays)


def _loss_head(h, target, lo, hi):
    Tp, D = h.shape

    def body(h_ref, t_ref, dh_ref, loss_ref):
        i = pl.program_id(0)
        row = i * ROW_TILE + lax.broadcasted_iota(jnp.int32, (ROW_TILE, 1), 0)
        live = jnp.logical_and(row >= lo, row < hi)
        err = jnp.where(live, h_ref[...] - t_ref[...], 0.0)
        dh_ref[...] = err / D
        part = 0.5 * jnp.sum(jnp.mean(err * err, axis=-1, keepdims=True), axis=0, keepdims=True)
        _accumulate(loss_ref, jnp.broadcast_to(part, (8, LANES)), i == 0)

    return pl.pallas_call(
        body, name="loss_head",
        out_shape=(jax.ShapeDtypeStruct((Tp, D), F32), jax.ShapeDtypeStruct((8, LANES), F32)),
        grid=(Tp // ROW_TILE,),
        in_specs=[_rows(D), _rows(D)],
        out_specs=[_rows(D), _const((8, LANES))],
        compiler_params=_cparams(),
    )(h, target)


def _pack(arrays, row_multiple=8):
    flat = jnp.concatenate([a.reshape(-1).astype(F32) for a in arrays])
    quantum = LANES * row_multiple
    total = -(-flat.shape[0] // quantum) * quantum
    flat = jnp.pad(flat, (0, total - flat.shape[0]))
    return flat.reshape(-1, LANES)


def _unpack(buf, shapes):
    flat = buf.reshape(-1)
    out, off = [], 0
    for shp in shapes:
        n = math.prod(shp)
        out.append(flat[off:off + n].reshape(tuple(shp)))
        off += n
    return out


def _block_diag_blocks(w):
    nh, d, _ = w.shape
    per = LANES // d
    eye = jnp.eye(per, dtype=w.dtype)
    g = w.reshape(nh // per, per, d, d)
    return jnp.einsum('bpij,pq->bpiqj', g, eye).reshape(nh // per, LANES, LANES)


def _block_diag_extract(blocks, d):
    nb = blocks.shape[0]
    per = LANES // d
    return jnp.stack([blocks[b, p * d:(p + 1) * d, p * d:(p + 1) * d] for b in range(nb) for p in range(per)])


def _step(a):
    x = a['x'][0]
    target = a['loss_target'][0]
    seq, D = x.shape
    n_meta = a['meta_tokens'].shape[0]
    T = seq + n_meta
    Tp = -(-T // ROW_TILE) * ROW_TILE
    L = a['ffn1_pre_g'].shape[0]
    W = a['lru_conv_b'].shape[1]
    QL = a['mla_q_norm_g'].shape[1]
    KL = a['mla_kv_norm_g'].shape[1]
    H = MLA_HEADS
    LD = a['lru_w_a'].shape[2]
    n_in = 2 * W + QL + KL
    assert W % LANES == 0 and QL % LANES == 0 and KL % LANES == 0 and LANES % LD == 0
    assert a['mla_w_ukv'].shape[2] == LANES and a['mla_w_uq'].shape[2] == QK_NOPE + QK_ROPE

    ax, ay, ac = (lax.axis_index(n) for n in MESH_AXES)
    me = 4 * ax + 2 * ay + ac
    core = ac.reshape(1).astype(jnp.int32)
    chip = (2 * ax + ay).reshape(1).astype(jnp.int32)

    kinds = {'ffn1_w_gate': 'cols', 'ffn1_w_up': 'cols', 'ffn1_w_down': 'rows', 'w_in': 'w_in', 'mla_w_uq': 'heads',
             'mla_w_ukv': 'cols', 'w_out': 'rows', 'ffn2_w_gate': 'cols', 'ffn2_w_up': 'cols', 'ffn2_w_down': 'rows'}
    layer_shards = lambda l: [a[n][l].astype(BF16) for n in BIG]
    assemble = lambda gathered: {n: _assemble(g, kinds[n], n_in) for n, g in zip(BIG, gathered)}
    first = _all_gather(layer_shards(0) + [a[n] for n in SMALL_SHARDED])
    big = [assemble(first[:len(BIG)])]
    gcw, gmeta = first[len(BIG):]
    conv_w = jnp.moveaxis(gcw, 0, 2).reshape(L, CONV_WIDTH, W)
    meta = jnp.moveaxis(gmeta, 0, 1).reshape(n_meta, D)
    wa_blk = jax.vmap(_block_diag_blocks)(a['lru_w_a']).astype(BF16)
    wx_blk = jax.vmap(_block_diag_blocks)(a['lru_w_x']).astype(BF16)
    widths = [W, W, QL, KL, LANES]

    QH = QK_NOPE + QK_ROPE
    pos = jnp.arange(Tp, dtype=F32)
    inv_freq = 1.0 / (ROPE_THETA ** (jnp.arange(0, QK_ROPE, 2, dtype=F32) / QK_ROPE))
    ang = pos[:, None] * inv_freq[None, :]
    cos, sin = jnp.cos(ang), jnp.sin(ang)
    hr = QK_ROPE // 2
    z = lambda n: jnp.zeros((Tp, n), F32)
    cosm = jnp.concatenate([jnp.ones((Tp, QK_NOPE), F32), cos, cos, z(LANES - QH)], axis=1)
    sin_hi = jnp.concatenate([z(QK_NOPE + hr), sin, z(LANES - QH)], axis=1)
    sin_lo = jnp.concatenate([z(QK_NOPE), -sin, z(LANES - QK_NOPE - hr)], axis=1)
    tabs = (cosm, sin_hi, sin_lo)

    vec = lambda name, l: a[name][l][None, :]

    h = jnp.concatenate([meta, x, jnp.zeros((Tp - T, D), F32)], axis=0)
    tpad = jnp.concatenate([jnp.zeros((n_meta, D), F32), target, jnp.zeros((Tp - T, D), F32)], axis=0)
    saved = []
    for l in range(L):
        wl = big[l]
        s = {'h0': h}
        h, s['a1'], s['b1'], s['f1'] = _ffn_fwd(h, vec('ffn1_pre_g', l), vec('ffn1_post_g', l),
                                                wl['ffn1_w_gate'], wl['ffn1_w_up'], wl['ffn1_w_down'])
        s['h1'] = h
        xr, gr, cq, ckv, kr = _mix_in_fwd(h, vec('mix_pre_g', l), wl['w_in'], widths)
        s.update(xr=xr, gr=gr, cq=cq, ckv=ckv)
        y_lru = _rglru_fwd(xr, gr, conv_w[l], vec('lru_conv_b', l), wa_blk[l], wx_blk[l], vec('lru_b_a', l),
                           vec('lru_b_x', l), vec('lru_lambda', l))
        q, k, v = _mla_proj_fwd(cq, ckv, kr, vec('mla_q_norm_g', l), vec('mla_kv_norm_g', l),
                                wl['mla_w_uq'], wl['mla_w_ukv'], *tabs)
        if l + 1 < L:
            nxt = layer_shards(l + 1)
            y_mla, lse, *gathered = _attn_fwd(q, k, v, _Gather(nxt), nxt)
            big.append(assemble(gathered))
        else:
            y_mla, lse = _attn_fwd(q, k, v)
        s.update(q=q, k=k, v=v, lse=lse, y_lru=y_lru, y_mla=y_mla)
        h, s['y'] = _mix_out_fwd(h, y_lru, y_mla, vec('lru_out_g', l), vec('mla_out_g', l), vec('mix_post_g', l),
                                 wl['w_out'])
        s['h2'] = h
        h, s['a2'], s['b2'], s['f2'] = _ffn_fwd(h, vec('ffn2_pre_g', l), vec('ffn2_post_g', l),
                                                wl['ffn2_w_gate'], wl['ffn2_w_up'], wl['ffn2_w_down'])
        saved.append(s)

    dh, loss_tile = _loss_head(h, tpad, n_meta, T)
    loss = lax.psum(loss_tile[0, 0], MESH_AXES)

    gw = {n: [None] * L for n in REPLICATED + ['lru_conv_w']}
    shard2d = {n: a[n].shape[1:] for n in BIG}
    results = {n: [lax.empty(a[n].shape, F32) for _ in range(4)] for n in BIG}

    def ffn_bwd(f, l, gbuf, dh, h_in, a_, b_, f_):
        wl = big[l]
        da, db, act, dfb, dpost = _ffn_bwd_down(dh, a_, b_, f_, vec(f + '_post_g', l), wl[f + '_w_down'])
        dh_in, u, dpre = _ffn_bwd_up(dh, h_in, da, db, vec(f + '_pre_g', l), wl[f + '_w_gate'], wl[f + '_w_up'])
        gbuf[f + '_w_gate'] = _wgrad(u, da, shard2d[f + '_w_gate'], 'cols')
        gbuf[f + '_w_up'] = _wgrad(u, db, shard2d[f + '_w_up'], 'cols')
        gbuf[f + '_w_down'] = _wgrad(act, dfb, shard2d[f + '_w_down'], 'rows')
        gw[f + '_pre_g'][l] = dpre[0]
        gw[f + '_post_g'][l] = dpost[0]
        return dh_in

    def finish_reduce(l, chip_partial, got2):
        for n, cp, g2 in zip(BIG, chip_partial, got2):
            results[n] = _final_sum_adamw(cp, g2, chip, a[n], a['m_' + n], a['v_' + n], results[n], l)

    pending = None
    for l in reversed(range(L)):
        s = saved[l]
        wl = big[l]
        gbuf = {}
        dh = ffn_bwd('ffn2', l, gbuf, dh, s['h2'], s['a2'], s['b2'], s['f2'])
        dyl, dym, dyb, cat, dlg, dmg, dpg = _mix_out_bwd(
            dh, s['y'], s['y_lru'], s['y_mla'], vec('lru_out_g', l), vec('mla_out_g', l), vec('mix_post_g', l),
            wl['w_out'])
        gbuf['w_out'] = _wgrad(cat, dyb, shard2d['w_out'], 'rows')
        gw['lru_out_g'][l], gw['mla_out_g'][l], gw['mix_post_g'][l] = dlg[0], dmg[0], dpg[0]
        dxr, dgr, dcw, dcb, dwa, dwx, dba, dbx, dlam = _rglru_bwd(
            dyl, s['xr'], s['gr'], conv_w[l], vec('lru_conv_b', l), wa_blk[l], wx_blk[l],
            vec('lru_b_a', l), vec('lru_b_x', l), vec('lru_lambda', l))
        gw['lru_conv_w'][l], gw['lru_conv_b'][l] = dcw, dcb[0]
        gw['lru_w_a'][l], gw['lru_w_x'][l] = _block_diag_extract(dwa, LD), _block_diag_extract(dwx, LD)
        gw['lru_b_a'][l], gw['lru_b_x'][l], gw['lru_lambda'][l] = dba[0], dbx[0], dlam[0]
        dq, delta = _attn_bwd_q(s['q'], s['k'], s['v'], dym, s['y_mla'], s['lse'])
        lse_row, delta_row = s['lse'].reshape(H, 1, Tp), delta.reshape(H, 1, Tp)
        if pending is None:
            dk, dv = _attn_bwd_kv(s['q'], s['k'], s['v'], dym, lse_row, delta_row)
        else:
            dk, dv, *got2 = _attn_bwd_kv(s['q'], s['k'], s['v'], dym, lse_row, delta_row,
                                         _Exchange(pending[1]), pending[1])
            finish_reduce(pending[0], pending[1], got2)
        dcq, dckv, dkr, dqa, dkva, cqn, ckvn, dqg, dkg = _mla_proj_bwd(
            dq, dk, dv, s['cq'], s['ckv'], vec('mla_q_norm_g', l), vec('mla_kv_norm_g', l),
            wl['mla_w_uq'], wl['mla_w_ukv'], *tabs)
        gw['mla_q_norm_g'][l], gw['mla_kv_norm_g'][l] = dqg[0], dkg[0]
        gbuf['mla_w_uq'] = _wgrad(cqn, dqa, shard2d['mla_w_uq'], 'heads')
        gbuf['mla_w_ukv'] = _wgrad(ckvn, dkva, shard2d['mla_w_ukv'], 'cols')
        dh, dz, hn, dmg_ = _mix_in_bwd(dh, s['h1'], [dxr, dgr, dcq, dckv, dkr], vec('mix_pre_g', l), wl['w_in'])
        gw['mix_pre_g'][l] = dmg_[0]
        gbuf['w_in'] = _wgrad(hn, dz, shard2d['w_in'], 'w_in', n_in)
        dh = ffn_bwd('ffn1', l, gbuf, dh, s['h0'], s['a1'], s['b1'], s['f1'])
        parts = [gbuf[n] for n in BIG]
        got = _swap_with_sibling(parts)
        pending = (l, [_chip_partial(p, g, core) for p, g in zip(parts, got)])

    finish_reduce(pending[0], pending[1], _exchange_chips(pending[1]))
    grad_x = dh[n_meta:T][None]
    gsmall = {n: jnp.stack(v) for n, v in gw.items()}
    gsmall['meta_tokens'] = dh[:n_meta]
    grads, delta, new_m, new_v = ({n: results[n][i] for n in BIG} for i in range(4))

    small_names = REPLICATED + SMALL_SHARDED
    small_full_shapes = [gsmall[n].shape for n in small_names]
    gsm = _sum_devices(_all_gather([_pack([gsmall[n] for n in small_names])])[0])
    for n, g in zip(small_names, _unpack(gsm, small_full_shapes)):
        if n == 'lru_conv_w':
            g = lax.dynamic_index_in_dim(jnp.moveaxis(g.reshape(L, CONV_WIDTH, N_DEV, W // N_DEV), 2, 0), me, 0, False)
        elif n == 'meta_tokens':
            g = lax.dynamic_index_in_dim(jnp.moveaxis(g.reshape(n_meta, N_DEV, D // N_DEV), 1, 0), me, 0, False)
        grads[n] = g
    for group in (REPLICATED, SMALL_SHARDED):
        shapes = [a[n].shape for n in group]
        packed = [_pack([a[p + n] for n in group]) for p in ('', 'm_', 'v_')]
        packed.append(_pack([grads[n] for n in group]))
        outs = [_unpack(o, shapes) for o in _adamw(*packed)]
        for i, n in enumerate(group):
            delta[n], new_m[n], new_v[n] = outs[0][i], outs[1][i], outs[2][i]

    return (loss, grad_x, *[grads[n] for n in WEIGHTS], *[delta[n] for n in WEIGHTS],
            *[new_m[n] for n in WEIGHTS], *[new_v[n] for n in WEIGHTS])


def kernel(x, meta_tokens, ffn1_pre_g, ffn1_w_gate, ffn1_w_up, ffn1_w_down, ffn1_post_g, mix_pre_g, w_in, lru_conv_w, lru_conv_b, lru_w_a, lru_b_a, lru_w_x, lru_b_x, lru_lambda, mla_q_norm_g, mla_w_uq, mla_kv_norm_g, mla_w_ukv, lru_out_g, mla_out_g, w_out, mix_post_g, ffn2_pre_g, ffn2_w_gate, ffn2_w_up, ffn2_w_down, ffn2_post_g, loss_target, m_meta_tokens, m_ffn1_pre_g, m_ffn1_w_gate, m_ffn1_w_up, m_ffn1_w_down, m_ffn1_post_g, m_mix_pre_g, m_w_in, m_lru_conv_w, m_lru_conv_b, m_lru_w_a, m_lru_b_a, m_lru_w_x, m_lru_b_x, m_lru_lambda, m_mla_q_norm_g, m_mla_w_uq, m_mla_kv_norm_g, m_mla_w_ukv, m_lru_out_g, m_mla_out_g, m_w_out, m_mix_post_g, m_ffn2_pre_g, m_ffn2_w_gate, m_ffn2_w_up, m_ffn2_w_down, m_ffn2_post_g, v_meta_tokens, v_ffn1_pre_g, v_ffn1_w_gate, v_ffn1_w_up, v_ffn1_w_down, v_ffn1_post_g, v_mix_pre_g, v_w_in, v_lru_conv_w, v_lru_conv_b, v_lru_w_a, v_lru_b_a, v_lru_w_x, v_lru_b_x, v_lru_lambda, v_mla_q_norm_g, v_mla_w_uq, v_mla_kv_norm_g, v_mla_w_ukv, v_lru_out_g, v_mla_out_g, v_w_out, v_mix_post_g, v_ffn2_pre_g, v_ffn2_w_gate, v_ffn2_w_up, v_ffn2_w_down, v_ffn2_post_g):
    return _step(dict(locals()))
```

```python
import functools
import math

import jax
import jax.numpy as jnp
from jax import lax
from jax.experimental import pallas as pl
from jax.experimental.pallas import tpu as pltpu

F32 = jnp.float32
BF16 = jnp.bfloat16

EPS = 1e-6
N_DEV = 8
LANES = 128
ROW_TILE = 384
SCAN_CHUNKS = 8
VMEM_LIMIT = 56 * 1024 * 1024
MM_ACC_BYTES = 8 * 1024 * 1024
WGRAD_K_TILE = 1100

LRU_C = 8.0
CONV_WIDTH = 4
MLA_HEADS = 8
QK_NOPE = 64
QK_ROPE = 32
ROPE_THETA = 10000.0

ADAM_LR = 0.001
ADAM_B1 = 0.9
ADAM_B2 = 0.999
ADAM_EPS = 1e-08
ADAM_WD = 0.01
ADAM_STEP = 10

MESH_AXES = ("x", "y", "c")
MESH = pl.DeviceIdType.MESH

WEIGHTS = ['meta_tokens', 'ffn1_pre_g', 'ffn1_w_gate', 'ffn1_w_up', 'ffn1_w_down', 'ffn1_post_g', 'mix_pre_g', 'w_in',
           'lru_conv_w', 'lru_conv_b', 'lru_w_a', 'lru_b_a', 'lru_w_x', 'lru_b_x', 'lru_lambda', 'mla_q_norm_g',
           'mla_w_uq', 'mla_kv_norm_g', 'mla_w_ukv', 'lru_out_g', 'mla_out_g', 'w_out', 'mix_post_g', 'ffn2_pre_g',
           'ffn2_w_gate', 'ffn2_w_up', 'ffn2_w_down', 'ffn2_post_g']
BIG = ['ffn1_w_gate', 'ffn1_w_up', 'ffn1_w_down', 'w_in', 'mla_w_uq', 'mla_w_ukv', 'w_out',
       'ffn2_w_gate', 'ffn2_w_up', 'ffn2_w_down']
TRANSPOSED = ['ffn1_w_gate', 'ffn1_w_up', 'ffn2_w_gate', 'ffn2_w_up']
SMALL_SHARDED = ['lru_conv_w', 'meta_tokens']
REPLICATED = [n for n in WEIGHTS if n not in BIG and n not in SMALL_SHARDED]


def _cparams(**kw):
    return pltpu.CompilerParams(vmem_limit_bytes=VMEM_LIMIT, **kw)


def _resident(shape):
    nd = len(shape)
    return pl.BlockSpec(shape, lambda *_: (0,) * nd, pipeline_mode=pl.Buffered(1))


def _rows(cols, tm=ROW_TILE):
    return pl.BlockSpec((tm, cols), lambda i: (i, 0))


def _const(shape):
    nd = len(shape)
    return pl.BlockSpec(shape, lambda *_: (0,) * nd)


def _dot(a, b):
    return jnp.dot(a, b, preferred_element_type=F32)


def _dot_nt(a, b):
    return lax.dot_general(a, b, (((1,), (1,)), ((), ())), preferred_element_type=F32)


def _dot_tn(a, b):
    return lax.dot_general(a, b, (((0,), (0,)), ((), ())), preferred_element_type=F32)


def _rms_fwd(x, g):
    rinv = lax.rsqrt(jnp.mean(x * x, axis=-1, keepdims=True) + EPS)
    xn = x * rinv
    return xn * g, xn, rinv


def _rms_bwd(xn, rinv, g, dy):
    dxn = dy * g
    dx = rinv * (dxn - xn * jnp.mean(dxn * xn, axis=-1, keepdims=True))
    return dx, jnp.sum(dy * xn, axis=0, keepdims=True)


def _accumulate(ref, val, first):
    @pl.when(first)
    def _():
        ref[...] = val

    @pl.when(jnp.logical_not(first))
    def _():
        ref[...] += val


class _Gather:
    def __init__(self, shards):
        self.n = len(shards)
        self.out_shape = tuple(jax.ShapeDtypeStruct((N_DEV,) + s.shape, s.dtype) for s in shards)
        self.scratch = [pltpu.SemaphoreType.DMA((7 * self.n,)), pltpu.SemaphoreType.DMA((7 * self.n,)),
                        pltpu.SemaphoreType.DMA((self.n,))]

    def bind(self, x_refs, out_refs, sems):
        self.x_refs, self.out_refs = x_refs, out_refs
        self.send_sems, self.recv_sems, self.local_sems = sems
        x, y, c = lax.axis_index("x"), lax.axis_index("y"), lax.axis_index("c")
        self.c = c
        self.me, self.sibling = (x, y, c), (x, y, 1 - c)
        self.chips = [(1 - x, y), (x, 1 - y), (1 - x, 1 - y)]

    def _copy(self, a, k, block, to, own=False):
        dst = self.out_refs[a].at[4 * block[0] + 2 * block[1] + block[2]]
        return pltpu.make_async_remote_copy(
            src_ref=self.x_refs[a] if own else dst, dst_ref=dst,
            send_sem=self.send_sems.at[7 * a + k], recv_sem=self.recv_sems.at[7 * a + k],
            device_id=to, device_id_type=MESH)

    def _mine(self):
        return [pltpu.make_async_copy(self.x_refs[a], self.out_refs[a].at[4 * self.me[0] + 2 * self.me[1] + self.c],
                                      self.local_sems.at[a]) for a in range(self.n)]

    def _first(self):
        first = []
        for a in range(self.n):
            first.append(self._copy(a, 0, self.me, self.sibling, own=True))
            first += [self._copy(a, 1 + j, self.me, (*chip, self.c), own=True) for j, chip in enumerate(self.chips)]
        return first

    def start(self):
        for cp in self._mine() + self._first():
            cp.start()

    def finish(self):
        c, n = self.c, self.n
        passed = []
        for j, chip in enumerate(self.chips):
            for a in range(n):
                self._copy(a, 1 + j, (*chip, c), self.me).wait_recv()
                cp = self._copy(a, 4 + j, (*chip, c), self.sibling)
                cp.start()
                passed.append(cp)
        for a in range(n):
            self._copy(a, 0, self.sibling, self.me).wait_recv()
        for j, chip in enumerate(self.chips):
            for a in range(n):
                self._copy(a, 4 + j, (*chip, 1 - c), self.me).wait_recv()
        for cp in self._first() + passed:
            cp.wait_send()
        for cp in self._mine():
            cp.wait()


class _Exchange:
    def __init__(self, parts):
        self.n = len(parts)
        self.out_shape = tuple(jax.ShapeDtypeStruct((3,) + p.shape[1:], p.dtype) for p in parts)
        self.scratch = [pltpu.SemaphoreType.DMA((3 * self.n,)), pltpu.SemaphoreType.DMA((3 * self.n,))]

    def bind(self, p_refs, out_refs, sems):
        self.p_refs, self.out_refs = p_refs, out_refs
        self.send_sems, self.recv_sems = sems

    def _copies(self):
        x, y, c = lax.axis_index("x"), lax.axis_index("y"), lax.axis_index("c")
        copies = []
        for a in range(self.n):
            for k, (tx, ty) in enumerate([(1 - x, y), (x, 1 - y), (1 - x, 1 - y)]):
                copies.append(pltpu.make_async_remote_copy(
                    src_ref=self.p_refs[a].at[2 * tx + ty], dst_ref=self.out_refs[a].at[k],
                    send_sem=self.send_sems.at[3 * a + k], recv_sem=self.recv_sems.at[3 * a + k],
                    device_id=(tx, ty, c), device_id_type=MESH))
        return copies

    def start(self):
        for cp in self._copies():
            cp.start()

    def finish(self):
        for cp in self._copies():
            cp.wait()


_HBM = pl.BlockSpec(memory_space=pl.ANY)


def _comm_call(comm, arrays, name):
    n = comm.n

    def body(*refs):
        comm.bind(refs[:n], refs[n:2 * n], refs[2 * n:])
        comm.start()
        comm.finish()

    return pl.pallas_call(
        body, name=name, out_shape=comm.out_shape,
        in_specs=[_HBM] * n, out_specs=[_HBM] * n, scratch_shapes=comm.scratch,
    )(*arrays)


def _all_gather(shards):
    return _comm_call(_Gather(shards), shards, "all_gather")


def _exchange_chips(parts):
    return _comm_call(_Exchange(parts), parts, "rs_chips")


def _swap_with_sibling(parts):
    n = len(parts)

    def body(*refs):
        p_refs, out_refs = refs[:n], refs[n:2 * n]
        send_sems, recv_sems = refs[2 * n:]
        x, y, c = lax.axis_index("x"), lax.axis_index("y"), lax.axis_index("c")
        copies = []
        for a in range(n):
            for k in range(4):
                cp = pltpu.make_async_remote_copy(
                    src_ref=p_refs[a].at[2 * k + (1 - c)], dst_ref=out_refs[a].at[k],
                    send_sem=send_sems.at[4 * a + k], recv_sem=recv_sems.at[4 * a + k],
                    device_id=(x, y, 1 - c), device_id_type=MESH)
                cp.start()
                copies.append(cp)
        for cp in copies:
            cp.wait()

    hbm = pl.BlockSpec(memory_space=pl.ANY)
    return pl.pallas_call(
        body, name="rs_sibling",
        out_shape=tuple(jax.ShapeDtypeStruct((4,) + p.shape[1:], p.dtype) for p in parts),
        in_specs=[hbm] * n, out_specs=[hbm] * n,
        scratch_shapes=[pltpu.SemaphoreType.DMA((4 * n,)), pltpu.SemaphoreType.DMA((4 * n,))],
    )(*parts)


def _row_block(rows, cap=2048):
    best = 8
    for t in range(8, min(rows, cap) + 1, 8):
        if rows % t == 0:
            best = t
    return best


def _elementwise_rows(rows, cols, elems=1 << 18):
    return _row_block(rows, cap=max(8, elems // cols // 8 * 8))


def _chip_partial(parts, got, core):
    _, rows, cols = got.shape
    tr = _elementwise_rows(rows, cols, 1 << 19)
    p4 = parts.reshape(4, 2, rows, cols)

    def body(c_ref, a_ref, b_ref, o_ref):
        o_ref[...] = (a_ref[...].astype(F32) + b_ref[...].astype(F32)).astype(o_ref.dtype)

    return pl.pallas_call(
        body, name="rs_chip_partial",
        out_shape=jax.ShapeDtypeStruct((4, rows, cols), parts.dtype),
        grid_spec=pltpu.PrefetchScalarGridSpec(
            num_scalar_prefetch=1, grid=(4, rows // tr),
            in_specs=[pl.BlockSpec((None, None, tr, cols), lambda k, r, c: (k, c[0], r, 0)),
                      pl.BlockSpec((None, tr, cols), lambda k, r, c: (k, r, 0))],
            out_specs=pl.BlockSpec((None, tr, cols), lambda k, r, c: (k, r, 0))),
        compiler_params=_cparams(),
    )(core, p4, got)


def _adamw_math(w, m, v, g):
    mm = ADAM_B1 * m + (1.0 - ADAM_B1) * g
    vv = ADAM_B2 * v + (1.0 - ADAM_B2) * jnp.square(g)
    m_hat = mm / (1.0 - ADAM_B1 ** ADAM_STEP)
    v_hat = vv / (1.0 - ADAM_B2 ** ADAM_STEP)
    return -ADAM_LR * (m_hat / (jnp.sqrt(v_hat) + ADAM_EPS) + ADAM_WD * w), mm, vv


def _final_sum_adamw(partial, got, chip, w, m, v, bufs, layer):
    _, rows, cols = got.shape
    tr = _elementwise_rows(rows, cols)

    def body(c_ref, a_ref, b_ref, w_ref, m_ref, v_ref, *rest):
        g_ref, d_ref, nm_ref, nv_ref = rest[4:]
        g = ((a_ref[...].astype(F32) + b_ref[0].astype(F32)) + b_ref[1].astype(F32)) + b_ref[2].astype(F32)
        g_ref[...] = g
        d_ref[...], nm_ref[...], nv_ref[...] = _adamw_math(w_ref[...], m_ref[...], v_ref[...], g)

    one_layer = pl.BlockSpec((None, tr, cols), lambda r, c: (layer, r, 0))
    return pl.pallas_call(
        body, name="rs_final_adamw",
        out_shape=tuple(jax.ShapeDtypeStruct(b.shape, F32) for b in bufs),
        grid_spec=pltpu.PrefetchScalarGridSpec(
            num_scalar_prefetch=1, grid=(rows // tr,),
            in_specs=[pl.BlockSpec((None, tr, cols), lambda r, c: (c[0], r, 0)),
                      pl.BlockSpec((3, tr, cols), lambda r, c: (0, r, 0)), one_layer, one_layer, one_layer] + [_HBM] * 4,
            out_specs=[one_layer] * 4),
        input_output_aliases={6: 0, 7: 1, 8: 2, 9: 3},
        compiler_params=_cparams(),
    )(chip, partial, got, w, m, v, *bufs)


def _sum_devices(gathered):
    _, rows, lanes = gathered.shape
    tr = _row_block(rows)

    def body(g_ref, o_ref):
        acc = g_ref[0]
        for j in range(1, N_DEV):
            acc = acc + g_ref[j]
        o_ref[...] = acc

    return pl.pallas_call(
        body, name="small_grad_sum",
        out_shape=jax.ShapeDtypeStruct((rows, lanes), F32),
        grid=(rows // tr,),
        in_specs=[pl.BlockSpec((N_DEV, tr, lanes), lambda r: (0, r, 0))],
        out_specs=pl.BlockSpec((tr, lanes), lambda r: (r, 0)),
        compiler_params=_cparams(),
    )(gathered)


def _assemble(g, kind, n_in=0):
    _, A, B = g.shape
    if kind == 'cols':
        oshape = (A, N_DEV * B)
    elif kind == 'rows':
        oshape = (N_DEV * A, B)
    elif kind == 'w_in':
        oshape = (A, n_in + LANES)
    else:
        oshape = (A, N_DEV * LANES)

    def body(x_ref, o_ref):
        if kind == 'cols':
            for j in range(N_DEV):
                o_ref[:, j * B:(j + 1) * B] = x_ref[j]
        elif kind == 'rows':
            for j in range(N_DEV):
                o_ref[j * A:(j + 1) * A, :] = x_ref[j]
        elif kind == 'w_in':
            nat = jnp.concatenate([x_ref[j] for j in range(N_DEV)], axis=1)
            o_ref[:, 0:n_in] = nat[:, 0:n_in]
            o_ref[:, n_in:] = jnp.concatenate(
                [jnp.zeros((A, QK_NOPE), g.dtype), nat[:, n_in:],
                 jnp.zeros((A, LANES - QK_NOPE - QK_ROPE), g.dtype)], axis=1)
        else:
            o_ref[...] = jnp.zeros(oshape, g.dtype)
            for j in range(N_DEV):
                o_ref[:, j * LANES:j * LANES + B] = x_ref[j]

    return pl.pallas_call(
        body, name="assemble_" + kind,
        out_shape=jax.ShapeDtypeStruct(oshape, g.dtype),
        compiler_params=_cparams(),
    )(g)


def _wgrad(a, b, shard, kind, n_in=0):
    K, M = a.shape
    N = b.shape[1]
    shard = tuple(shard)
    tk = ROW_TILE
    for cand in range(ROW_TILE, WGRAD_K_TILE + 1, 16):
        if K % cand == 0:
            tk = cand
    nk = K // tk
    if kind == 'cols':
        S = shard[1]
        tn = S
        for t in range(S, N + 1, S):
            if N % t == 0 and t % LANES == 0 and M * t * 4 <= MM_ACC_BYTES:
                tn = t
        oblock = pl.BlockSpec((tn // S,) + shard, lambda n, k: (n, 0, 0))
    elif kind == 'rows':
        tn = LANES
        for t in range(LANES, N + 1, LANES):
            if N % t == 0 and M * t * 4 <= MM_ACC_BYTES:
                tn = t
        oblock = pl.BlockSpec((N_DEV, shard[0], tn), lambda n, k: (0, 0, n))
    else:
        tn = N
        oblock = pl.BlockSpec((N_DEV,) + shard, lambda n, k: (0, 0, 0))

    def body(a_ref, b_ref, o_ref, acc_ref):
        k = pl.program_id(1)

        @pl.when(k == 0)
        def _():
            acc_ref[...] = jnp.zeros_like(acc_ref)

        acc_ref[...] += _dot_tn(a_ref[...], b_ref[...])

        @pl.when(k == nk - 1)
        def _():
            acc = acc_ref[...]
            if kind == 'cols':
                S = shard[1]
                for j in range(tn // S):
                    o_ref[j] = acc[:, j * S:(j + 1) * S].astype(o_ref.dtype)
            elif kind == 'rows':
                S = shard[0]
                for j in range(N_DEV):
                    o_ref[j] = acc[j * S:(j + 1) * S, :].astype(o_ref.dtype)
            elif kind == 'w_in':
                S = shard[1]
                nat = jnp.concatenate([acc[:, 0:n_in], acc[:, n_in + QK_NOPE:n_in + QK_NOPE + QK_ROPE]], axis=1)
                for j in range(N_DEV):
                    o_ref[j] = nat[:, j * S:(j + 1) * S].astype(o_ref.dtype)
            else:
                S = shard[1]
                for j in range(N_DEV):
                    o_ref[j] = acc[:, j * LANES:j * LANES + S].astype(o_ref.dtype)

    return pl.pallas_call(
        body, name="wgrad_" + kind,
        out_shape=jax.ShapeDtypeStruct((N_DEV,) + shard, BF16),
        grid=(N // tn, nk),
        in_specs=[pl.BlockSpec((tk, M), lambda n, k: (k, 0)), pl.BlockSpec((tk, tn), lambda n, k: (k, n))],
        out_specs=oblock,
        scratch_shapes=[pltpu.VMEM((M, tn), F32)],
        compiler_params=_cparams(),
    )(a, b)


def _adamw(w, m, v, g):
    rows, cols = w.shape
    tr = _elementwise_rows(rows, cols) if rows % 8 == 0 else rows

    def body(w_ref, m_ref, v_ref, g_ref, d_ref, nm_ref, nv_ref):
        d_ref[...], nm_ref[...], nv_ref[...] = _adamw_math(w_ref[...], m_ref[...], v_ref[...], g_ref[...])

    spec = pl.BlockSpec((tr, cols), lambda r: (r, 0))
    return pl.pallas_call(
        body, name="adamw",
        out_shape=(jax.ShapeDtypeStruct((rows, cols), F32),) * 3,
        grid=(rows // tr,),
        in_specs=[spec] * 4, out_specs=[spec] * 3,
        compiler_params=_cparams(),
    )(w, m, v, g)


def _hidden_chunk(F):
    return F // 2 if (F // 2) % LANES == 0 else F


def _sigmoid(x):
    return 0.5 * jnp.tanh(0.5 * x) + 0.5


def _ffn_fwd(h, pre_g, post_g, wgT, wuT, wd):
    Tp, D = h.shape
    F = wd.shape[0]
    fc = _hidden_chunk(F)

    def body(h_ref, pg_ref, qg_ref, wg_ref, wu_ref, wd_ref, ho_ref, a_ref, b_ref, f_ref):
        hx = h_ref[...]
        u, _, _ = _rms_fwd(hx, pg_ref[...])
        ub = u.astype(BF16)
        f = jnp.zeros((ROW_TILE, D), F32)
        for c in range(F // fc):
            sl = slice(c * fc, (c + 1) * fc)
            a = _dot_nt(ub, wg_ref[sl, :])
            b = _dot_nt(ub, wu_ref[sl, :])
            a_ref[:, sl] = a.astype(BF16)
            b_ref[:, sl] = b.astype(BF16)
            act = (a * _sigmoid(a) * b).astype(BF16)
            f = f + _dot(act, wd_ref[sl, :])
        f_ref[...] = f
        y, _, _ = _rms_fwd(f, qg_ref[...])
        ho_ref[...] = hx + 0.5 * y

    return pl.pallas_call(
        body, name="ffn_fwd",
        out_shape=(jax.ShapeDtypeStruct((Tp, D), F32), jax.ShapeDtypeStruct((Tp, F), BF16),
                   jax.ShapeDtypeStruct((Tp, F), BF16), jax.ShapeDtypeStruct((Tp, D), F32)),
        grid=(Tp // ROW_TILE,),
        in_specs=[_rows(D), _const((1, D)), _const((1, D)), _resident((F, D)), _resident((F, D)), _resident((F, D))],
        out_specs=[_rows(D), _rows(F), _rows(F), _rows(D)],
        compiler_params=_cparams(),
    )(h, pre_g, post_g, wgT, wuT, wd)


def _ffn_bwd_down(dh, a, b, f, post_g, wd):
    Tp, D = dh.shape
    F = a.shape[1]
    fc = _hidden_chunk(F)

    def body(dh_ref, a_ref, b_ref, f_ref, qg_ref, wd_ref, da_ref, db_ref, act_ref, df_ref, dqg_ref):
        qg = qg_ref[...]
        _, fn, frinv = _rms_fwd(f_ref[...], qg)
        df, dqg = _rms_bwd(fn, frinv, qg, 0.5 * dh_ref[...])
        dfb = df.astype(BF16)
        df_ref[...] = dfb
        for c in range(F // fc):
            sl = slice(c * fc, (c + 1) * fc)
            dact = _dot_nt(dfb, wd_ref[sl, :])
            av = a_ref[:, sl].astype(F32)
            bv = b_ref[:, sl].astype(F32)
            sig = _sigmoid(av)
            silu = av * sig
            act_ref[:, sl] = (silu * bv).astype(BF16)
            da_ref[:, sl] = ((dact * bv) * (sig + silu * (1.0 - sig))).astype(BF16)
            db_ref[:, sl] = (dact * silu).astype(BF16)
        _accumulate(dqg_ref, dqg, pl.program_id(0) == 0)

    return pl.pallas_call(
        body, name="ffn_bwd_down",
        out_shape=(jax.ShapeDtypeStruct((Tp, F), BF16), jax.ShapeDtypeStruct((Tp, F), BF16),
                   jax.ShapeDtypeStruct((Tp, F), BF16), jax.ShapeDtypeStruct((Tp, D), BF16),
                   jax.ShapeDtypeStruct((1, D), F32)),
        grid=(Tp // ROW_TILE,),
        in_specs=[_rows(D), _rows(F), _rows(F), _rows(D), _const((1, D)), _resident((F, D))],
        out_specs=[_rows(F), _rows(F), _rows(F), _rows(D), _const((1, D))],
        compiler_params=_cparams(),
    )(dh, a, b, f, post_g, wd)


def _ffn_bwd_up(dh, h, da, db, pre_g, wgT, wuT):
    Tp, D = dh.shape
    F = da.shape[1]

    def body(dh_ref, h_ref, da_ref, db_ref, pg_ref, wg_ref, wu_ref, dhi_ref, u_ref, dpg_ref):
        pg = pg_ref[...]
        u, hn, hrinv = _rms_fwd(h_ref[...], pg)
        u_ref[...] = u.astype(BF16)
        du = _dot(da_ref[...], wg_ref[...]) + _dot(db_ref[...], wu_ref[...])
        dx, dpg = _rms_bwd(hn, hrinv, pg, du)
        dhi_ref[...] = dh_ref[...] + dx
        _accumulate(dpg_ref, dpg, pl.program_id(0) == 0)

    return pl.pallas_call(
        body, name="ffn_bwd_up",
        out_shape=(jax.ShapeDtypeStruct((Tp, D), F32), jax.ShapeDtypeStruct((Tp, D), BF16),
                   jax.ShapeDtypeStruct((1, D), F32)),
        grid=(Tp // ROW_TILE,),
        in_specs=[_rows(D), _rows(D), _rows(F), _rows(F), _const((1, D)), _resident((F, D)), _resident((F, D))],
        out_specs=[_rows(D), _rows(D), _const((1, D))],
        compiler_params=_cparams(),
    )(dh, h, da, db, pre_g, wgT, wuT)


def _mix_in_fwd(h, g, win, widths):
    Tp, D = h.shape
    ncol = win.shape[1]
    offs = [0]
    for wd_ in widths:
        offs.append(offs[-1] + wd_)

    def body(h_ref, g_ref, w_ref, *out_refs):
        y, _, _ = _rms_fwd(h_ref[...], g_ref[...])
        z = _dot(y.astype(BF16), w_ref[...])
        for o_ref, lo, wd_ in zip(out_refs, offs, widths):
            o_ref[...] = z[:, lo:lo + wd_]

    return pl.pallas_call(
        body, name="mix_in_fwd",
        out_shape=tuple(jax.ShapeDtypeStruct((Tp, wd_), F32) for wd_ in widths),
        grid=(Tp // ROW_TILE,),
        in_specs=[_rows(D), _const((1, D)), _resident((D, ncol))],
        out_specs=[_rows(wd_) for wd_ in widths],
        compiler_params=_cparams(),
    )(h, g, win)


def _mix_in_bwd(dres, h, dparts, g, win):
    Tp, D = h.shape
    ncol = win.shape[1]
    widths = [p.shape[1] for p in dparts]
    n = len(dparts)

    def body(*refs):
        dres_ref, h_ref = refs[0], refs[1]
        part_refs = refs[2:2 + n]
        g_ref, w_ref = refs[2 + n], refs[3 + n]
        dh_ref, dz_ref, u_ref, dg_ref = refs[4 + n:]
        dz = jnp.concatenate([r[...] for r in part_refs], axis=1).astype(BF16)
        dz_ref[...] = dz
        gg = g_ref[...]
        u, hn, rinv = _rms_fwd(h_ref[...], gg)
        u_ref[...] = u.astype(BF16)
        dx, dg = _rms_bwd(hn, rinv, gg, _dot_nt(dz, w_ref[...]))
        dh_ref[...] = dres_ref[...] + dx
        _accumulate(dg_ref, dg, pl.program_id(0) == 0)

    return pl.pallas_call(
        body, name="mix_in_bwd",
        out_shape=(jax.ShapeDtypeStruct((Tp, D), F32), jax.ShapeDtypeStruct((Tp, ncol), BF16),
                   jax.ShapeDtypeStruct((Tp, D), BF16), jax.ShapeDtypeStruct((1, D), F32)),
        grid=(Tp // ROW_TILE,),
        in_specs=[_rows(D), _rows(D)] + [_rows(wd_) for wd_ in widths] + [_const((1, D)), _resident((D, ncol))],
        out_specs=[_rows(D), _rows(ncol), _rows(D), _const((1, D))],
        compiler_params=_cparams(),
    )(dres, h, *dparts, g, win)


def _mix_out_fwd(h, y_lru, y_mla, lru_g, mla_g, post_g, wout):
    Tp, D = h.shape
    W = y_lru.shape[1]

    def body(h_ref, yl_ref, ym_ref, lg_ref, mg_ref, pg_ref, w_ref, ho_ref, y_ref):
        yl, _, _ = _rms_fwd(yl_ref[...], lg_ref[...])
        ym, _, _ = _rms_fwd(ym_ref[...], mg_ref[...])
        y = _dot(yl.astype(BF16), w_ref[0:W, :]) + _dot(ym.astype(BF16), w_ref[W:, :])
        y_ref[...] = y
        yn, _, _ = _rms_fwd(y, pg_ref[...])
        ho_ref[...] = h_ref[...] + yn

    return pl.pallas_call(
        body, name="mix_out_fwd",
        out_shape=(jax.ShapeDtypeStruct((Tp, D), F32), jax.ShapeDtypeStruct((Tp, D), F32)),
        grid=(Tp // ROW_TILE,),
        in_specs=[_rows(D), _rows(W), _rows(W), _const((1, W)), _const((1, W)), _const((1, D)), _resident(wout.shape)],
        out_specs=[_rows(D), _rows(D)],
        compiler_params=_cparams(),
    )(h, y_lru, y_mla, lru_g, mla_g, post_g, wout)


def _mix_out_bwd(dh, y, y_lru, y_mla, lru_g, mla_g, post_g, wout):
    Tp, D = dh.shape
    W = y_lru.shape[1]

    def body(dh_ref, y_ref, yl_ref, ym_ref, lg_ref, mg_ref, pg_ref, w_ref,
             dyl_ref, dym_ref, dy_ref, cat_ref, dlg_ref, dmg_ref, dpg_ref):
        first = pl.program_id(0) == 0
        pg, lg, mg = pg_ref[...], lg_ref[...], mg_ref[...]
        _, yn, yrinv = _rms_fwd(y_ref[...], pg)
        dy, dpg = _rms_bwd(yn, yrinv, pg, dh_ref[...])
        dyb = dy.astype(BF16)
        dy_ref[...] = dyb
        dcat = _dot_nt(dyb, w_ref[...])
        yl, yln, ylr = _rms_fwd(yl_ref[...], lg)
        ym, ymn, ymr = _rms_fwd(ym_ref[...], mg)
        cat_ref[:, 0:W] = yl.astype(BF16)
        cat_ref[:, W:] = ym.astype(BF16)
        dyl, dlg = _rms_bwd(yln, ylr, lg, dcat[:, 0:W])
        dym, dmg = _rms_bwd(ymn, ymr, mg, dcat[:, W:])
        dyl_ref[...] = dyl
        dym_ref[...] = dym
        _accumulate(dlg_ref, dlg, first)
        _accumulate(dmg_ref, dmg, first)
        _accumulate(dpg_ref, dpg, first)

    return pl.pallas_call(
        body, name="mix_out_bwd",
        out_shape=(jax.ShapeDtypeStruct((Tp, W), F32), jax.ShapeDtypeStruct((Tp, W), F32),
                   jax.ShapeDtypeStruct((Tp, D), BF16), jax.ShapeDtypeStruct((Tp, 2 * W), BF16),
                   jax.ShapeDtypeStruct((1, W), F32), jax.ShapeDtypeStruct((1, W), F32),
                   jax.ShapeDtypeStruct((1, D), F32)),
        grid=(Tp // ROW_TILE,),
        in_specs=[_rows(D), _rows(D), _rows(W), _rows(W), _const((1, W)), _const((1, W)), _const((1, D)),
                  _resident(wout.shape)],
        out_specs=[_rows(W), _rows(W), _rows(D), _rows(2 * W), _const((1, W)), _const((1, W)), _const((1, D))],
        compiler_params=_cparams(),
    )(dh, y, y_lru, y_mla, lru_g, mla_g, post_g, wout)


def _softplus_neg(lam):
    return jnp.maximum(-lam, 0.0) + jnp.log1p(jnp.exp(-jnp.abs(lam)))


def _neg_expm1(y):
    series = -y * (1.0 + 0.5 * y * (1.0 + (1.0 / 3.0) * y * (1.0 + 0.25 * y)))
    return jnp.where(y > -0.01, series, 1.0 - jnp.exp(y))


_GELU_K = math.sqrt(2.0 / math.pi)


def _gelu(x):
    return 0.5 * x * (1.0 + jnp.tanh(_GELU_K * (x + 0.044715 * (x * x * x))))


def _gelu_grad(x):
    t = jnp.tanh(_GELU_K * (x + 0.044715 * (x * x * x)))
    return 0.5 * (1.0 + t) + 0.5 * x * (1.0 - t * t) * (_GELU_K * (1.0 + 3.0 * 0.044715 * (x * x)))


def _lru_conv(xpad_ref, lo, n, cw, cb):
    xc = xpad_ref[pl.ds(8 + lo, n), :] * cw[3:4, :] + cb
    for k in range(CONV_WIDTH - 1):
        xc = xc + xpad_ref[pl.ds(8 + lo - (CONV_WIDTH - 1 - k), n), :] * cw[k:k + 1, :]
    return xc


def _lru_gates(xc, wa, wx, ba, bx, sp):
    xb = xc.astype(BF16)
    r = _sigmoid(_dot(xb, wa) + ba)
    i = _sigmoid(_dot(xb, wx) + bx)
    la = (-LRU_C * r) * sp
    a = jnp.exp(la)
    mult = jnp.sqrt(_neg_expm1(2.0 * la))
    return r, i, a, mult


def _scan_carries(last_h, last_p, reverse):
    row = lax.broadcasted_iota(jnp.int32, last_h.shape, 0)
    carry = jnp.zeros_like(last_h)
    for _ in range(SCAN_CHUNKS - 1):
        nxt = last_h + last_p * carry
        if reverse:
            carry = jnp.where(row < SCAN_CHUNKS - 1, pltpu.roll(nxt, SCAN_CHUNKS - 1, axis=0), 0.0)
        else:
            carry = jnp.where(row > 0, pltpu.roll(nxt, 1, axis=0), 0.0)
    return carry


def _lru_forward_scan(a_ref, h_ref, p_ref, a_off, h_off, rc):
    zero = jnp.zeros((SCAN_CHUNKS, LANES), F32)

    def step(i, carry):
        hh, pp = carry
        av = a_ref[pl.ds(a_off + i, SCAN_CHUNKS, stride=rc), :]
        hh = av * hh + h_ref[pl.ds(h_off + i, SCAN_CHUNKS, stride=rc), :]
        pp = av * pp
        h_ref[pl.ds(h_off + i, SCAN_CHUNKS, stride=rc), :] = hh
        p_ref[pl.ds(i, SCAN_CHUNKS, stride=rc), :] = pp
        return hh, pp

    last_h, last_p = lax.fori_loop(0, rc, step, (zero, zero + 1.0))
    carry = _scan_carries(last_h, last_p, reverse=False)
    for c in range(SCAN_CHUNKS):
        rows = pl.ds(h_off + c * rc, rc)
        h_ref[rows, :] = h_ref[rows, :] + p_ref[pl.ds(c * rc, rc), :] * carry[c:c + 1, :]


def _rglru_fwd(xr, gr, cw, cb, wa, wx, ba, bx, lam):
    Tp, W = xr.shape
    nb = W // LANES
    rc = Tp // SCAN_CHUNKS

    def body(xr_ref, gr_ref, cw_ref, cb_ref, wa_ref, wx_ref, ba_ref, bx_ref, lam_ref, y_ref,
             xpad, a_s, h_s, p_s):
        xpad[0:8, :] = jnp.zeros((8, LANES), F32)
        xpad[pl.ds(8, Tp), :] = xr_ref[...]
        cw_, cb_ = cw_ref[...], cb_ref[...]
        sp = _softplus_neg(lam_ref[...])
        for c in range(SCAN_CHUNKS):
            xc = _lru_conv(xpad, c * rc, rc, cw_, cb_)
            _, i, a, mult = _lru_gates(xc, wa_ref[...], wx_ref[...], ba_ref[...], bx_ref[...], sp)
            a_s[pl.ds(c * rc, rc), :] = a
            h_s[pl.ds(c * rc, rc), :] = mult * (i * xc)
        _lru_forward_scan(a_s, h_s, p_s, 0, 0, rc)
        for c in range(SCAN_CHUNKS):
            rows = pl.ds(c * rc, rc)
            y_ref[rows, :] = h_s[rows, :] * _gelu(gr_ref[rows, :])

    col = pl.BlockSpec((Tp, LANES), lambda j: (0, j))
    vec = pl.BlockSpec((1, LANES), lambda j: (0, j))
    mat = pl.BlockSpec((None, LANES, LANES), lambda j: (j, 0, 0))
    return pl.pallas_call(
        body, name="rglru_fwd",
        out_shape=jax.ShapeDtypeStruct((Tp, W), F32),
        grid=(nb,),
        in_specs=[col, col, pl.BlockSpec((CONV_WIDTH, LANES), lambda j: (0, j)), vec, mat, mat, vec, vec, vec],
        out_specs=col,
        scratch_shapes=[pltpu.VMEM((Tp + 8, LANES), F32), pltpu.VMEM((Tp, LANES), F32),
                        pltpu.VMEM((Tp, LANES), F32), pltpu.VMEM((Tp, LANES), F32)],
        compiler_params=_cparams(),
    )(xr, gr, cw, cb, wa, wx, ba, bx, lam)


def _rglru_bwd(dy, xr, gr, cw, cb, wa, wx, ba, bx, lam):
    Tp, W = xr.shape
    nb = W // LANES
    rc = Tp // SCAN_CHUNKS

    def body(dy_ref, xr_ref, gr_ref, cw_ref, cb_ref, wa_ref, wx_ref, ba_ref, bx_ref, lam_ref,
             dxr_ref, dgr_ref, dcw_ref, dcb_ref, dwa_ref, dwx_ref, dba_ref, dbx_ref, dlam_ref,
             xpad, a_s, h_s, p_s, xc_s, g_s, dxc_s):
        zeros8 = jnp.zeros((8, LANES), F32)
        xpad[0:8, :] = zeros8
        xpad[pl.ds(8, Tp), :] = xr_ref[...]
        a_s[pl.ds(Tp, 8), :] = zeros8
        h_s[0:8, :] = zeros8
        dxc_s[pl.ds(Tp, 8), :] = zeros8
        cw_, cb_ = cw_ref[...], cb_ref[...]
        lam_ = lam_ref[...]
        sp = _softplus_neg(lam_)
        gate_args = (wa_ref[...], wx_ref[...], ba_ref[...], bx_ref[...], sp)

        for c in range(SCAN_CHUNKS):
            rows = pl.ds(c * rc, rc)
            xc = _lru_conv(xpad, c * rc, rc, cw_, cb_)
            xc_s[rows, :] = xc
            _, i, a, mult = _lru_gates(xc, *gate_args)
            a_s[rows, :] = a
            h_s[pl.ds(8 + c * rc, rc), :] = mult * (i * xc)
        _lru_forward_scan(a_s, h_s, p_s, 0, 8, rc)

        for c in range(SCAN_CHUNKS):
            rows = pl.ds(c * rc, rc)
            dyv, grv = dy_ref[rows, :], gr_ref[rows, :]
            g_s[rows, :] = dyv * _gelu(grv)
            dgr_ref[rows, :] = dyv * h_s[pl.ds(8 + c * rc, rc), :] * _gelu_grad(grv)

        zero = jnp.zeros((SCAN_CHUNKS, LANES), F32)

        def rstep(k, carry):
            gg, qq = carry
            i = rc - 1 - k
            av = a_s[pl.ds(i + 1, SCAN_CHUNKS, stride=rc), :]
            gg = g_s[pl.ds(i, SCAN_CHUNKS, stride=rc), :] + av * gg
            qq = av * qq
            g_s[pl.ds(i, SCAN_CHUNKS, stride=rc), :] = gg
            p_s[pl.ds(i, SCAN_CHUNKS, stride=rc), :] = qq
            return gg, qq

        first_g, first_q = lax.fori_loop(0, rc, rstep, (zero, zero + 1.0))
        carry = _scan_carries(first_g, first_q, reverse=True)
        for c in range(SCAN_CHUNKS):
            rows = pl.ds(c * rc, rc)
            g_s[rows, :] = g_s[rows, :] + p_s[rows, :] * carry[c:c + 1, :]

        dwa = jnp.zeros((LANES, LANES), F32)
        dwx = jnp.zeros((LANES, LANES), F32)
        dba = jnp.zeros((1, LANES), F32)
        dbx = jnp.zeros((1, LANES), F32)
        dsp = jnp.zeros((1, LANES), F32)
        for c in range(SCAN_CHUNKS):
            rows = pl.ds(c * rc, rc)
            xc = xc_s[rows, :]
            r, i, a, mult = _lru_gates(xc, *gate_args)
            gg = g_s[rows, :]
            da = gg * h_s[pl.ds(7 + c * rc, rc), :]
            d_s = gg * mult
            dmult = gg * (i * xc)
            dla = da * a - dmult * (a * a) / mult
            dr = dla * (-LRU_C * sp)
            dsp = dsp + jnp.sum(dla * (-LRU_C * r), axis=0, keepdims=True)
            dpr = dr * r * (1.0 - r)
            dpi = (d_s * xc) * i * (1.0 - i)
            dprb, dpib, xcb = dpr.astype(BF16), dpi.astype(BF16), xc.astype(BF16)
            dxc_s[rows, :] = d_s * i + _dot_nt(dprb, wa_ref[...]) + _dot_nt(dpib, wx_ref[...])
            dwa = dwa + _dot_tn(xcb, dprb)
            dwx = dwx + _dot_tn(xcb, dpib)
            dba = dba + jnp.sum(dpr, axis=0, keepdims=True)
            dbx = dbx + jnp.sum(dpi, axis=0, keepdims=True)
        dwa_ref[...] = dwa
        dwx_ref[...] = dwx
        dba_ref[...] = dba
        dbx_ref[...] = dbx
        dlam_ref[...] = dsp * (-jax.nn.sigmoid(-lam_))

        dcw = [jnp.zeros((1, LANES), F32) for _ in range(CONV_WIDTH)]
        dcb = jnp.zeros((1, LANES), F32)
        for c in range(SCAN_CHUNKS):
            rows = pl.ds(c * rc, rc)
            dxc = dxc_s[rows, :]
            dcb = dcb + jnp.sum(dxc, axis=0, keepdims=True)
            dxr = dxc * cw_[3:4, :]
            for k in range(CONV_WIDTH):
                back = CONV_WIDTH - 1 - k
                dcw[k] = dcw[k] + jnp.sum(dxc * xpad[pl.ds(8 + c * rc - back, rc), :], axis=0, keepdims=True)
                if back:
                    dxr = dxr + dxc_s[pl.ds(c * rc + back, rc), :] * cw_[k:k + 1, :]
            dxr_ref[rows, :] = dxr
        dcw_ref[...] = jnp.concatenate(dcw, axis=0)
        dcb_ref[...] = dcb

    col = pl.BlockSpec((Tp, LANES), lambda j: (0, j))
    vec = pl.BlockSpec((1, LANES), lambda j: (0, j))
    mat = pl.BlockSpec((None, LANES, LANES), lambda j: (j, 0, 0))
    cwspec = pl.BlockSpec((CONV_WIDTH, LANES), lambda j: (0, j))
    return pl.pallas_call(
        body, name="rglru_bwd",
        out_shape=(jax.ShapeDtypeStruct((Tp, W), F32), jax.ShapeDtypeStruct((Tp, W), F32),
                   jax.ShapeDtypeStruct((CONV_WIDTH, W), F32), jax.ShapeDtypeStruct((1, W), F32),
                   jax.ShapeDtypeStruct((nb, LANES, LANES), F32), jax.ShapeDtypeStruct((nb, LANES, LANES), F32),
                   jax.ShapeDtypeStruct((1, W), F32), jax.ShapeDtypeStruct((1, W), F32),
                   jax.ShapeDtypeStruct((1, W), F32)),
        grid=(nb,),
        in_specs=[col, col, col, cwspec, vec, mat, mat, vec, vec, vec],
        out_specs=[col, col, cwspec, vec, mat, mat, vec, vec, vec],
        scratch_shapes=[pltpu.VMEM((Tp + 8, LANES), F32), pltpu.VMEM((Tp + 8, LANES), F32),
                        pltpu.VMEM((Tp + 8, LANES), F32), pltpu.VMEM((Tp, LANES), F32),
                        pltpu.VMEM((Tp, LANES), F32), pltpu.VMEM((Tp, LANES), F32),
                        pltpu.VMEM((Tp + 8, LANES), F32)],
        compiler_params=_cparams(),
    )(dy, xr, gr, cw, cb, wa, wx, ba, bx, lam)


def _rope(x, cosm, sin_hi, sin_lo):
    return x * cosm + pltpu.roll(x, QK_ROPE // 2, axis=1) * sin_hi + pltpu.roll(x, LANES - QK_ROPE // 2, axis=1) * sin_lo


def _rope_inverse(x, cosm, sin_hi, sin_lo):
    return x * cosm - pltpu.roll(x, QK_ROPE // 2, axis=1) * sin_hi - pltpu.roll(x, LANES - QK_ROPE // 2, axis=1) * sin_lo


def _mla_proj_fwd(cq, ckv, kr, qg, kg, wuq, wkv, cosm, sin_hi, sin_lo):
    Tp, QL = cq.shape
    KL = ckv.shape[1]
    H = MLA_HEADS
    half = LANES // 2

    def body(cq_ref, ckv_ref, kr_ref, qg_ref, kg_ref, wuq_ref, wkv_ref, c_ref, s1_ref, s2_ref,
             q_ref, k_ref, v_ref):
        tabs = (c_ref[...], s1_ref[...], s2_ref[...])
        lane = lax.broadcasted_iota(jnp.int32, (ROW_TILE, LANES), 1)
        low = lane < half
        cqn, _, _ = _rms_fwd(cq_ref[...], qg_ref[...])
        q = _dot(cqn.astype(BF16), wuq_ref[...])
        ckvn, _, _ = _rms_fwd(ckv_ref[...], kg_ref[...])
        kv = _dot(ckvn.astype(BF16), wkv_ref[...])
        krr = _rope(kr_ref[...], *tabs)
        for hd in range(H):
            sl = slice(hd * LANES, (hd + 1) * LANES)
            q_ref[hd] = _rope(q[:, sl], *tabs).astype(BF16)
            k_ref[hd] = jnp.where(low, kv[:, sl], krr).astype(BF16)
        for p in range(H // 2):
            even = kv[:, 2 * p * LANES:(2 * p + 1) * LANES]
            odd = kv[:, (2 * p + 1) * LANES:(2 * p + 2) * LANES]
            v_ref[:, p * LANES:(p + 1) * LANES] = jnp.where(low, pltpu.roll(even, half, axis=1), odd).astype(BF16)

    heads = pl.BlockSpec((H, ROW_TILE, LANES), lambda i: (0, i, 0))
    return pl.pallas_call(
        body, name="mla_proj_fwd",
        out_shape=(jax.ShapeDtypeStruct((H, Tp, LANES), BF16), jax.ShapeDtypeStruct((H, Tp, LANES), BF16),
                   jax.ShapeDtypeStruct((Tp, H * half), BF16)),
        grid=(Tp // ROW_TILE,),
        in_specs=[_rows(QL), _rows(KL), _rows(LANES), _const((1, QL)), _const((1, KL)),
                  _resident(wuq.shape), _resident(wkv.shape), _rows(LANES), _rows(LANES), _rows(LANES)],
        out_specs=[heads, heads, _rows(H * half)],
        compiler_params=_cparams(),
    )(cq, ckv, kr, qg, kg, wuq, wkv, cosm, sin_hi, sin_lo)


def _mla_proj_bwd(dq, dk, dv, cq, ckv, qg, kg, wuq, wkv, cosm, sin_hi, sin_lo):
    H, Tp, _ = dq.shape
    QL, KL = cq.shape[1], ckv.shape[1]
    half = LANES // 2

    def body(dq_ref, dk_ref, dv_ref, cq_ref, ckv_ref, qg_ref, kg_ref, wuq_ref, wkv_ref,
             c_ref, s1_ref, s2_ref,
             dcq_ref, dckv_ref, dkr_ref, dqa_ref, dkva_ref, cqn_ref, ckvn_ref, dqg_ref, dkg_ref):
        first = pl.program_id(0) == 0
        tabs = (c_ref[...], s1_ref[...], s2_ref[...])
        lane = lax.broadcasted_iota(jnp.int32, (ROW_TILE, LANES), 1)
        low = lane < half
        rope_lanes = jnp.logical_and(lane >= QK_NOPE, lane < QK_NOPE + QK_ROPE)
        dkr = jnp.zeros((ROW_TILE, LANES), F32)
        for hd in range(H):
            sl = slice(hd * LANES, (hd + 1) * LANES)
            dqa_ref[:, sl] = _rope_inverse(dq_ref[hd], *tabs).astype(BF16)
            dkh = dk_ref[hd]
            dvp = dv_ref[:, (hd // 2) * LANES:(hd // 2 + 1) * LANES]
            dvh = pltpu.roll(dvp, half, axis=1) if hd % 2 == 0 else dvp
            dkva_ref[:, sl] = jnp.where(low, dkh, dvh).astype(BF16)
            dkr = dkr + jnp.where(rope_lanes, dkh, 0.0)
        dkr_ref[...] = _rope_inverse(dkr, *tabs)
        qg, kg = qg_ref[...], kg_ref[...]
        cqs, cqn, cqr = _rms_fwd(cq_ref[...], qg)
        cqn_ref[...] = cqs.astype(BF16)
        dcq, dqg = _rms_bwd(cqn, cqr, qg, _dot_nt(dqa_ref[...], wuq_ref[...]))
        dcq_ref[...] = dcq
        cks, ckn, ckr = _rms_fwd(ckv_ref[...], kg)
        ckvn_ref[...] = cks.astype(BF16)
        dckv, dkg = _rms_bwd(ckn, ckr, kg, _dot_nt(dkva_ref[...], wkv_ref[...]))
        dckv_ref[...] = dckv
        _accumulate(dqg_ref, dqg, first)
        _accumulate(dkg_ref, dkg, first)

    heads = pl.BlockSpec((H, ROW_TILE, LANES), lambda i: (0, i, 0))
    return pl.pallas_call(
        body, name="mla_proj_bwd",
        out_shape=(jax.ShapeDtypeStruct((Tp, QL), F32), jax.ShapeDtypeStruct((Tp, KL), F32),
                   jax.ShapeDtypeStruct((Tp, LANES), F32), jax.ShapeDtypeStruct((Tp, H * LANES), BF16),
                   jax.ShapeDtypeStruct((Tp, H * LANES), BF16),
                   jax.ShapeDtypeStruct((Tp, QL), BF16), jax.ShapeDtypeStruct((Tp, KL), BF16),
                   jax.ShapeDtypeStruct((1, QL), F32), jax.ShapeDtypeStruct((1, KL), F32)),
        grid=(Tp // ROW_TILE,),
        in_specs=[heads, heads, _rows(H * half), _rows(QL), _rows(KL), _const((1, QL)), _const((1, KL)),
                  _resident(wuq.shape), _resident(wkv.shape), _rows(LANES), _rows(LANES), _rows(LANES)],
        out_specs=[_rows(QL), _rows(KL), _rows(LANES), _rows(H * LANES), _rows(H * LANES),
                   _rows(QL), _rows(KL), _const((1, QL)), _const((1, KL))],
        compiler_params=_cparams(),
    )(dq, dk, dv, cq, ckv, qg, kg, wuq, wkv, cosm, sin_hi, sin_lo)


_NEG = -1e30
_ATT_SCALE = (QK_NOPE + QK_ROPE) ** -0.5
_ATT_LOG2 = _ATT_SCALE * math.log2(math.e)


def _causal(s, q0, k0, transposed):
    r = lax.broadcasted_iota(jnp.int32, s.shape, 0)
    c = lax.broadcasted_iota(jnp.int32, s.shape, 1)
    keep = (k0 + r <= q0 + c) if transposed else (k0 + c <= q0 + r)
    return jnp.where(keep, s, _NEG)


def _hosted(comm, n_in, n_out, refs, first, last):
    if comm is None:
        return refs, lambda: None
    n = comm.n
    own = refs[:n_in] + refs[n_in + n:n_in + n + n_out]
    comm.bind(refs[n_in:n_in + n], refs[n_in + n + n_out:n_in + 2 * n + n_out], refs[n_in + 2 * n + n_out:])
    pl.when(first)(comm.start)
    return own, lambda: pl.when(last)(comm.finish)


def _attn_fwd(q, k, v, comm=None, comm_arrays=()):
    H, Tp, _ = q.shape
    t = ROW_TILE
    half = LANES // 2
    n_extra = 0 if comm is None else comm.n

    def body(*refs):
        p_id, i = pl.program_id(0), pl.program_id(1)
        first = jnp.logical_and(p_id == 0, i == 0)
        last = jnp.logical_and(p_id == H // 2 - 1, i == Tp // t - 1)
        (q_ref, k_ref, v_ref, y_ref, lse_ref), finish = _hosted(comm, 3, 2, refs, first, last)
        qs = [q_ref[0], q_ref[1]]

        def step(k0, carry, width, masked):
            vv = v_ref[pl.ds(k0, width), :]
            out = []
            for hh in range(2):
                m, l, acc = carry[hh]
                kv = k_ref[hh, pl.ds(k0, width), :]
                s = _dot_nt(qs[hh], kv) * _ATT_LOG2
                if masked:
                    s = _causal(s, i * t, k0, False)
                m_new = jnp.maximum(m, jnp.max(s, axis=-1, keepdims=True))
                alpha = jnp.exp2(m - m_new)
                p = jnp.exp2(s - m_new)
                l = alpha * l + jnp.sum(p, axis=-1, keepdims=True)
                acc = alpha * acc + _dot(p.astype(BF16), vv)
                out.append((m_new, l, acc))
            return tuple(out)

        one = (jnp.full((t, 1), _NEG, F32), jnp.zeros((t, 1), F32), jnp.zeros((t, LANES), F32))
        carry = lax.fori_loop(0, i // 2, lambda jj, c: step(pl.multiple_of(jj * 2 * t, t), c, 2 * t, False), (one, one))
        k0 = pl.multiple_of((i // 2) * 2 * t, t)
        carry = lax.cond(i % 2 == 0, lambda c: step(k0, c, t, True), lambda c: step(k0, c, 2 * t, True), carry)
        outs = []
        for hh in range(2):
            m, l, acc = carry[hh]
            outs.append(acc / l)
            lse_ref[hh] = m + jnp.log2(l)
        lane = lax.broadcasted_iota(jnp.int32, (t, LANES), 1)
        y_ref[...] = jnp.where(lane < half, outs[0], outs[1])
        finish()

    return pl.pallas_call(
        body, name="attn_fwd" if comm is None else "attn_fwd_gather",
        out_shape=(jax.ShapeDtypeStruct((Tp, H * half), F32), jax.ShapeDtypeStruct((H, Tp, 1), F32))
        + (() if comm is None else comm.out_shape),
        grid=(H // 2, Tp // t),
        in_specs=[pl.BlockSpec((2, t, LANES), lambda p, i: (p, i, 0)),
                  pl.BlockSpec((2, Tp, LANES), lambda p, i: (p, 0, 0)),
                  pl.BlockSpec((Tp, LANES), lambda p, i: (0, p))] + [_HBM] * n_extra,
        out_specs=[pl.BlockSpec((t, LANES), lambda p, i: (i, p)),
                   pl.BlockSpec((2, t, 1), lambda p, i: (p, i, 0))] + [_HBM] * n_extra,
        scratch_shapes=[] if comm is None else comm.scratch,
        compiler_params=_cparams(),
    )(q, k, v, *comm_arrays)


def _attn_bwd_q(q, k, v, dy, y, lse):
    H, Tp, _ = q.shape
    t = ROW_TILE
    half = LANES // 2

    def body(q_ref, k_ref, v_ref, dy_ref, y_ref, lse_ref, dq_ref, delta_ref):
        i = pl.program_id(1)
        lane = lax.broadcasted_iota(jnp.int32, (t, LANES), 1)
        qs, dobs, deltas, lses = [], [], [], []
        for hh in range(2):
            mine = (lane < half) if hh == 0 else (lane >= half)
            do = jnp.where(mine, dy_ref[...], 0.0)
            delta = jnp.sum(do * y_ref[...], axis=-1, keepdims=True)
            delta_ref[hh] = delta
            qs.append(q_ref[hh])
            dobs.append(do.astype(BF16))
            deltas.append(delta)
            lses.append(lse_ref[hh])

        def step(k0, dqs, width, masked):
            vv = v_ref[pl.ds(k0, width), :]
            out = []
            for hh in range(2):
                kv = k_ref[hh, pl.ds(k0, width), :]
                s = _dot_nt(qs[hh], kv) * _ATT_LOG2
                if masked:
                    s = _causal(s, i * t, k0, False)
                p = jnp.exp2(s - lses[hh])
                dp = _dot_nt(dobs[hh], vv)
                ds = p * (dp - deltas[hh])
                out.append(dqs[hh] + _dot(ds.astype(BF16), kv))
            return tuple(out)

        zero = jnp.zeros((t, LANES), F32)
        dqs = lax.fori_loop(0, i // 2, lambda jj, c: step(pl.multiple_of(jj * 2 * t, t), c, 2 * t, False), (zero, zero))
        k0 = pl.multiple_of((i // 2) * 2 * t, t)
        dqs = lax.cond(i % 2 == 0, lambda c: step(k0, c, t, True), lambda c: step(k0, c, 2 * t, True), dqs)
        for hh in range(2):
            dq_ref[hh] = dqs[hh] * _ATT_SCALE

    return pl.pallas_call(
        body, name="attn_bwd_q",
        out_shape=(jax.ShapeDtypeStruct((H, Tp, LANES), F32), jax.ShapeDtypeStruct((H, Tp, 1), F32)),
        grid=(H // 2, Tp // t),
        in_specs=[pl.BlockSpec((2, t, LANES), lambda p, i: (p, i, 0)),
                  pl.BlockSpec((2, Tp, LANES), lambda p, i: (p, 0, 0)),
                  pl.BlockSpec((Tp, LANES), lambda p, i: (0, p)),
                  pl.BlockSpec((t, LANES), lambda p, i: (i, p)),
                  pl.BlockSpec((t, LANES), lambda p, i: (i, p)),
                  pl.BlockSpec((2, t, 1), lambda p, i: (p, i, 0))],
        out_specs=[pl.BlockSpec((2, t, LANES), lambda p, i: (p, i, 0)),
                   pl.BlockSpec((2, t, 1), lambda p, i: (p, i, 0))],
        compiler_params=_cparams(),
    )(q, k, v, dy, y, lse)


def _attn_bwd_kv(q, k, v, dy, lse_row, delta_row, comm=None, comm_arrays=()):
    H, Tp, _ = q.shape
    t = ROW_TILE
    half = LANES // 2
    nq = Tp // t
    n_extra = 0 if comm is None else comm.n

    def body(*refs):
        p_id, j = pl.program_id(0), pl.program_id(1)
        first = jnp.logical_and(p_id == 0, j == 0)
        last = jnp.logical_and(p_id == H // 2 - 1, j == nq - 1)
        (q_ref, k_ref, v_ref, dy_ref, lse_ref, delta_ref, dk_ref, dv_ref), finish = _hosted(comm, 6, 2, refs, first, last)
        vv = v_ref[...]
        ks = [k_ref[0], k_ref[1]]

        def step(q0, carry, width, masked):
            cols = pl.ds(pl.multiple_of(q0, LANES), width)
            dyv = dy_ref[pl.ds(q0, width), :]
            lane_w = lax.broadcasted_iota(jnp.int32, (width, LANES), 1)
            out = []
            for hh in range(2):
                dk, dv = carry[hh]
                mine = (lane_w < half) if hh == 0 else (lane_w >= half)
                qv = q_ref[hh, pl.ds(q0, width), :]
                dob = jnp.where(mine, dyv, 0.0).astype(BF16)
                st = _dot_nt(ks[hh], qv) * _ATT_LOG2
                if masked:
                    st = _causal(st, q0, j * t, True)
                pt = jnp.exp2(st - lse_ref[hh, :, cols])
                dv = dv + _dot(pt.astype(BF16), dob)
                dpt = _dot_nt(vv, dob)
                dst = pt * (dpt - delta_ref[hh, :, cols])
                dk = dk + _dot(dst.astype(BF16), qv)
                out.append((dk, dv))
            return tuple(out)

        zero = jnp.zeros((t, LANES), F32)
        init = ((zero, zero), (zero, zero))
        q0 = pl.multiple_of(j * t, t)
        even = (nq - j) % 2 == 0
        carry = lax.cond(even, lambda c: step(q0, c, 2 * t, True), lambda c: step(q0, c, t, True), init)
        rest = pl.multiple_of((j + 1 + even.astype(jnp.int32)) * t, t)
        carry = lax.fori_loop(0, (nq - j - 1) // 2,
                              lambda ii, c: step(pl.multiple_of(rest + ii * 2 * t, t), c, 2 * t, False), carry)
        for hh in range(2):
            dk_ref[hh] = carry[hh][0] * _ATT_SCALE
        dv_ref[...] = carry[0][1] + carry[1][1]
        finish()

    return pl.pallas_call(
        body, name="attn_bwd_kv" if comm is None else "attn_bwd_kv_exchange",
        out_shape=(jax.ShapeDtypeStruct((H, Tp, LANES), F32), jax.ShapeDtypeStruct((Tp, H * half), F32))
        + (() if comm is None else comm.out_shape),
        grid=(H // 2, nq),
        in_specs=[pl.BlockSpec((2, Tp, LANES), lambda p, j: (p, 0, 0)),
                  pl.BlockSpec((2, t, LANES), lambda p, j: (p, j, 0)),
                  pl.BlockSpec((t, LANES), lambda p, j: (j, p)),
                  pl.BlockSpec((Tp, LANES), lambda p, j: (0, p)),
                  pl.BlockSpec((2, 1, Tp), lambda p, j: (p, 0, 0)),
                  pl.BlockSpec((2, 1, Tp), lambda p, j: (p, 0, 0))] + [_HBM] * n_extra,
        out_specs=[pl.BlockSpec((2, t, LANES), lambda p, j: (p, j, 0)),
                   pl.BlockSpec((t, LANES), lambda p, j: (j, p))] + [_HBM] * n_extra,
        scratch_shapes=[] if comm is None else comm.scratch,
        compiler_params=_cparams(),
    )(q, k, v, dy, lse_row, delta_row, *comm_arrays)


def _loss_head(h, target, lo, hi):
    Tp, D = h.shape

    def body(h_ref, t_ref, dh_ref, loss_ref):
        i = pl.program_id(0)
        row = i * ROW_TILE + lax.broadcasted_iota(jnp.int32, (ROW_TILE, 1), 0)
        live = jnp.logical_and(row >= lo, row < hi)
        err = jnp.where(live, h_ref[...] - t_ref[...], 0.0)
        dh_ref[...] = err / D
        part = 0.5 * jnp.sum(jnp.mean(err * err, axis=-1, keepdims=True), axis=0, keepdims=True)
        _accumulate(loss_ref, jnp.broadcast_to(part, (8, LANES)), i == 0)

    return pl.pallas_call(
        body, name="loss_head",
        out_shape=(jax.ShapeDtypeStruct((Tp, D), F32), jax.ShapeDtypeStruct((8, LANES), F32)),
        grid=(Tp // ROW_TILE,),
        in_specs=[_rows(D), _rows(D)],
        out_specs=[_rows(D), _const((8, LANES))],
        compiler_params=_cparams(),
    )(h, target)


def _pack(arrays, row_multiple=8):
    flat = jnp.concatenate([a.reshape(-1).astype(F32) for a in arrays])
    quantum = LANES * row_multiple
    total = -(-flat.shape[0] // quantum) * quantum
    flat = jnp.pad(flat, (0, total - flat.shape[0]))
    return flat.reshape(-1, LANES)


def _unpack(buf, shapes):
    flat = buf.reshape(-1)
    out, off = [], 0
    for shp in shapes:
        n = math.prod(shp)
        out.append(flat[off:off + n].reshape(tuple(shp)))
        off += n
    return out


def _block_diag_blocks(w):
    nh, d, _ = w.shape
    per = LANES // d
    eye = jnp.eye(per, dtype=w.dtype)
    g = w.reshape(nh // per, per, d, d)
    return jnp.einsum('bpij,pq->bpiqj', g, eye).reshape(nh // per, LANES, LANES)


def _block_diag_extract(blocks, d):
    nb = blocks.shape[0]
    per = LANES // d
    return jnp.stack([blocks[b, p * d:(p + 1) * d, p * d:(p + 1) * d] for b in range(nb) for p in range(per)])


def _step(a):
    x = a['x'][0]
    target = a['loss_target'][0]
    seq, D = x.shape
    n_meta = a['meta_tokens'].shape[0]
    T = seq + n_meta
    Tp = -(-T // ROW_TILE) * ROW_TILE
    L = a['ffn1_pre_g'].shape[0]
    W = a['lru_conv_b'].shape[1]
    QL = a['mla_q_norm_g'].shape[1]
    KL = a['mla_kv_norm_g'].shape[1]
    H = MLA_HEADS
    LD = a['lru_w_a'].shape[2]
    n_in = 2 * W + QL + KL
    assert W % LANES == 0 and QL % LANES == 0 and KL % LANES == 0 and LANES % LD == 0
    assert a['mla_w_ukv'].shape[2] == LANES and a['mla_w_uq'].shape[2] == QK_NOPE + QK_ROPE

    ax, ay, ac = (lax.axis_index(n) for n in MESH_AXES)
    me = 4 * ax + 2 * ay + ac
    core = ac.reshape(1).astype(jnp.int32)
    chip = (2 * ax + ay).reshape(1).astype(jnp.int32)

    view = lambda n, t: jnp.swapaxes(t, 1, 2) if n in TRANSPOSED else t
    kinds = {'ffn1_w_gate': 'rows', 'ffn1_w_up': 'rows', 'ffn1_w_down': 'rows', 'w_in': 'w_in', 'mla_w_uq': 'heads',
             'mla_w_ukv': 'cols', 'w_out': 'rows', 'ffn2_w_gate': 'rows', 'ffn2_w_up': 'rows', 'ffn2_w_down': 'rows'}
    layer_shards = lambda l: [view(n, a[n])[l].astype(BF16) for n in BIG]
    assemble = lambda gathered: {n: _assemble(g, kinds[n], n_in) for n, g in zip(BIG, gathered)}
    first = _all_gather(layer_shards(0) + [a[n] for n in SMALL_SHARDED])
    big = [assemble(first[:len(BIG)])]
    gcw, gmeta = first[len(BIG):]
    conv_w = jnp.moveaxis(gcw, 0, 2).reshape(L, CONV_WIDTH, W)
    meta = jnp.moveaxis(gmeta, 0, 1).reshape(n_meta, D)
    wa_blk = jax.vmap(_block_diag_blocks)(a['lru_w_a']).astype(BF16)
    wx_blk = jax.vmap(_block_diag_blocks)(a['lru_w_x']).astype(BF16)
    widths = [W, W, QL, KL, LANES]

    QH = QK_NOPE + QK_ROPE
    pos = jnp.arange(Tp, dtype=F32)
    inv_freq = 1.0 / (ROPE_THETA ** (jnp.arange(0, QK_ROPE, 2, dtype=F32) / QK_ROPE))
    ang = pos[:, None] * inv_freq[None, :]
    cos, sin = jnp.cos(ang), jnp.sin(ang)
    hr = QK_ROPE // 2
    z = lambda n: jnp.zeros((Tp, n), F32)
    cosm = jnp.concatenate([jnp.ones((Tp, QK_NOPE), F32), cos, cos, z(LANES - QH)], axis=1)
    sin_hi = jnp.concatenate([z(QK_NOPE + hr), sin, z(LANES - QH)], axis=1)
    sin_lo = jnp.concatenate([z(QK_NOPE), -sin, z(LANES - QK_NOPE - hr)], axis=1)
    tabs = (cosm, sin_hi, sin_lo)

    vec = lambda name, l: a[name][l][None, :]

    h = jnp.concatenate([meta, x, jnp.zeros((Tp - T, D), F32)], axis=0)
    tpad = jnp.concatenate([jnp.zeros((n_meta, D), F32), target, jnp.zeros((Tp - T, D), F32)], axis=0)
    saved = []
    for l in range(L):
        wl = big[l]
        s = {'h0': h}
        h, s['a1'], s['b1'], s['f1'] = _ffn_fwd(h, vec('ffn1_pre_g', l), vec('ffn1_post_g', l),
                                                wl['ffn1_w_gate'], wl['ffn1_w_up'], wl['ffn1_w_down'])
        s['h1'] = h
        xr, gr, cq, ckv, kr = _mix_in_fwd(h, vec('mix_pre_g', l), wl['w_in'], widths)
        s.update(xr=xr, gr=gr, cq=cq, ckv=ckv)
        y_lru = _rglru_fwd(xr, gr, conv_w[l], vec('lru_conv_b', l), wa_blk[l], wx_blk[l], vec('lru_b_a', l),
                           vec('lru_b_x', l), vec('lru_lambda', l))
        q, k, v = _mla_proj_fwd(cq, ckv, kr, vec('mla_q_norm_g', l), vec('mla_kv_norm_g', l),
                                wl['mla_w_uq'], wl['mla_w_ukv'], *tabs)
        if l + 1 < L:
            nxt = layer_shards(l + 1)
            y_mla, lse, *gathered = _attn_fwd(q, k, v, _Gather(nxt), nxt)
            big.append(assemble(gathered))
        else:
            y_mla, lse = _attn_fwd(q, k, v)
        s.update(q=q, k=k, v=v, lse=lse, y_lru=y_lru, y_mla=y_mla)
        h, s['y'] = _mix_out_fwd(h, y_lru, y_mla, vec('lru_out_g', l), vec('mla_out_g', l), vec('mix_post_g', l),
                                 wl['w_out'])
        s['h2'] = h
        h, s['a2'], s['b2'], s['f2'] = _ffn_fwd(h, vec('ffn2_pre_g', l), vec('ffn2_post_g', l),
                                                wl['ffn2_w_gate'], wl['ffn2_w_up'], wl['ffn2_w_down'])
        saved.append(s)

    dh, loss_tile = _loss_head(h, tpad, n_meta, T)
    loss = lax.psum(loss_tile[0, 0], MESH_AXES)

    gw = {n: [None] * L for n in REPLICATED + ['lru_conv_w']}
    shard2d = {n: view(n, a[n]).shape[1:] for n in BIG}
    results = {n: [lax.empty((L,) + shard2d[n], F32) for _ in range(4)] for n in BIG}

    def ffn_bwd(f, l, gbuf, dh, h_in, a_, b_, f_):
        wl = big[l]
        da, db, act, dfb, dpost = _ffn_bwd_down(dh, a_, b_, f_, vec(f + '_post_g', l), wl[f + '_w_down'])
        dh_in, u, dpre = _ffn_bwd_up(dh, h_in, da, db, vec(f + '_pre_g', l), wl[f + '_w_gate'], wl[f + '_w_up'])
        gbuf[f + '_w_gate'] = _wgrad(da, u, shard2d[f + '_w_gate'], 'rows')
        gbuf[f + '_w_up'] = _wgrad(db, u, shard2d[f + '_w_up'], 'rows')
        gbuf[f + '_w_down'] = _wgrad(act, dfb, shard2d[f + '_w_down'], 'rows')
        gw[f + '_pre_g'][l] = dpre[0]
        gw[f + '_post_g'][l] = dpost[0]
        return dh_in

    def finish_reduce(l, chip_partial, got2):
        for n, cp, g2 in zip(BIG, chip_partial, got2):
            results[n] = _final_sum_adamw(cp, g2, chip, view(n, a[n]), view(n, a['m_' + n]), view(n, a['v_' + n]),
                                          results[n], l)

    pending = None
    for l in reversed(range(L)):
        s = saved[l]
        wl = big[l]
        gbuf = {}
        dh = ffn_bwd('ffn2', l, gbuf, dh, s['h2'], s['a2'], s['b2'], s['f2'])
        dyl, dym, dyb, cat, dlg, dmg, dpg = _mix_out_bwd(
            dh, s['y'], s['y_lru'], s['y_mla'], vec('lru_out_g', l), vec('mla_out_g', l), vec('mix_post_g', l),
            wl['w_out'])
        gbuf['w_out'] = _wgrad(cat, dyb, shard2d['w_out'], 'rows')
        gw['lru_out_g'][l], gw['mla_out_g'][l], gw['mix_post_g'][l] = dlg[0], dmg[0], dpg[0]
        dxr, dgr, dcw, dcb, dwa, dwx, dba, dbx, dlam = _rglru_bwd(
            dyl, s['xr'], s['gr'], conv_w[l], vec('lru_conv_b', l), wa_blk[l], wx_blk[l],
            vec('lru_b_a', l), vec('lru_b_x', l), vec('lru_lambda', l))
        gw['lru_conv_w'][l], gw['lru_conv_b'][l] = dcw, dcb[0]
        gw['lru_w_a'][l], gw['lru_w_x'][l] = _block_diag_extract(dwa, LD), _block_diag_extract(dwx, LD)
        gw['lru_b_a'][l], gw['lru_b_x'][l], gw['lru_lambda'][l] = dba[0], dbx[0], dlam[0]
        dq, delta = _attn_bwd_q(s['q'], s['k'], s['v'], dym, s['y_mla'], s['lse'])
        lse_row, delta_row = s['lse'].reshape(H, 1, Tp), delta.reshape(H, 1, Tp)
        if pending is None:
            dk, dv = _attn_bwd_kv(s['q'], s['k'], s['v'], dym, lse_row, delta_row)
        else:
            dk, dv, *got2 = _attn_bwd_kv(s['q'], s['k'], s['v'], dym, lse_row, delta_row,
                                         _Exchange(pending[1]), pending[1])
            finish_reduce(pending[0], pending[1], got2)
        dcq, dckv, dkr, dqa, dkva, cqn, ckvn, dqg, dkg = _mla_proj_bwd(
            dq, dk, dv, s['cq'], s['ckv'], vec('mla_q_norm_g', l), vec('mla_kv_norm_g', l),
            wl['mla_w_uq'], wl['mla_w_ukv'], *tabs)
        gw['mla_q_norm_g'][l], gw['mla_kv_norm_g'][l] = dqg[0], dkg[0]
        gbuf['mla_w_uq'] = _wgrad(cqn, dqa, shard2d['mla_w_uq'], 'heads')
        gbuf['mla_w_ukv'] = _wgrad(ckvn, dkva, shard2d['mla_w_ukv'], 'cols')
        dh, dz, hn, dmg_ = _mix_in_bwd(dh, s['h1'], [dxr, dgr, dcq, dckv, dkr], vec('mix_pre_g', l), wl['w_in'])
        gw['mix_pre_g'][l] = dmg_[0]
        gbuf['w_in'] = _wgrad(hn, dz, shard2d['w_in'], 'w_in', n_in)
        dh = ffn_bwd('ffn1', l, gbuf, dh, s['h0'], s['a1'], s['b1'], s['f1'])
        parts = [gbuf[n] for n in BIG]
        got = _swap_with_sibling(parts)
        pending = (l, [_chip_partial(p, g, core) for p, g in zip(parts, got)])

    finish_reduce(pending[0], pending[1], _exchange_chips(pending[1]))
    grad_x = dh[n_meta:T][None]
    gsmall = {n: jnp.stack(v) for n, v in gw.items()}
    gsmall['meta_tokens'] = dh[:n_meta]
    grads, delta, new_m, new_v = ({n: view(n, results[n][i]) for n in BIG} for i in range(4))

    small_names = REPLICATED + SMALL_SHARDED
    small_full_shapes = [gsmall[n].shape for n in small_names]
    gsm = _sum_devices(_all_gather([_pack([gsmall[n] for n in small_names])])[0])
    for n, g in zip(small_names, _unpack(gsm, small_full_shapes)):
        if n == 'lru_conv_w':
            g = lax.dynamic_index_in_dim(jnp.moveaxis(g.reshape(L, CONV_WIDTH, N_DEV, W // N_DEV), 2, 0), me, 0, False)
        elif n == 'meta_tokens':
            g = lax.dynamic_index_in_dim(jnp.moveaxis(g.reshape(n_meta, N_DEV, D // N_DEV), 1, 0), me, 0, False)
        grads[n] = g
    for group in (REPLICATED, SMALL_SHARDED):
        shapes = [a[n].shape for n in group]
        packed = [_pack([a[p + n] for n in group]) for p in ('', 'm_', 'v_')]
        packed.append(_pack([grads[n] for n in group]))
        outs = [_unpack(o, shapes) for o in _adamw(*packed)]
        for i, n in enumerate(group):
            delta[n], new_m[n], new_v[n] = outs[0][i], outs[1][i], outs[2][i]

    return (loss, grad_x, *[grads[n] for n in WEIGHTS], *[delta[n] for n in WEIGHTS],
            *[new_m[n] for n in WEIGHTS], *[new_v[n] for n in WEIGHTS])


def kernel(x, meta_tokens, ffn1_pre_g, ffn1_w_gate, ffn1_w_up, ffn1_w_down, ffn1_post_g, mix_pre_g, w_in, lru_conv_w, lru_conv_b, lru_w_a, lru_b_a, lru_w_x, lru_b_x, lru_lambda, mla_q_norm_g, mla_w_uq, mla_kv_norm_g, mla_w_ukv, lru_out_g, mla_out_g, w_out, mix_post_g, ffn2_pre_g, ffn2_w_gate, ffn2_w_up, ffn2_w_down, ffn2_post_g, loss_target, m_meta_tokens, m_ffn1_pre_g, m_ffn1_w_gate, m_ffn1_w_up, m_ffn1_w_down, m_ffn1_post_g, m_mix_pre_g, m_w_in, m_lru_conv_w, m_lru_conv_b, m_lru_w_a, m_lru_b_a, m_lru_w_x, m_lru_b_x, m_lru_lambda, m_mla_q_norm_g, m_mla_w_uq, m_mla_kv_norm_g, m_mla_w_ukv, m_lru_out_g, m_mla_out_g, m_w_out, m_mix_post_g, m_ffn2_pre_g, m_ffn2_w_gate, m_ffn2_w_up, m_ffn2_w_down, m_ffn2_post_g, v_meta_tokens, v_ffn1_pre_g, v_ffn1_w_gate, v_ffn1_w_up, v_ffn1_w_down, v_ffn1_post_g, v_mix_pre_g, v_w_in, v_lru_conv_w, v_lru_conv_b, v_lru_w_a, v_lru_b_a, v_lru_w_x, v_lru_b_x, v_lru_lambda, v_mla_q_norm_g, v_mla_w_uq, v_mla_kv_norm_g, v_mla_w_ukv, v_lru_out_g, v_mla_out_g, v_w_out, v_mix_post_g, v_ffn2_pre_g, v_ffn2_w_gate, v_ffn2_w_up, v_ffn2_w_down, v_ffn2_post_g):
    return _step(dict(locals()))
```
